```python
import math
import jax
import jax.numpy as jnp
from jax import lax
import numpy as np

D_MODEL = 2048
BATCH = 4
SEQ = 2048
DEPTH = 4
DEC_BATCH = 128
DEC_SEQ = 1
PAST_LEN = 16384
PAGE_SIZE = 128

PLE_DIM = 256
N_MIXERS = 2
N_CONV_LAYERS = (DEPTH + N_MIXERS - 1) // N_MIXERS
N_LRU_LAYERS = DEPTH // N_MIXERS
SC_WIDTH = 3
D_RNN = D_MODEL
LRU_BLOCKS = 8
LRU_BW = D_RNN // LRU_BLOCKS
RG_CONV_WIDTH = 4
LRU_C = 8.0
D_FF = -(-8 * D_MODEL // (3 * 256)) * 256
EPS = 1e-6

kernel_name = "hybrid_shortconv_rglru_decoder_step"


def rmsnorm(x, g):
    xf = x.astype(jnp.float32)
    y = xf * lax.rsqrt(jnp.mean(xf * xf, axis=-1, keepdims=True) + EPS)
    return (y * g.astype(jnp.float32)).astype(x.dtype)


def causal_dwconv(u, buf, w, b=None):
    width = w.shape[0]
    t = u.shape[1]
    full = jnp.concatenate([buf.astype(u.dtype), u], axis=1)
    y = full[:, 0:t] * w[0]
    for k in range(1, width):
        y = y + full[:, k:k + t] * w[k]
    if b is not None:
        y = y + b
    return y, full[:, t:]


def short_conv_mixer(h, buf, w_in, w_conv, w_out):
    bcx = h @ w_in
    b_gate, c_gate, xin = jnp.split(bcx, 3, axis=-1)
    conv, new_buf = causal_dwconv(c_gate * xin, buf, w_conv)
    return (b_gate * conv) @ w_out, new_buf


def _lin_comb(c1, c2):
    a1, b1 = c1
    a2, b2 = c2
    return a1 * a2, a2 * b1 + b2


def rglru_mixer(h, conv_buf, h0, w_x, w_gate, conv_w, conv_b, w_a, b_a, w_i, b_i, lam, w_out):
    gate = jax.nn.gelu(h @ w_gate, approximate=True)
    u, new_conv = causal_dwconv(h @ w_x, conv_buf, conv_w, conv_b)
    bsz, t, _ = u.shape
    ub = u.reshape(bsz, t, LRU_BLOCKS, LRU_BW)
    r = jax.nn.sigmoid(jnp.einsum('btnk,nkj->btnj', ub, w_a).reshape(bsz, t, D_RNN) + b_a)
    i = jax.nn.sigmoid(jnp.einsum('btnk,nkj->btnj', ub, w_i).reshape(bsz, t, D_RNN) + b_i)
    log_a = -LRU_C * r.astype(jnp.float32) * jax.nn.softplus(-lam.astype(jnp.float32))
    a = jnp.exp(log_a)
    mult = jnp.sqrt(-jnp.expm1(2.0 * log_a))
    bx = mult * i.astype(jnp.float32) * u.astype(jnp.float32)
    a_cum, b_cum = lax.associative_scan(_lin_comb, (a, bx), axis=1)
    hs = b_cum + a_cum * h0.astype(jnp.float32)[:, None, :]
    y = (gate * hs.astype(h.dtype)) @ w_out
    return y, new_conv, hs[:, -1].astype(h0.dtype)


def trunk(x, p, conv_state, rgc_state, rgh_state,
          mix_norm, ffn_norm, ple_norm, final_norm,
          sc_w_in, sc_w_conv, sc_w_out,
          rg_w_x, rg_w_gate, rg_conv_w, rg_conv_b, rg_w_a, rg_b_a, rg_w_i, rg_b_i, rg_lambda, rg_w_out,
          ffn_w_gate, ffn_w_up, ffn_w_down, ple_w_gate, ple_w_proj):
    new_conv, new_rgc, new_rgh = [], [], []
    for layer in range(DEPTH):
        j = layer // N_MIXERS
        hn = rmsnorm(x, mix_norm[layer])
        if layer % N_MIXERS == 0:
            y, nb = short_conv_mixer(hn, conv_state[j], sc_w_in[j], sc_w_conv[j], sc_w_out[j])
            new_conv.append(nb)
        else:
            y, nc, nh = rglru_mixer(hn, rgc_state[j], rgh_state[j], rg_w_x[j], rg_w_gate[j],
                                    rg_conv_w[j], rg_conv_b[j], rg_w_a[j], rg_b_a[j],
                                    rg_w_i[j], rg_b_i[j], rg_lambda[j], rg_w_out[j])
            new_rgc.append(nc)
            new_rgh.append(nh)
        x = x + y
        hn = rmsnorm(x, ffn_norm[layer])
        x = x + (jax.nn.silu(hn @ ffn_w_gate[layer]) * (hn @ ffn_w_up[layer])) @ ffn_w_down[layer]
        g = jax.nn.sigmoid(rmsnorm(x, ple_norm[layer]) @ ple_w_gate[layer])
        x = x + g * (p[layer] @ ple_w_proj[layer])
    return rmsnorm(x, final_norm), jnp.stack(new_conv), jnp.stack(new_rgc), jnp.stack(new_rgh)


def setup_inputs(seed: int = 0) -> dict:
    key = jax.random.key(seed)
    ks = iter(jax.random.split(key, 40))
    f32 = jnp.float32

    def nrm(shape, scale=1.0):
        return jax.random.normal(next(ks), shape, f32) * scale

    def gain(shape):
        return 1.0 + 0.05 * jax.random.normal(next(ks), shape, f32)

    a_init = jax.random.uniform(next(ks), (N_LRU_LAYERS, D_RNN), f32, 0.9, 0.999)
    return {
        "x_prompt": nrm((BATCH, SEQ, D_MODEL)),
        "x_sample": nrm((DEC_BATCH, DEC_SEQ, D_MODEL)),
        "p_prompt": nrm((DEPTH, BATCH, SEQ, PLE_DIM)),
        "p_sample": nrm((DEPTH, DEC_BATCH, DEC_SEQ, PLE_DIM)),
        "state_conv": nrm((N_CONV_LAYERS, DEC_BATCH, SC_WIDTH - 1, D_MODEL)),
        "state_rg_conv": nrm((N_LRU_LAYERS, DEC_BATCH, RG_CONV_WIDTH - 1, D_RNN)),
        "state_rg_h": nrm((N_LRU_LAYERS, DEC_BATCH, D_RNN), 0.5),
        "mix_norm": gain((DEPTH, D_MODEL)),
        "ffn_norm": gain((DEPTH, D_MODEL)),
        "ple_norm": gain((DEPTH, D_MODEL)),
        "final_norm": gain((D_MODEL,)),
        "sc_w_in": nrm((N_CONV_LAYERS, D_MODEL, 3 * D_MODEL), D_MODEL ** -0.5),
        "sc_w_conv": nrm((N_CONV_LAYERS, SC_WIDTH, D_MODEL), SC_WIDTH ** -0.5),
        "sc_w_out": nrm((N_CONV_LAYERS, D_MODEL, D_MODEL), D_MODEL ** -0.5),
        "rg_w_x": nrm((N_LRU_LAYERS, D_MODEL, D_RNN), D_MODEL ** -0.5),
        "rg_w_gate": nrm((N_LRU_LAYERS, D_MODEL, D_RNN), D_MODEL ** -0.5),
        "rg_conv_w": nrm((N_LRU_LAYERS, RG_CONV_WIDTH, D_RNN), RG_CONV_WIDTH ** -0.5),
        "rg_conv_b": nrm((N_LRU_LAYERS, D_RNN), 0.01),
        "rg_w_a": nrm((N_LRU_LAYERS, LRU_BLOCKS, LRU_BW, LRU_BW), LRU_BW ** -0.5),
        "rg_b_a": nrm((N_LRU_LAYERS, D_RNN), 0.01),
        "rg_w_i": nrm((N_LRU_LAYERS, LRU_BLOCKS, LRU_BW, LRU_BW), LRU_BW ** -0.5),
        "rg_b_i": nrm((N_LRU_LAYERS, D_RNN), 0.01),
        "rg_lambda": jnp.log(a_init / (1.0 - a_init)),
        "rg_w_out": nrm((N_LRU_LAYERS, D_RNN, D_MODEL), D_RNN ** -0.5),
        "ffn_w_gate": nrm((DEPTH, D_MODEL, D_FF), D_MODEL ** -0.5),
        "ffn_w_up": nrm((DEPTH, D_MODEL, D_FF), D_MODEL ** -0.5),
        "ffn_w_down": nrm((DEPTH, D_FF, D_MODEL), D_FF ** -0.5),
        "ple_w_gate": nrm((DEPTH, D_MODEL, D_MODEL), D_MODEL ** -0.5),
        "ple_w_proj": nrm((DEPTH, PLE_DIM, D_MODEL), PLE_DIM ** -0.5),
    }


def reference(x_prompt, x_sample, p_prompt, p_sample, state_conv, state_rg_conv, state_rg_h,
              mix_norm, ffn_norm, ple_norm, final_norm,
              sc_w_in, sc_w_conv, sc_w_out,
              rg_w_x, rg_w_gate, rg_conv_w, rg_conv_b, rg_w_a, rg_b_a, rg_w_i, rg_b_i, rg_lambda, rg_w_out,
              ffn_w_gate, ffn_w_up, ffn_w_down, ple_w_gate, ple_w_proj):
    bsz = x_prompt.shape[0]
    dt = x_prompt.dtype
    conv0 = jnp.zeros((N_CONV_LAYERS, bsz, SC_WIDTH - 1, D_MODEL), dt)
    rgc0 = jnp.zeros((N_LRU_LAYERS, bsz, RG_CONV_WIDTH - 1, D_RNN), dt)
    rgh0 = jnp.zeros((N_LRU_LAYERS, bsz, D_RNN), dt)
    y_prompt, conv_p, rgc_p, rgh_p = trunk(
        x_prompt, p_prompt, conv0, rgc0, rgh0,
        mix_norm, ffn_norm, ple_norm, final_norm, sc_w_in, sc_w_conv, sc_w_out,
        rg_w_x, rg_w_gate, rg_conv_w, rg_conv_b, rg_w_a, rg_b_a, rg_w_i, rg_b_i, rg_lambda, rg_w_out,
        ffn_w_gate, ffn_w_up, ffn_w_down, ple_w_gate, ple_w_proj)
    y_sample, conv_s, rgc_s, rgh_s = trunk(
        x_sample, p_sample, state_conv, state_rg_conv, state_rg_h,
        mix_norm, ffn_norm, ple_norm, final_norm, sc_w_in, sc_w_conv, sc_w_out,
        rg_w_x, rg_w_gate, rg_conv_w, rg_conv_b, rg_w_a, rg_b_a, rg_w_i, rg_b_i, rg_lambda, rg_w_out,
        ffn_w_gate, ffn_w_up, ffn_w_down, ple_w_gate, ple_w_proj)
    return (y_prompt, y_sample, conv_p, conv_s, rgc_p, rgc_s, rgh_p, rgh_s)
```

```python
import functools

import jax
import jax.numpy as jnp
from jax import lax
from jax.experimental import pallas as pl
from jax.experimental.pallas import tpu as pltpu

D_MODEL = 2048
DEPTH = 4
PLE_DIM = 256
SC_WIDTH = 3
RG_CONV_WIDTH = 4
LRU_BW = 256
LRU_C = 8.0
D_FF = 5632
EPS = 1e-6

SUBLANES = 8
ROW_TILE = 1024
MIX_CHUNK = LRU_BW
FFN_CHUNK = 256
PLE_CHUNK = 512
VMEM_LIMIT = 56 * 1024 * 1024

_F32 = jnp.float32
_BF16 = jnp.bfloat16


def _dot(a, b):
    return jnp.dot(a, b, preferred_element_type=_F32)


def _w(ref):
    return ref[...].astype(_BF16)


def _rmsnorm(x, g):
    y = x * lax.rsqrt(jnp.mean(x * x, axis=-1, keepdims=True) + EPS)
    return y * g


def _sigmoid(x):
    return jax.nn.sigmoid(x)


def _gelu_tanh(x):
    c = 0.7978845608028654
    return 0.5 * x * (1.0 + jnp.tanh(c * (x + 0.044715 * (x * x * x))))


def _softplus(x):
    return jnp.maximum(x, 0.0) + jnp.log1p(jnp.exp(-jnp.abs(x)))


def _linear_scan(a, b, rows):
    n = a.shape[0]
    shift = 1
    while shift < n:
        keep = rows >= shift
        a_prev = jnp.where(keep, pltpu.roll(a, shift, 0), 1.0)
        b_prev = jnp.where(keep, pltpu.roll(b, shift, 0), 0.0)
        b = a * b_prev + b
        a = a * a_prev
        shift *= 2
    return a, b


def _start_mixer(x_ref, g_ref, o_ref, hn_ref):
    @pl.when(pl.program_id(1) == 0)
    def _():
        x = x_ref[...]
        hn_ref[...] = _rmsnorm(x, g_ref[...]).astype(_BF16)
        o_ref[...] = x


def _conv_seq_kernel(x_ref, g_ref, wb_ref, wc_ref, wx_ref, cw_ref, wo_ref,
                     o_ref, st_ref, hn_ref, ubuf_ref, carry_ref, *, tm, tiles_per_seq):
    i, j = pl.program_id(0), pl.program_id(1)
    _start_mixer(x_ref, g_ref, o_ref, hn_ref)
    hn = hn_ref[...]
    b_gate = _dot(hn, _w(wb_ref))
    u = _dot(hn, _w(wc_ref)) * _dot(hn, _w(wx_ref))
    first = (i % tiles_per_seq) == 0
    ubuf_ref[0:SUBLANES, :] = jnp.where(first, 0.0, carry_ref[j])
    ubuf_ref[SUBLANES:SUBLANES + tm, :] = u
    cw = cw_ref[...]
    conv = (ubuf_ref[pl.ds(SUBLANES - 2, tm), :] * cw[0:1, :]
            + ubuf_ref[pl.ds(SUBLANES - 1, tm), :] * cw[1:2, :]
            + u * cw[2:3, :])
    carry_ref[j] = ubuf_ref[tm:tm + SUBLANES, :]
    st_ref[...] = ubuf_ref[pl.ds(tm + SUBLANES - (SC_WIDTH - 1), SC_WIDTH - 1), :]
    o_ref[...] += _dot((b_gate * conv).astype(_BF16), _w(wo_ref))


def _conv_step_kernel(x_ref, g_ref, wb_ref, wc_ref, wx_ref, cw_ref, wo_ref, s0_ref, s1_ref,
                      o_ref, u_ref, hn_ref):
    _start_mixer(x_ref, g_ref, o_ref, hn_ref)
    hn = hn_ref[...]
    b_gate = _dot(hn, _w(wb_ref))
    u = _dot(hn, _w(wc_ref)) * _dot(hn, _w(wx_ref))
    cw = cw_ref[...]
    conv = s0_ref[...] * cw[0:1, :] + s1_ref[...] * cw[1:2, :] + u * cw[2:3, :]
    u_ref[...] = u
    o_ref[...] += _dot((b_gate * conv).astype(_BF16), _w(wo_ref))


def _conv_mixer(x, layer, g, w_in, w_conv, w_out, *, tm, seq_len=None, state=None):
    m = x.shape[0]
    tn = MIX_CHUNK
    nj = D_MODEL // tn
    j = layer // 2
    in_specs = [
        pl.BlockSpec((tm, D_MODEL), lambda i, c: (i, 0), pipeline_mode=pl.Buffered(1)),
        pl.BlockSpec((None, 1, D_MODEL), lambda i, c: (layer, 0, 0)),
        pl.BlockSpec((None, D_MODEL, tn), lambda i, c: (j, 0, c)),
        pl.BlockSpec((None, D_MODEL, tn), lambda i, c: (j, 0, nj + c)),
        pl.BlockSpec((None, D_MODEL, tn), lambda i, c: (j, 0, 2 * nj + c)),
        pl.BlockSpec((None, SC_WIDTH, tn), lambda i, c: (j, 0, c)),
        pl.BlockSpec((None, tn, D_MODEL), lambda i, c: (j, c, 0)),
    ]
    args = [x, g, w_in, w_in, w_in, w_conv, w_out]
    x_out = jax.ShapeDtypeStruct((m, D_MODEL), _F32)
    x_spec = pl.BlockSpec((tm, D_MODEL), lambda i, c: (i, 0))
    hn_scratch = pltpu.VMEM((tm, D_MODEL), _BF16)
    if state is None:
        tiles_per_seq = seq_len // tm
        kern = functools.partial(_conv_seq_kernel, tm=tm, tiles_per_seq=tiles_per_seq)
        out_shape = (x_out, jax.ShapeDtypeStruct((m // tm, SC_WIDTH - 1, D_MODEL), _F32))
        out_specs = (x_spec, pl.BlockSpec((None, SC_WIDTH - 1, tn), lambda i, c: (i, 0, c)))
        scratch = [hn_scratch, pltpu.VMEM((tm + SUBLANES, tn), _F32),
                   pltpu.VMEM((nj, SUBLANES, tn), _F32)]
    else:
        kern = _conv_step_kernel
        in_specs += [pl.BlockSpec((tm, tn), lambda i, c: (i, c)),
                     pl.BlockSpec((tm, tn), lambda i, c: (i, nj + c))]
        args += [state, state]
        out_shape = (x_out, jax.ShapeDtypeStruct((m, D_MODEL), _F32))
        out_specs = (x_spec, pl.BlockSpec((tm, tn), lambda i, c: (i, c)))
        scratch = [hn_scratch]
    return pl.pallas_call(
        kern, grid=(m // tm, nj), in_specs=in_specs, out_specs=out_specs, out_shape=out_shape,
        scratch_shapes=scratch,
        compiler_params=pltpu.CompilerParams(
            dimension_semantics=("arbitrary", "arbitrary"), vmem_limit_bytes=VMEM_LIMIT),
        name=f"conv_mixer_l{layer}_m{m}",
    )(*args)


def _lru_gates(u, wa_ref, ba_ref, wi_ref, bi_ref, lam_ref):
    ub = u.astype(_BF16)
    r = _sigmoid(_dot(ub, _w(wa_ref)) + ba_ref[...])
    gate_i = _sigmoid(_dot(ub, _w(wi_ref)) + bi_ref[...])
    log_a = (-LRU_C * r) * _softplus(-lam_ref[...])
    a = jnp.exp(log_a)
    mult = jnp.sqrt(-jnp.tanh(log_a) * (a * a + 1.0))
    return a, mult * gate_i * u


def _lru_seq_kernel(x_ref, g_ref, wg_ref, wx_ref, cw_ref, cb_ref, wa_ref, ba_ref, wi_ref, bi_ref,
                    lam_ref, wo_ref, o_ref, rgc_ref, hl_ref, hn_ref, xbuf_ref, carry_ref, hcarry_ref,
                    *, tm, tiles_per_seq):
    i, j = pl.program_id(0), pl.program_id(1)
    _start_mixer(x_ref, g_ref, o_ref, hn_ref)
    hn = hn_ref[...]
    gate = _gelu_tanh(_dot(hn, _w(wg_ref)))
    xx = _dot(hn, _w(wx_ref))
    first = (i % tiles_per_seq) == 0
    xbuf_ref[0:SUBLANES, :] = jnp.where(first, 0.0, carry_ref[j])
    xbuf_ref[SUBLANES:SUBLANES + tm, :] = xx
    cw = cw_ref[...]
    u = (xbuf_ref[pl.ds(SUBLANES - 3, tm), :] * cw[0:1, :]
         + xbuf_ref[pl.ds(SUBLANES - 2, tm), :] * cw[1:2, :]
         + xbuf_ref[pl.ds(SUBLANES - 1, tm), :] * cw[2:3, :]
         + xx * cw[3:4, :]) + cb_ref[...]
    carry_ref[j] = xbuf_ref[tm:tm + SUBLANES, :]
    rgc_ref[...] = xbuf_ref[pl.ds(tm + SUBLANES - (RG_CONV_WIDTH - 1), RG_CONV_WIDTH - 1), :]
    a, b = _lru_gates(u, wa_ref, ba_ref, wi_ref, bi_ref, lam_ref)
    rows = lax.broadcasted_iota(jnp.int32, a.shape, 0)
    a_cum, b_cum = _linear_scan(a, b, rows)
    h0 = jnp.where(first, 0.0, hcarry_ref[j])
    hs = b_cum + a_cum * h0
    h_last = hs[tm - 1:tm, :]
    hcarry_ref[j] = h_last
    hl_ref[...] = h_last
    o_ref[...] += _dot((gate * hs).astype(_BF16), _w(wo_ref))


def _lru_step_kernel(x_ref, g_ref, wg_ref, wx_ref, cw_ref, cb_ref, wa_ref, ba_ref, wi_ref, bi_ref,
                     lam_ref, wo_ref, s0_ref, s1_ref, s2_ref, h0_ref, o_ref, xx_ref, h_ref, hn_ref):
    _start_mixer(x_ref, g_ref, o_ref, hn_ref)
    hn = hn_ref[...]
    gate = _gelu_tanh(_dot(hn, _w(wg_ref)))
    xx = _dot(hn, _w(wx_ref))
    cw = cw_ref[...]
    u = (s0_ref[...] * cw[0:1, :] + s1_ref[...] * cw[1:2, :] + s2_ref[...] * cw[2:3, :]
         + xx * cw[3:4, :]) + cb_ref[...]
    a, b = _lru_gates(u, wa_ref, ba_ref, wi_ref, bi_ref, lam_ref)
    hs = b + a * h0_ref[...]
    xx_ref[...] = xx
    h_ref[...] = hs
    o_ref[...] += _dot((gate * hs).astype(_BF16), _w(wo_ref))


def _lru_mixer(x, layer, g, w_x, w_gate, conv_w, conv_b, w_a, b_a, w_i, b_i, lam, w_out,
               *, tm, seq_len=None, state=None, h0=None):
    m = x.shape[0]
    tn = MIX_CHUNK
    nj = D_MODEL // tn
    j = layer // 2
    in_specs = [
        pl.BlockSpec((tm, D_MODEL), lambda i, c: (i, 0), pipeline_mode=pl.Buffered(1)),
        pl.BlockSpec((None, 1, D_MODEL), lambda i, c: (layer, 0, 0)),
        pl.BlockSpec((None, D_MODEL, tn), lambda i, c: (j, 0, c)),
        pl.BlockSpec((None, D_MODEL, tn), lambda i, c: (j, 0, c)),
        pl.BlockSpec((None, RG_CONV_WIDTH, tn), lambda i, c: (j, 0, c)),
        pl.BlockSpec((None, 1, tn), lambda i, c: (j, 0, c)),
        pl.BlockSpec((None, None, LRU_BW, LRU_BW), lambda i, c: (j, c, 0, 0)),
        pl.BlockSpec((None, 1, tn), lambda i, c: (j, 0, c)),
        pl.BlockSpec((None, None, LRU_BW, LRU_BW), lambda i, c: (j, c, 0, 0)),
        pl.BlockSpec((None, 1, tn), lambda i, c: (j, 0, c)),
        pl.BlockSpec((None, 1, tn), lambda i, c: (j, 0, c)),
        pl.BlockSpec((None, tn, D_MODEL), lambda i, c: (j, c, 0)),
    ]
    args = [x, g, w_gate, w_x, conv_w, conv_b, w_a, b_a, w_i, b_i, lam, w_out]
    x_out = jax.ShapeDtypeStruct((m, D_MODEL), _F32)
    x_spec = pl.BlockSpec((tm, D_MODEL), lambda i, c: (i, 0))
    hn_scratch = pltpu.VMEM((tm, D_MODEL), _BF16)
    if state is None:
        tiles_per_seq = seq_len // tm
        nt = m // tm
        kern = functools.partial(_lru_seq_kernel, tm=tm, tiles_per_seq=tiles_per_seq)
        out_shape = (x_out, jax.ShapeDtypeStruct((nt, RG_CONV_WIDTH - 1, D_MODEL), _F32),
                     jax.ShapeDtypeStruct((nt, 1, D_MODEL), _F32))
        out_specs = (x_spec,
                     pl.BlockSpec((None, RG_CONV_WIDTH - 1, tn), lambda i, c: (i, 0, c)),
                     pl.BlockSpec((None, 1, tn), lambda i, c: (i, 0, c)))
        scratch = [hn_scratch, pltpu.VMEM((tm + SUBLANES, tn), _F32),
                   pltpu.VMEM((nj, SUBLANES, tn), _F32), pltpu.VMEM((nj, 1, tn), _F32)]
    else:
        kern = _lru_step_kernel
        in_specs += [pl.BlockSpec((tm, tn), lambda i, c: (i, c)),
                     pl.BlockSpec((tm, tn), lambda i, c: (i, nj + c)),
                     pl.BlockSpec((tm, tn), lambda i, c: (i, 2 * nj + c)),
                     pl.BlockSpec((tm, tn), lambda i, c: (i, c))]
        args += [state, state, state, h0]
        out_shape = (x_out, jax.ShapeDtypeStruct((m, D_MODEL), _F32),
                     jax.ShapeDtypeStruct((m, D_MODEL), _F32))
        col_spec = pl.BlockSpec((tm, tn), lambda i, c: (i, c))
        out_specs = (x_spec, col_spec, col_spec)
        scratch = [hn_scratch]
    return pl.pallas_call(
        kern, grid=(m // tm, nj), in_specs=in_specs, out_specs=out_specs, out_shape=out_shape,
        scratch_shapes=scratch,
        compiler_params=pltpu.CompilerParams(
            dimension_semantics=("arbitrary", "arbitrary"), vmem_limit_bytes=VMEM_LIMIT),
        name=f"lru_mixer_l{layer}_m{m}",
    )(*args)


def _ffn_kernel(x_ref, g_ref, wg_ref, wu_ref, wd_ref, o_ref, hn_ref):
    _start_mixer(x_ref, g_ref, o_ref, hn_ref)
    hn = hn_ref[...]
    gt = _dot(hn, _w(wg_ref))
    up = _dot(hn, _w(wu_ref))
    h = (gt * _sigmoid(gt)) * up
    o_ref[...] += _dot(h.astype(_BF16), _w(wd_ref))


def _ffn(x, layer, g, w_gate, w_up, w_down, *, tm):
    m = x.shape[0]
    tf = FFN_CHUNK
    return pl.pallas_call(
        _ffn_kernel, grid=(m // tm, D_FF // tf),
        in_specs=[
            pl.BlockSpec((tm, D_MODEL), lambda i, c: (i, 0), pipeline_mode=pl.Buffered(1)),
            pl.BlockSpec((None, 1, D_MODEL), lambda i, c: (layer, 0, 0)),
            pl.BlockSpec((None, D_MODEL, tf), lambda i, c: (layer, 0, c)),
            pl.BlockSpec((None, D_MODEL, tf), lambda i, c: (layer, 0, c)),
            pl.BlockSpec((None, tf, D_MODEL), lambda i, c: (layer, c, 0)),
        ],
        out_specs=pl.BlockSpec((tm, D_MODEL), lambda i, c: (i, 0)),
        out_shape=jax.ShapeDtypeStruct((m, D_MODEL), _F32),
        scratch_shapes=[pltpu.VMEM((tm, D_MODEL), _BF16)],
        compiler_params=pltpu.CompilerParams(
            dimension_semantics=("arbitrary", "arbitrary"), vmem_limit_bytes=VMEM_LIMIT),
        name=f"ffn_l{layer}_m{m}",
    )(x, g, w_gate, w_up, w_down)


def _ple_kernel(x_ref, g_ref, wg_ref, p_ref, wp_ref, gf_ref, o_ref, hn_ref, *, tn, final):
    c = pl.program_id(1)

    @pl.when(c == 0)
    def _():
        hn_ref[...] = _rmsnorm(x_ref[...], g_ref[...]).astype(_BF16)

    cols = pl.ds(pl.multiple_of(c * tn, tn), tn)
    gate = _sigmoid(_dot(hn_ref[...], _w(wg_ref)))
    proj = _dot(p_ref[...].astype(_BF16), _w(wp_ref))
    o_ref[:, cols] = x_ref[:, cols] + gate * proj
    if final:
        @pl.when(c == pl.num_programs(1) - 1)
        def _():
            o_ref[...] = _rmsnorm(o_ref[...], gf_ref[...])


def _ple(x, p, layer, g, w_gate, w_proj, g_final, *, tm):
    m = x.shape[0]
    tn = PLE_CHUNK
    kern = functools.partial(_ple_kernel, tn=tn, final=(layer == DEPTH - 1))
    return pl.pallas_call(
        kern, grid=(m // tm, D_MODEL // tn),
        in_specs=[
            pl.BlockSpec((tm, D_MODEL), lambda i, c: (i, 0), pipeline_mode=pl.Buffered(1)),
            pl.BlockSpec((None, 1, D_MODEL), lambda i, c: (layer, 0, 0)),
            pl.BlockSpec((None, D_MODEL, tn), lambda i, c: (layer, 0, c)),
            pl.BlockSpec((None, tm, PLE_DIM), lambda i, c: (layer, i, 0)),
            pl.BlockSpec((None, PLE_DIM, tn), lambda i, c: (layer, 0, c)),
            pl.BlockSpec((1, D_MODEL), lambda i, c: (0, 0)),
        ],
        out_specs=pl.BlockSpec((tm, D_MODEL), lambda i, c: (i, 0)),
        out_shape=jax.ShapeDtypeStruct((m, D_MODEL), _F32),
        scratch_shapes=[pltpu.VMEM((tm, D_MODEL), _BF16)],
        compiler_params=pltpu.CompilerParams(
            dimension_semantics=("arbitrary", "arbitrary"), vmem_limit_bytes=VMEM_LIMIT),
        name=f"ple_l{layer}_m{m}",
    )(x, g, w_gate, p, w_proj, g_final)


def _trunk(x, p, weights, *, tm, seq_len=None, conv_state=None, rgc_state=None, rgh_state=None):
    (mix_norm, ffn_norm, ple_norm, final_norm, sc_w_in, sc_w_conv, sc_w_out,
     rg_w_x, rg_w_gate, rg_conv_w, rg_conv_b, rg_w_a, rg_b_a, rg_w_i, rg_b_i, rg_lambda, rg_w_out,
     ffn_w_gate, ffn_w_up, ffn_w_down, ple_w_gate, ple_w_proj) = weights
    m = x.shape[0]
    step = seq_len is None
    if not step:
        tiles_per_seq = seq_len // tm
        last_tile = slice(tiles_per_seq - 1, None, tiles_per_seq)
    new_conv, new_rgc, new_rgh = [], [], []
    for layer in range(DEPTH):
        j = layer // 2
        if layer % 2 == 0:
            if step:
                st = conv_state[j]
                x, u = _conv_mixer(x, layer, mix_norm, sc_w_in, sc_w_conv, sc_w_out, tm=tm,
                                   state=st.reshape(m, (SC_WIDTH - 1) * D_MODEL))
                new_conv.append(jnp.concatenate([st[:, 1:], u[:, None, :]], axis=1))
            else:
                x, st = _conv_mixer(x, layer, mix_norm, sc_w_in, sc_w_conv, sc_w_out, tm=tm,
                                    seq_len=seq_len)
                new_conv.append(st[last_tile])
        else:
            lru_w = (rg_w_x, rg_w_gate, rg_conv_w, rg_conv_b, rg_w_a, rg_b_a, rg_w_i, rg_b_i,
                     rg_lambda, rg_w_out)
            if step:
                st = rgc_state[j]
                x, xx, h = _lru_mixer(x, layer, mix_norm, *lru_w, tm=tm,
                                      state=st.reshape(m, (RG_CONV_WIDTH - 1) * D_MODEL),
                                      h0=rgh_state[j])
                new_rgc.append(jnp.concatenate([st[:, 1:], xx[:, None, :]], axis=1))
                new_rgh.append(h)
            else:
                x, st, h = _lru_mixer(x, layer, mix_norm, *lru_w, tm=tm, seq_len=seq_len)
                new_rgc.append(st[last_tile])
                new_rgh.append(h[last_tile, 0, :])
        x = _ffn(x, layer, ffn_norm, ffn_w_gate, ffn_w_up, ffn_w_down, tm=tm)
        x = _ple(x, p, layer, ple_norm, ple_w_gate, ple_w_proj, final_norm, tm=tm)
    return x, jnp.stack(new_conv), jnp.stack(new_rgc), jnp.stack(new_rgh)


def kernel(x_prompt, x_sample, p_prompt, p_sample, state_conv, state_rg_conv, state_rg_h, mix_norm, ffn_norm, ple_norm, final_norm, sc_w_in, sc_w_conv, sc_w_out, rg_w_x, rg_w_gate, rg_conv_w, rg_conv_b, rg_w_a, rg_b_a, rg_w_i, rg_b_i, rg_lambda, rg_w_out, ffn_w_gate, ffn_w_up, ffn_w_down, ple_w_gate, ple_w_proj):
    bsz, seq, _ = x_prompt.shape
    dec = x_sample.shape[0]
    def rows(v):
        return v.reshape(v.shape[0], 1, v.shape[1])

    weights = (rows(mix_norm), rows(ffn_norm), rows(ple_norm), final_norm.reshape(1, D_MODEL),
               sc_w_in, sc_w_conv, sc_w_out, rg_w_x, rg_w_gate, rg_conv_w, rows(rg_conv_b),
               rg_w_a, rows(rg_b_a), rg_w_i, rows(rg_b_i), rows(rg_lambda),
               rg_w_out, ffn_w_gate, ffn_w_up, ffn_w_down, ple_w_gate, ple_w_proj)
    y_p, conv_p, rgc_p, rgh_p = _trunk(
        x_prompt.reshape(bsz * seq, D_MODEL), p_prompt.reshape(DEPTH, bsz * seq, PLE_DIM), weights,
        tm=ROW_TILE, seq_len=seq)
    y_s, conv_s, rgc_s, rgh_s = _trunk(
        x_sample.reshape(dec, D_MODEL), p_sample.reshape(DEPTH, dec, PLE_DIM), weights,
        tm=dec, conv_state=state_conv, rgc_state=state_rg_conv, rgh_state=state_rg_h)
    return (y_p.reshape(bsz, seq, D_MODEL), y_s.reshape(dec, 1, D_MODEL),
            conv_p, conv_s, rgc_p, rgc_s, rgh_p, rgh_s)
```

```python
import functools

import jax
import jax.numpy as jnp
from jax import lax
from jax.experimental import pallas as pl
from jax.experimental.pallas import tpu as pltpu

D_MODEL = 2048
DEPTH = 4
PLE_DIM = 256
SC_WIDTH = 3
RG_CONV_WIDTH = 4
LRU_BW = 256
LRU_C = 8.0
D_FF = 5632
EPS = 1e-6

SUBLANES = 8
MXU_DIM = 256
ROW_TILE = 1024
ROW_SUBBLOCKS = 2
MIX_CHUNK = LRU_BW
FFN_CHUNK = 256
PLE_CHUNK = 256
VMEM_LIMIT = 60 * 1024 * 1024

_F32 = jnp.float32
_BF16 = jnp.bfloat16


def _dot(a, b):
    return jnp.dot(a, b, preferred_element_type=_F32)


def _w(ref):
    return ref[...].astype(_BF16)


def _rmsnorm(x, g):
    y = x * lax.rsqrt(jnp.mean(x * x, axis=-1, keepdims=True) + EPS)
    return y * g


def _sigmoid(x):
    return jax.nn.sigmoid(x)


def _gelu_tanh(x):
    c = 0.7978845608028654
    return 0.5 * x * (1.0 + jnp.tanh(c * (x + 0.044715 * (x * x * x))))


def _softplus(x):
    return jnp.maximum(x, 0.0) + jnp.log1p(jnp.exp(-jnp.abs(x)))


def _interleave_time(x, rs):
    *lead, t, c = x.shape
    y = x.reshape(*lead, t // rs, SUBLANES, rs // SUBLANES, c)
    return jnp.swapaxes(y, -2, -3).reshape(x.shape)


def _deinterleave_time(x, rs):
    *lead, t, c = x.shape
    y = x.reshape(*lead, t // rs, rs // SUBLANES, SUBLANES, c)
    return jnp.swapaxes(y, -2, -3).reshape(x.shape)


def _time_shifts(buf_ref, vals, prev_tail, width):
    rs = vals.shape[0]
    head = (width - 1) * SUBLANES
    tail = vals[rs - head:rs, :]
    sub = lax.broadcasted_iota(jnp.int32, (SUBLANES, vals.shape[1]), 0)
    for v in range(width - 1):
        grp = slice(v * SUBLANES, (v + 1) * SUBLANES)
        mixed = jnp.where(sub == SUBLANES - 1, prev_tail[grp, :], tail[grp, :])
        buf_ref[grp, :] = pltpu.roll(mixed, 1, 0)
    buf_ref[head:head + rs, :] = vals
    shifted = [buf_ref[head - k * SUBLANES:head - k * SUBLANES + rs, :] for k in range(width - 1, 0, -1)]
    return shifted, tail


def _interleaved_scan(a, b, h_in, hbuf_ref, pbuf_ref):
    rs, tn = a.shape
    groups = rs // SUBLANES
    h = b[0:SUBLANES, :]
    p = a[0:SUBLANES, :]
    hbuf_ref[0:SUBLANES, :] = h
    pbuf_ref[0:SUBLANES, :] = p
    for q in range(1, groups):
        grp = slice(q * SUBLANES, (q + 1) * SUBLANES)
        h = a[grp, :] * h + b[grp, :]
        p = a[grp, :] * p
        hbuf_ref[grp, :] = h
        pbuf_ref[grp, :] = p
    sub = lax.broadcasted_iota(jnp.int32, (SUBLANES, tn), 0)
    carry_s = h_in
    carry = jnp.broadcast_to(h_in, (SUBLANES, tn))
    for s in range(1, SUBLANES):
        carry_s = p[s - 1:s, :] * carry_s + h[s - 1:s, :]
        carry = jnp.where(sub == s, carry_s, carry)
    h_out = p[SUBLANES - 1:SUBLANES, :] * carry_s + h[SUBLANES - 1:SUBLANES, :]
    hs = hbuf_ref[...] + pbuf_ref[...] * jnp.concatenate([carry] * groups, axis=0)
    return hs, h_out


def _row_subblock(tm):
    return tm // ROW_SUBBLOCKS if tm >= ROW_SUBBLOCKS * MXU_DIM else tm


def _start_mixer(x_ref, g_ref, o_ref, hn_ref):
    @pl.when(pl.program_id(1) == 0)
    def _():
        x = x_ref[...]
        hn_ref[...] = _rmsnorm(x, g_ref[...]).astype(_BF16)
        o_ref[...] = x


def _conv_seq_kernel(x_ref, g_ref, wb_ref, wc_ref, wx_ref, cw_ref, wo_ref,
                     o_ref, st_ref, hn_ref, ubuf_ref, carry_ref, *, tm, tiles_per_seq):
    i, j = pl.program_id(0), pl.program_id(1)
    _start_mixer(x_ref, g_ref, o_ref, hn_ref)
    wb, wc, wx, wo, cw = _w(wb_ref), _w(wc_ref), _w(wx_ref), _w(wo_ref), cw_ref[...]
    first = (i % tiles_per_seq) == 0
    tail = jnp.where(first, 0.0, carry_ref[j])
    rs = _row_subblock(tm)
    ups = []
    for r0 in range(0, tm, rs):
        hn = hn_ref[r0:r0 + rs, :]
        ups.append((_dot(hn, wb), _dot(hn, wc), _dot(hn, wx)))
    for k, (b_gate, c_gate, xin) in enumerate(ups):
        u = c_gate * xin
        (u2, u1), tail = _time_shifts(ubuf_ref.at[k], u, tail, SC_WIDTH)
        conv = u2 * cw[0:1, :] + u1 * cw[1:2, :] + u * cw[2:3, :]
        o_ref[k * rs:(k + 1) * rs, :] += _dot((b_gate * conv).astype(_BF16), wo)
    carry_ref[j] = tail
    st_ref[...] = tail


def _conv_step_kernel(x_ref, g_ref, wb_ref, wc_ref, wx_ref, cw_ref, wo_ref, s0_ref, s1_ref,
                      o_ref, u_ref, hn_ref):
    _start_mixer(x_ref, g_ref, o_ref, hn_ref)
    hn = hn_ref[...]
    b_gate = _dot(hn, _w(wb_ref))
    u = _dot(hn, _w(wc_ref)) * _dot(hn, _w(wx_ref))
    cw = cw_ref[...]
    conv = s0_ref[...] * cw[0:1, :] + s1_ref[...] * cw[1:2, :] + u * cw[2:3, :]
    u_ref[...] = u
    o_ref[...] += _dot((b_gate * conv).astype(_BF16), _w(wo_ref))


def _conv_mixer(x, layer, g, w_in, w_conv, w_out, *, tm, seq_len=None, state=None):
    m = x.shape[0]
    tn = MIX_CHUNK
    nj = D_MODEL // tn
    j = layer // 2
    in_specs = [
        pl.BlockSpec((tm, D_MODEL), lambda i, c: (i, 0)),
        pl.BlockSpec((None, 1, D_MODEL), lambda i, c: (layer, 0, 0)),
        pl.BlockSpec((None, D_MODEL, tn), lambda i, c: (j, 0, c)),
        pl.BlockSpec((None, D_MODEL, tn), lambda i, c: (j, 0, nj + c)),
        pl.BlockSpec((None, D_MODEL, tn), lambda i, c: (j, 0, 2 * nj + c)),
        pl.BlockSpec((None, SC_WIDTH, tn), lambda i, c: (j, 0, c)),
        pl.BlockSpec((None, tn, D_MODEL), lambda i, c: (j, c, 0)),
    ]
    args = [x, g, w_in, w_in, w_in, w_conv, w_out]
    x_out = jax.ShapeDtypeStruct((m, D_MODEL), _F32)
    x_spec = pl.BlockSpec((tm, D_MODEL), lambda i, c: (i, 0))
    hn_scratch = pltpu.VMEM((tm, D_MODEL), _BF16)
    if state is None:
        tiles_per_seq = seq_len // tm
        kern = functools.partial(_conv_seq_kernel, tm=tm, tiles_per_seq=tiles_per_seq)
        head = (SC_WIDTH - 1) * SUBLANES
        rs = _row_subblock(tm)
        out_shape = (x_out, jax.ShapeDtypeStruct((m // tm, head, D_MODEL), _F32))
        out_specs = (x_spec, pl.BlockSpec((None, head, tn), lambda i, c: (i, 0, c)))
        scratch = [hn_scratch, pltpu.VMEM((tm // rs, head + rs, tn), _F32),
                   pltpu.VMEM((nj, head, tn), _F32)]
    else:
        kern = _conv_step_kernel
        in_specs += [pl.BlockSpec((tm, tn), lambda i, c: (i, c)),
                     pl.BlockSpec((tm, tn), lambda i, c: (i, nj + c))]
        args += [state, state]
        out_shape = (x_out, jax.ShapeDtypeStruct((m, D_MODEL), _F32))
        out_specs = (x_spec, pl.BlockSpec((tm, tn), lambda i, c: (i, c)))
        scratch = [hn_scratch]
    return pl.pallas_call(
        kern, grid=(m // tm, nj), in_specs=in_specs, out_specs=out_specs, out_shape=out_shape,
        scratch_shapes=scratch,
        compiler_params=pltpu.CompilerParams(
            dimension_semantics=("arbitrary", "arbitrary"), vmem_limit_bytes=VMEM_LIMIT),
        name=f"conv_mixer_l{layer}_m{m}",
    )(*args)


def _lru_gate_params(wa_ref, ba_ref, wi_ref, bi_ref, lam_ref):
    return _w(wa_ref), ba_ref[...], _w(wi_ref), bi_ref[...], _softplus(-lam_ref[...])


def _lru_gates(u, wa, ba, wi, bi, sp_neg_lam):
    ub = u.astype(_BF16)
    r = _sigmoid(_dot(ub, wa) + ba)
    gate_i = _sigmoid(_dot(ub, wi) + bi)
    log_a = (-LRU_C * r) * sp_neg_lam
    a = jnp.exp(log_a)
    mult = jnp.sqrt(-jnp.tanh(log_a) * (a * a + 1.0))
    return a, mult * gate_i * u


def _lru_seq_kernel(x_ref, g_ref, wg_ref, wx_ref, cw_ref, cb_ref, wa_ref, ba_ref, wi_ref, bi_ref,
                    lam_ref, wo_ref, o_ref, rgc_ref, hl_ref, hn_ref, xbuf_ref, hbuf_ref, pbuf_ref,
                    carry_ref, hcarry_ref,
                    *, tm, tiles_per_seq):
    i, j = pl.program_id(0), pl.program_id(1)
    _start_mixer(x_ref, g_ref, o_ref, hn_ref)
    wg, wx, wo, cw = _w(wg_ref), _w(wx_ref), _w(wo_ref), cw_ref[...]
    gate_params = _lru_gate_params(wa_ref, ba_ref, wi_ref, bi_ref, lam_ref)
    first = (i % tiles_per_seq) == 0
    tail = jnp.where(first, 0.0, carry_ref[j])
    h_state = jnp.where(first, 0.0, hcarry_ref[j])
    rs = _row_subblock(tm)
    nsub = tm // rs

    def up_projection(k):
        hn = hn_ref[k * rs:(k + 1) * rs, :]
        return _dot(hn, wg), _dot(hn, wx)

    ups = {k: up_projection(k) for k in range(min(2, nsub))}
    for k in range(nsub):
        gate_pre, xx = ups.pop(k)
        if k + 2 < nsub:
            ups[k + 2] = up_projection(k + 2)
        gate = _gelu_tanh(gate_pre)
        (x3, x2, x1), tail = _time_shifts(xbuf_ref.at[k], xx, tail, RG_CONV_WIDTH)
        u = (x3 * cw[0:1, :] + x2 * cw[1:2, :] + x1 * cw[2:3, :] + xx * cw[3:4, :]) + cb_ref[...]
        a, b = _lru_gates(u, *gate_params)
        hs, h_state = _interleaved_scan(a, b, h_state, hbuf_ref.at[k], pbuf_ref.at[k])
        o_ref[k * rs:(k + 1) * rs, :] += _dot((gate * hs).astype(_BF16), wo)
    carry_ref[j] = tail
    rgc_ref[...] = tail
    hcarry_ref[j] = h_state
    hl_ref[...] = h_state


def _lru_step_kernel(x_ref, g_ref, wg_ref, wx_ref, cw_ref, cb_ref, wa_ref, ba_ref, wi_ref, bi_ref,
                     lam_ref, wo_ref, s0_ref, s1_ref, s2_ref, h0_ref, o_ref, xx_ref, h_ref, hn_ref):
    _start_mixer(x_ref, g_ref, o_ref, hn_ref)
    hn = hn_ref[...]
    gate = _gelu_tanh(_dot(hn, _w(wg_ref)))
    xx = _dot(hn, _w(wx_ref))
    cw = cw_ref[...]
    u = (s0_ref[...] * cw[0:1, :] + s1_ref[...] * cw[1:2, :] + s2_ref[...] * cw[2:3, :]
         + xx * cw[3:4, :]) + cb_ref[...]
    a, b = _lru_gates(u, *_lru_gate_params(wa_ref, ba_ref, wi_ref, bi_ref, lam_ref))
    hs = b + a * h0_ref[...]
    xx_ref[...] = xx
    h_ref[...] = hs
    o_ref[...] += _dot((gate * hs).astype(_BF16), _w(wo_ref))


def _lru_mixer(x, layer, g, w_x, w_gate, conv_w, conv_b, w_a, b_a, w_i, b_i, lam, w_out,
               *, tm, seq_len=None, state=None, h0=None):
    m = x.shape[0]
    tn = MIX_CHUNK
    nj = D_MODEL // tn
    j = layer // 2
    in_specs = [
        pl.BlockSpec((tm, D_MODEL), lambda i, c: (i, 0)),
        pl.BlockSpec((None, 1, D_MODEL), lambda i, c: (layer, 0, 0)),
        pl.BlockSpec((None, D_MODEL, tn), lambda i, c: (j, 0, c)),
        pl.BlockSpec((None, D_MODEL, tn), lambda i, c: (j, 0, c)),
        pl.BlockSpec((None, RG_CONV_WIDTH, tn), lambda i, c: (j, 0, c)),
        pl.BlockSpec((None, 1, tn), lambda i, c: (j, 0, c)),
        pl.BlockSpec((None, None, LRU_BW, LRU_BW), lambda i, c: (j, c, 0, 0)),
        pl.BlockSpec((None, 1, tn), lambda i, c: (j, 0, c)),
        pl.BlockSpec((None, None, LRU_BW, LRU_BW), lambda i, c: (j, c, 0, 0)),
        pl.BlockSpec((None, 1, tn), lambda i, c: (j, 0, c)),
        pl.BlockSpec((None, 1, tn), lambda i, c: (j, 0, c)),
        pl.BlockSpec((None, tn, D_MODEL), lambda i, c: (j, c, 0)),
    ]
    args = [x, g, w_gate, w_x, conv_w, conv_b, w_a, b_a, w_i, b_i, lam, w_out]
    x_out = jax.ShapeDtypeStruct((m, D_MODEL), _F32)
    x_spec = pl.BlockSpec((tm, D_MODEL), lambda i, c: (i, 0))
    hn_scratch = pltpu.VMEM((tm, D_MODEL), _BF16)
    if state is None:
        tiles_per_seq = seq_len // tm
        nt = m // tm
        kern = functools.partial(_lru_seq_kernel, tm=tm, tiles_per_seq=tiles_per_seq)
        head = (RG_CONV_WIDTH - 1) * SUBLANES
        rs = _row_subblock(tm)
        out_shape = (x_out, jax.ShapeDtypeStruct((nt, head, D_MODEL), _F32),
                     jax.ShapeDtypeStruct((nt, 1, D_MODEL), _F32))
        out_specs = (x_spec,
                     pl.BlockSpec((None, head, tn), lambda i, c: (i, 0, c)),
                     pl.BlockSpec((None, 1, tn), lambda i, c: (i, 0, c)))
        scratch = [hn_scratch, pltpu.VMEM((tm // rs, head + rs, tn), _F32),
                   pltpu.VMEM((tm // rs, rs, tn), _F32), pltpu.VMEM((tm // rs, rs, tn), _F32),
                   pltpu.VMEM((nj, head, tn), _F32), pltpu.VMEM((nj, 1, tn), _F32)]
    else:
        kern = _lru_step_kernel
        in_specs += [pl.BlockSpec((tm, tn), lambda i, c: (i, c)),
                     pl.BlockSpec((tm, tn), lambda i, c: (i, nj + c)),
                     pl.BlockSpec((tm, tn), lambda i, c: (i, 2 * nj + c)),
                     pl.BlockSpec((tm, tn), lambda i, c: (i, c))]
        args += [state, state, state, h0]
        out_shape = (x_out, jax.ShapeDtypeStruct((m, D_MODEL), _F32),
                     jax.ShapeDtypeStruct((m, D_MODEL), _F32))
        col_spec = pl.BlockSpec((tm, tn), lambda i, c: (i, c))
        out_specs = (x_spec, col_spec, col_spec)
        scratch = [hn_scratch]
    return pl.pallas_call(
        kern, grid=(m // tm, nj), in_specs=in_specs, out_specs=out_specs, out_shape=out_shape,
        scratch_shapes=scratch,
        compiler_params=pltpu.CompilerParams(
            dimension_semantics=("arbitrary", "arbitrary"), vmem_limit_bytes=VMEM_LIMIT),
        name=f"lru_mixer_l{layer}_m{m}",
    )(*args)


def _ffn_ple_kernel(x_ref, gf_ref, wg_ref, wu_ref, wd_ref, gp_ref, pwg_ref, p_ref, pwp_ref, gl_ref,
                    o_ref, hn_ref, *, n_ffn, tp, final):
    c = pl.program_id(1)
    tm = hn_ref.shape[0]
    rs = _row_subblock(tm)
    _start_mixer(x_ref, gf_ref, o_ref, hn_ref)

    @pl.when(c < n_ffn)
    def _():
        wg, wu, wd = _w(wg_ref), _w(wu_ref), _w(wd_ref)
        for r0 in range(0, tm, rs):
            hn = hn_ref[r0:r0 + rs, :]
            gt = _dot(hn, wg)
            up = _dot(hn, wu)
            h = (gt * _sigmoid(gt)) * up
            o_ref[r0:r0 + rs, :] += _dot(h.astype(_BF16), wd)

    @pl.when(c == n_ffn)
    def _():
        hn_ref[...] = _rmsnorm(o_ref[...], gp_ref[...]).astype(_BF16)

    @pl.when(c >= n_ffn)
    def _():
        cols = pl.ds(pl.multiple_of((c - n_ffn) * tp, tp), tp)
        wg, wp = _w(pwg_ref), _w(pwp_ref)
        for r0 in range(0, tm, rs):
            gate = _sigmoid(_dot(hn_ref[r0:r0 + rs, :], wg))
            proj = _dot(p_ref[r0:r0 + rs, :].astype(_BF16), wp)
            o_ref[r0:r0 + rs, cols] += gate * proj

    if final:
        @pl.when(c == pl.num_programs(1) - 1)
        def _():
            o_ref[...] = _rmsnorm(o_ref[...], gl_ref[...])


def _ffn_ple(x, p, layer, g_ffn, w_gate, w_up, w_down, g_ple, pw_gate, pw_proj, g_final, *, tm):
    m = x.shape[0]
    tf, tp = FFN_CHUNK, PLE_CHUNK
    n_ffn, n_ple = D_FF // tf, D_MODEL // tp

    def ffn_c(c):
        return jnp.minimum(c, n_ffn - 1)

    def ple_c(c):
        return jnp.maximum(c - n_ffn, 0)

    kern = functools.partial(_ffn_ple_kernel, n_ffn=n_ffn, tp=tp, final=(layer == DEPTH - 1))
    return pl.pallas_call(
        kern, grid=(m // tm, n_ffn + n_ple),
        in_specs=[
            pl.BlockSpec((tm, D_MODEL), lambda i, c: (i, 0)),
            pl.BlockSpec((None, 1, D_MODEL), lambda i, c: (layer, 0, 0)),
            pl.BlockSpec((None, D_MODEL, tf), lambda i, c: (layer, 0, ffn_c(c))),
            pl.BlockSpec((None, D_MODEL, tf), lambda i, c: (layer, 0, ffn_c(c))),
            pl.BlockSpec((None, tf, D_MODEL), lambda i, c: (layer, ffn_c(c), 0)),
            pl.BlockSpec((None, 1, D_MODEL), lambda i, c: (layer, 0, 0)),
            pl.BlockSpec((None, D_MODEL, tp), lambda i, c: (layer, 0, ple_c(c))),
            pl.BlockSpec((None, tm, PLE_DIM), lambda i, c: (layer, i, 0)),
            pl.BlockSpec((None, PLE_DIM, tp), lambda i, c: (layer, 0, ple_c(c))),
            pl.BlockSpec((1, D_MODEL), lambda i, c: (0, 0)),
        ],
        out_specs=pl.BlockSpec((tm, D_MODEL), lambda i, c: (i, 0)),
        out_shape=jax.ShapeDtypeStruct((m, D_MODEL), _F32),
        scratch_shapes=[pltpu.VMEM((tm, D_MODEL), _BF16)],
        compiler_params=pltpu.CompilerParams(
            dimension_semantics=("arbitrary", "arbitrary"), vmem_limit_bytes=VMEM_LIMIT),
        name=f"ffn_ple_l{layer}_m{m}",
    )(x, g_ffn, w_gate, w_up, w_down, g_ple, pw_gate, p, pw_proj, g_final)


def _trunk(x, p, weights, *, tm, seq_len=None, conv_state=None, rgc_state=None, rgh_state=None):
    (mix_norm, ffn_norm, ple_norm, final_norm, sc_w_in, sc_w_conv, sc_w_out,
     rg_w_x, rg_w_gate, rg_conv_w, rg_conv_b, rg_w_a, rg_b_a, rg_w_i, rg_b_i, rg_lambda, rg_w_out,
     ffn_w_gate, ffn_w_up, ffn_w_down, ple_w_gate, ple_w_proj) = weights
    m = x.shape[0]
    step = seq_len is None
    if not step:
        tiles_per_seq = seq_len // tm
        last_tile = slice(tiles_per_seq - 1, None, tiles_per_seq)
        last_time = slice(SUBLANES - 1, None, SUBLANES)
    new_conv, new_rgc, new_rgh = [], [], []
    for layer in range(DEPTH):
        j = layer // 2
        if layer % 2 == 0:
            if step:
                st = conv_state[j]
                x, u = _conv_mixer(x, layer, mix_norm, sc_w_in, sc_w_conv, sc_w_out, tm=tm,
                                   state=st.reshape(m, (SC_WIDTH - 1) * D_MODEL))
                new_conv.append(jnp.concatenate([st[:, 1:], u[:, None, :]], axis=1))
            else:
                x, st = _conv_mixer(x, layer, mix_norm, sc_w_in, sc_w_conv, sc_w_out, tm=tm,
                                    seq_len=seq_len)
                new_conv.append(st[last_tile, last_time])
        else:
            lru_w = (rg_w_x, rg_w_gate, rg_conv_w, rg_conv_b, rg_w_a, rg_b_a, rg_w_i, rg_b_i,
                     rg_lambda, rg_w_out)
            if step:
                st = rgc_state[j]
                x, xx, h = _lru_mixer(x, layer, mix_norm, *lru_w, tm=tm,
                                      state=st.reshape(m, (RG_CONV_WIDTH - 1) * D_MODEL),
                                      h0=rgh_state[j])
                new_rgc.append(jnp.concatenate([st[:, 1:], xx[:, None, :]], axis=1))
                new_rgh.append(h)
            else:
                x, st, h = _lru_mixer(x, layer, mix_norm, *lru_w, tm=tm, seq_len=seq_len)
                new_rgc.append(st[last_tile, last_time])
                new_rgh.append(h[last_tile, 0, :])
        x = _ffn_ple(x, p, layer, ffn_norm, ffn_w_gate, ffn_w_up, ffn_w_down,
                     ple_norm, ple_w_gate, ple_w_proj, final_norm, tm=tm)
    return x, jnp.stack(new_conv), jnp.stack(new_rgc), jnp.stack(new_rgh)


def kernel(x_prompt, x_sample, p_prompt, p_sample, state_conv, state_rg_conv, state_rg_h, mix_norm, ffn_norm, ple_norm, final_norm, sc_w_in, sc_w_conv, sc_w_out, rg_w_x, rg_w_gate, rg_conv_w, rg_conv_b, rg_w_a, rg_b_a, rg_w_i, rg_b_i, rg_lambda, rg_w_out, ffn_w_gate, ffn_w_up, ffn_w_down, ple_w_gate, ple_w_proj):
    bsz, seq, _ = x_prompt.shape
    dec = x_sample.shape[0]
    def rows(v):
        return v.reshape(v.shape[0], 1, v.shape[1])

    weights = (rows(mix_norm), rows(ffn_norm), rows(ple_norm), final_norm.reshape(1, D_MODEL),
               sc_w_in, sc_w_conv, sc_w_out, rg_w_x, rg_w_gate, rg_conv_w, rows(rg_conv_b),
               rg_w_a, rows(rg_b_a), rg_w_i, rows(rg_b_i), rows(rg_lambda),
               rg_w_out, ffn_w_gate, ffn_w_up, ffn_w_down, ple_w_gate, ple_w_proj)
    rs = _row_subblock(ROW_TILE)
    y_p, conv_p, rgc_p, rgh_p = _trunk(
        _interleave_time(x_prompt, rs).reshape(bsz * seq, D_MODEL),
        _interleave_time(p_prompt, rs).reshape(DEPTH, bsz * seq, PLE_DIM), weights,
        tm=ROW_TILE, seq_len=seq)
    y_p = _deinterleave_time(y_p.reshape(bsz, seq, D_MODEL), rs)
    y_s, conv_s, rgc_s, rgh_s = _trunk(
        x_sample.reshape(dec, D_MODEL), p_sample.reshape(DEPTH, dec, PLE_DIM), weights,
        tm=dec, conv_state=state_conv, rgc_state=state_rg_conv, rgh_state=state_rg_h)
    return (y_p, y_s.reshape(dec, 1, D_MODEL),
            conv_p, conv_s, rgc_p, rgc_s, rgh_p, rgh_s)
```

```python
import functools

import jax
import jax.numpy as jnp
from jax import lax
from jax.experimental import pallas as pl
from jax.experimental.pallas import tpu as pltpu

D_MODEL = 2048
DEPTH = 4
PLE_DIM = 256
SC_WIDTH = 3
RG_CONV_WIDTH = 4
LRU_BW = 256
LRU_C = 8.0
D_FF = 5632
EPS = 1e-6

SUBLANES = 8
MXU_DIM = 256
ROW_TILE = 1024
ROW_SUBBLOCKS = 2
MIX_CHUNK = LRU_BW
FFN_CHUNK = 512
PLE_CHUNK = 512
FFN_CHUNK_F32 = 256
PLE_CHUNK_F32 = 256
VMEM_LIMIT = 60 * 1024 * 1024

_F32 = jnp.float32
_BF16 = jnp.bfloat16


def _dot(a, b):
    return jnp.dot(a, b, preferred_element_type=_F32)


def _w(ref):
    w = ref[...]
    return w if w.dtype == _BF16 else w.astype(_BF16)


def _rmsnorm(x, g):
    y = x * lax.rsqrt(jnp.mean(x * x, axis=-1, keepdims=True) + EPS)
    return y * g


def _sigmoid(x):
    return jax.nn.sigmoid(x)


def _gelu_tanh(x):
    c = 0.7978845608028654
    return 0.5 * x * (1.0 + jnp.tanh(c * (x + 0.044715 * (x * x * x))))


def _softplus(x):
    return jnp.maximum(x, 0.0) + jnp.log1p(jnp.exp(-jnp.abs(x)))


def _interleave_time(x, rs):
    *lead, t, c = x.shape
    y = x.reshape(*lead, t // rs, SUBLANES, rs // SUBLANES, c)
    return jnp.swapaxes(y, -2, -3).reshape(x.shape)


def _deinterleave_time(x, rs):
    *lead, t, c = x.shape
    y = x.reshape(*lead, t // rs, rs // SUBLANES, SUBLANES, c)
    return jnp.swapaxes(y, -2, -3).reshape(x.shape)


def _time_shifts(buf_ref, vals, prev_tail, width):
    rs = vals.shape[0]
    head = (width - 1) * SUBLANES
    tail = vals[rs - head:rs, :]
    sub = lax.broadcasted_iota(jnp.int32, (SUBLANES, vals.shape[1]), 0)
    for v in range(width - 1):
        grp = slice(v * SUBLANES, (v + 1) * SUBLANES)
        mixed = jnp.where(sub == SUBLANES - 1, prev_tail[grp, :], tail[grp, :])
        buf_ref[grp, :] = pltpu.roll(mixed, 1, 0)
    buf_ref[head:head + rs, :] = vals
    shifted = [buf_ref[head - k * SUBLANES:head - k * SUBLANES + rs, :] for k in range(width - 1, 0, -1)]
    return shifted, tail


def _interleaved_scan(a, b, h_in, hbuf_ref, pbuf_ref):
    rs, tn = a.shape
    groups = rs // SUBLANES
    h = b[0:SUBLANES, :]
    p = a[0:SUBLANES, :]
    hbuf_ref[0:SUBLANES, :] = h
    pbuf_ref[0:SUBLANES, :] = p
    for q in range(1, groups):
        grp = slice(q * SUBLANES, (q + 1) * SUBLANES)
        h = a[grp, :] * h + b[grp, :]
        p = a[grp, :] * p
        hbuf_ref[grp, :] = h
        pbuf_ref[grp, :] = p
    sub = lax.broadcasted_iota(jnp.int32, (SUBLANES, tn), 0)
    carry_s = h_in
    carry = jnp.broadcast_to(h_in, (SUBLANES, tn))
    for s in range(1, SUBLANES):
        carry_s = p[s - 1:s, :] * carry_s + h[s - 1:s, :]
        carry = jnp.where(sub == s, carry_s, carry)
    h_out = p[SUBLANES - 1:SUBLANES, :] * carry_s + h[SUBLANES - 1:SUBLANES, :]
    hs = hbuf_ref[...] + pbuf_ref[...] * jnp.concatenate([carry] * groups, axis=0)
    return hs, h_out


def _row_subblock(tm):
    return tm // ROW_SUBBLOCKS if tm >= ROW_SUBBLOCKS * MXU_DIM else tm


def _start_mixer(x_ref, g_ref, o_ref, hn_ref):
    @pl.when(pl.program_id(1) == 0)
    def _():
        x = x_ref[...]
        hn_ref[...] = _rmsnorm(x, g_ref[...]).astype(_BF16)
        o_ref[...] = x


def _conv_seq_kernel(x_ref, g_ref, wb_ref, wc_ref, wx_ref, cw_ref, wo_ref,
                     o_ref, st_ref, hn_ref, ubuf_ref, carry_ref, *, tm, tiles_per_seq):
    i, j = pl.program_id(0), pl.program_id(1)
    _start_mixer(x_ref, g_ref, o_ref, hn_ref)
    wb, wc, wx, wo, cw = _w(wb_ref), _w(wc_ref), _w(wx_ref), _w(wo_ref), cw_ref[...]
    first = (i % tiles_per_seq) == 0
    tail = jnp.where(first, 0.0, carry_ref[j])
    rs = _row_subblock(tm)
    ups = []
    for r0 in range(0, tm, rs):
        hn = hn_ref[r0:r0 + rs, :]
        ups.append((_dot(hn, wb), _dot(hn, wc), _dot(hn, wx)))
    for k, (b_gate, c_gate, xin) in enumerate(ups):
        u = c_gate * xin
        (u2, u1), tail = _time_shifts(ubuf_ref.at[k], u, tail, SC_WIDTH)
        conv = u2 * cw[0:1, :] + u1 * cw[1:2, :] + u * cw[2:3, :]
        o_ref[k * rs:(k + 1) * rs, :] += _dot((b_gate * conv).astype(_BF16), wo)
    carry_ref[j] = tail
    st_ref[...] = tail


def _conv_step_kernel(x_ref, g_ref, wb_ref, wc_ref, wx_ref, cw_ref, wo_ref, s0_ref, s1_ref,
                      o_ref, u_ref, wb16_ref, wc16_ref, wx16_ref, wo16_ref, hn_ref):
    _start_mixer(x_ref, g_ref, o_ref, hn_ref)
    wb, wc, wx, wo = _w(wb_ref), _w(wc_ref), _w(wx_ref), _w(wo_ref)
    wb16_ref[...], wc16_ref[...], wx16_ref[...], wo16_ref[...] = wb, wc, wx, wo
    hn = hn_ref[...]
    b_gate = _dot(hn, wb)
    u = _dot(hn, wc) * _dot(hn, wx)
    cw = cw_ref[...]
    conv = s0_ref[...] * cw[0:1, :] + s1_ref[...] * cw[1:2, :] + u * cw[2:3, :]
    u_ref[...] = u
    o_ref[...] += _dot((b_gate * conv).astype(_BF16), wo)


def _col_blocks(lead, offset, shape):
    return pl.BlockSpec((None,) + shape, lambda i, c: (lead, 0, offset + c))


def _row_blocks(lead, shape):
    return pl.BlockSpec((None,) + shape, lambda i, c: (lead, c, 0))


def _conv_mixer(x, layer, g, w_conv, w_b, w_c, w_x, w_out, *, tm, seq_len=None, state=None):
    m = x.shape[0]
    tn = MIX_CHUNK
    nj = D_MODEL // tn
    j = layer // 2
    in_specs = [
        pl.BlockSpec((tm, D_MODEL), lambda i, c: (i, 0)),
        pl.BlockSpec((None, 1, D_MODEL), lambda i, c: (layer, 0, 0)),
        _col_blocks(w_b[1], w_b[2], (D_MODEL, tn)),
        _col_blocks(w_c[1], w_c[2], (D_MODEL, tn)),
        _col_blocks(w_x[1], w_x[2], (D_MODEL, tn)),
        pl.BlockSpec((None, SC_WIDTH, tn), lambda i, c: (j, 0, c)),
        _row_blocks(w_out[1], (tn, D_MODEL)),
    ]
    args = [x, g, w_b[0], w_c[0], w_x[0], w_conv, w_out[0]]
    x_out = jax.ShapeDtypeStruct((m, D_MODEL), _F32)
    x_spec = pl.BlockSpec((tm, D_MODEL), lambda i, c: (i, 0))
    hn_scratch = pltpu.VMEM((tm, D_MODEL), _BF16)
    if state is None:
        tiles_per_seq = seq_len // tm
        kern = functools.partial(_conv_seq_kernel, tm=tm, tiles_per_seq=tiles_per_seq)
        head = (SC_WIDTH - 1) * SUBLANES
        rs = _row_subblock(tm)
        out_shape = (x_out, jax.ShapeDtypeStruct((m // tm, head, D_MODEL), _F32))
        out_specs = (x_spec, pl.BlockSpec((None, head, tn), lambda i, c: (i, 0, c)))
        scratch = [hn_scratch, pltpu.VMEM((tm // rs, head + rs, tn), _F32),
                   pltpu.VMEM((nj, head, tn), _F32)]
    else:
        kern = _conv_step_kernel
        in_specs += [pl.BlockSpec((tm, tn), lambda i, c: (i, c)),
                     pl.BlockSpec((tm, tn), lambda i, c: (i, nj + c))]
        args += [state, state]
        w16 = jax.ShapeDtypeStruct((1, D_MODEL, D_MODEL), _BF16)
        out_shape = (x_out, jax.ShapeDtypeStruct((m, D_MODEL), _F32), w16, w16, w16, w16)
        out_specs = (x_spec, pl.BlockSpec((tm, tn), lambda i, c: (i, c)),
                     _col_blocks(0, 0, (D_MODEL, tn)), _col_blocks(0, 0, (D_MODEL, tn)),
                     _col_blocks(0, 0, (D_MODEL, tn)), _row_blocks(0, (tn, D_MODEL)))
        scratch = [hn_scratch]
    return pl.pallas_call(
        kern, grid=(m // tm, nj), in_specs=in_specs, out_specs=out_specs, out_shape=out_shape,
        scratch_shapes=scratch,
        compiler_params=pltpu.CompilerParams(
            dimension_semantics=("arbitrary", "arbitrary"), vmem_limit_bytes=VMEM_LIMIT),
        name=f"conv_mixer_l{layer}_m{m}",
    )(*args)


def _lru_gate_params(wa_ref, ba_ref, wi_ref, bi_ref, lam_ref):
    return _w(wa_ref), ba_ref[...], _w(wi_ref), bi_ref[...], _softplus(-lam_ref[...])


def _lru_gates(u, wa, ba, wi, bi, sp_neg_lam):
    ub = u.astype(_BF16)
    r = _sigmoid(_dot(ub, wa) + ba)
    gate_i = _sigmoid(_dot(ub, wi) + bi)
    log_a = (-LRU_C * r) * sp_neg_lam
    a = jnp.exp(log_a)
    mult = jnp.sqrt(-jnp.tanh(log_a) * (a * a + 1.0))
    return a, mult * gate_i * u


def _lru_seq_kernel(x_ref, g_ref, wg_ref, wx_ref, cw_ref, cb_ref, wa_ref, ba_ref, wi_ref, bi_ref,
                    lam_ref, wo_ref, o_ref, rgc_ref, hl_ref, hn_ref, xbuf_ref, hbuf_ref, pbuf_ref,
                    carry_ref, hcarry_ref,
                    *, tm, tiles_per_seq):
    i, j = pl.program_id(0), pl.program_id(1)
    _start_mixer(x_ref, g_ref, o_ref, hn_ref)
    wg, wx, wo, cw = _w(wg_ref), _w(wx_ref), _w(wo_ref), cw_ref[...]
    gate_params = _lru_gate_params(wa_ref, ba_ref, wi_ref, bi_ref, lam_ref)
    first = (i % tiles_per_seq) == 0
    tail = jnp.where(first, 0.0, carry_ref[j])
    h_state = jnp.where(first, 0.0, hcarry_ref[j])
    rs = _row_subblock(tm)
    nsub = tm // rs

    def up_projection(k):
        hn = hn_ref[k * rs:(k + 1) * rs, :]
        return _dot(hn, wg), _dot(hn, wx)

    ups = {k: up_projection(k) for k in range(min(2, nsub))}
    for k in range(nsub):
        gate_pre, xx = ups.pop(k)
        if k + 2 < nsub:
            ups[k + 2] = up_projection(k + 2)
        gate = _gelu_tanh(gate_pre)
        (x3, x2, x1), tail = _time_shifts(xbuf_ref.at[k], xx, tail, RG_CONV_WIDTH)
        u = (x3 * cw[0:1, :] + x2 * cw[1:2, :] + x1 * cw[2:3, :] + xx * cw[3:4, :]) + cb_ref[...]
        a, b = _lru_gates(u, *gate_params)
        hs, h_state = _interleaved_scan(a, b, h_state, hbuf_ref.at[k], pbuf_ref.at[k])
        o_ref[k * rs:(k + 1) * rs, :] += _dot((gate * hs).astype(_BF16), wo)
    carry_ref[j] = tail
    rgc_ref[...] = tail
    hcarry_ref[j] = h_state
    hl_ref[...] = h_state


def _lru_step_kernel(x_ref, g_ref, wg_ref, wx_ref, cw_ref, cb_ref, wa_ref, ba_ref, wi_ref, bi_ref,
                     lam_ref, wo_ref, s0_ref, s1_ref, s2_ref, h0_ref, o_ref, xx_ref, h_ref,
                     wg16_ref, wx16_ref, wa16_ref, wi16_ref, wo16_ref, hn_ref):
    _start_mixer(x_ref, g_ref, o_ref, hn_ref)
    wg, wx, wo = _w(wg_ref), _w(wx_ref), _w(wo_ref)
    gate_params = _lru_gate_params(wa_ref, ba_ref, wi_ref, bi_ref, lam_ref)
    wg16_ref[...], wx16_ref[...], wo16_ref[...] = wg, wx, wo
    wa16_ref[...], wi16_ref[...] = gate_params[0], gate_params[2]
    hn = hn_ref[...]
    gate = _gelu_tanh(_dot(hn, wg))
    xx = _dot(hn, wx)
    cw = cw_ref[...]
    u = (s0_ref[...] * cw[0:1, :] + s1_ref[...] * cw[1:2, :] + s2_ref[...] * cw[2:3, :]
         + xx * cw[3:4, :]) + cb_ref[...]
    a, b = _lru_gates(u, *gate_params)
    hs = b + a * h0_ref[...]
    xx_ref[...] = xx
    h_ref[...] = hs
    o_ref[...] += _dot((gate * hs).astype(_BF16), wo)


def _lru_mixer(x, layer, g, conv_w, conv_b, b_a, b_i, lam, w_gate, w_x, w_a, w_i, w_out,
               *, tm, seq_len=None, state=None, h0=None):
    m = x.shape[0]
    tn = MIX_CHUNK
    nj = D_MODEL // tn
    j = layer // 2

    def gate_blocks(lead):
        return pl.BlockSpec((None, None, LRU_BW, LRU_BW), lambda i, c: (lead, c, 0, 0))

    def vec_blocks():
        return pl.BlockSpec((None, 1, tn), lambda i, c: (j, 0, c))

    in_specs = [
        pl.BlockSpec((tm, D_MODEL), lambda i, c: (i, 0)),
        pl.BlockSpec((None, 1, D_MODEL), lambda i, c: (layer, 0, 0)),
        _col_blocks(w_gate[1], 0, (D_MODEL, tn)),
        _col_blocks(w_x[1], 0, (D_MODEL, tn)),
        pl.BlockSpec((None, RG_CONV_WIDTH, tn), lambda i, c: (j, 0, c)),
        vec_blocks(),
        gate_blocks(w_a[1]),
        vec_blocks(),
        gate_blocks(w_i[1]),
        vec_blocks(),
        vec_blocks(),
        _row_blocks(w_out[1], (tn, D_MODEL)),
    ]
    args = [x, g, w_gate[0], w_x[0], conv_w, conv_b, w_a[0], b_a, w_i[0], b_i, lam, w_out[0]]
    x_out = jax.ShapeDtypeStruct((m, D_MODEL), _F32)
    x_spec = pl.BlockSpec((tm, D_MODEL), lambda i, c: (i, 0))
    hn_scratch = pltpu.VMEM((tm, D_MODEL), _BF16)
    if state is None:
        tiles_per_seq = seq_len // tm
        nt = m // tm
        kern = functools.partial(_lru_seq_kernel, tm=tm, tiles_per_seq=tiles_per_seq)
        head = (RG_CONV_WIDTH - 1) * SUBLANES
        rs = _row_subblock(tm)
        out_shape = (x_out, jax.ShapeDtypeStruct((nt, head, D_MODEL), _F32),
                     jax.ShapeDtypeStruct((nt, 1, D_MODEL), _F32))
        out_specs = (x_spec,
                     pl.BlockSpec((None, head, tn), lambda i, c: (i, 0, c)),
                     pl.BlockSpec((None, 1, tn), lambda i, c: (i, 0, c)))
        scratch = [hn_scratch, pltpu.VMEM((tm // rs, head + rs, tn), _F32),
                   pltpu.VMEM((tm // rs, rs, tn), _F32), pltpu.VMEM((tm // rs, rs, tn), _F32),
                   pltpu.VMEM((nj, head, tn), _F32), pltpu.VMEM((nj, 1, tn), _F32)]
    else:
        kern = _lru_step_kernel
        in_specs += [pl.BlockSpec((tm, tn), lambda i, c: (i, c)),
                     pl.BlockSpec((tm, tn), lambda i, c: (i, nj + c)),
                     pl.BlockSpec((tm, tn), lambda i, c: (i, 2 * nj + c)),
                     pl.BlockSpec((tm, tn), lambda i, c: (i, c))]
        args += [state, state, state, h0]
        w16 = jax.ShapeDtypeStruct((1, D_MODEL, D_MODEL), _BF16)
        g16 = jax.ShapeDtypeStruct((1, nj, LRU_BW, LRU_BW), _BF16)
        out_shape = (x_out, jax.ShapeDtypeStruct((m, D_MODEL), _F32),
                     jax.ShapeDtypeStruct((m, D_MODEL), _F32), w16, w16, g16, g16, w16)
        col_spec = pl.BlockSpec((tm, tn), lambda i, c: (i, c))
        out_specs = (x_spec, col_spec, col_spec,
                     _col_blocks(0, 0, (D_MODEL, tn)), _col_blocks(0, 0, (D_MODEL, tn)),
                     gate_blocks(0), gate_blocks(0), _row_blocks(0, (tn, D_MODEL)))
        scratch = [hn_scratch]
    return pl.pallas_call(
        kern, grid=(m // tm, nj), in_specs=in_specs, out_specs=out_specs, out_shape=out_shape,
        scratch_shapes=scratch,
        compiler_params=pltpu.CompilerParams(
            dimension_semantics=("arbitrary", "arbitrary"), vmem_limit_bytes=VMEM_LIMIT),
        name=f"lru_mixer_l{layer}_m{m}",
    )(*args)


def _ffn_ple_kernel(x_ref, gf_ref, wg_ref, wu_ref, wd_ref, gp_ref, pwg_ref, p_ref, pwp_ref, gl_ref,
                    *rest, n_ffn, tp, final, emit):
    if emit:
        o_ref, wg16_ref, wu16_ref, wd16_ref, pwg16_ref, pwp16_ref, hn_ref = rest
    else:
        o_ref, hn_ref = rest
    c = pl.program_id(1)
    tm = hn_ref.shape[0]
    rs = _row_subblock(tm)
    _start_mixer(x_ref, gf_ref, o_ref, hn_ref)

    @pl.when(c < n_ffn)
    def _():
        wg, wu, wd = _w(wg_ref), _w(wu_ref), _w(wd_ref)
        if emit:
            wg16_ref[...], wu16_ref[...], wd16_ref[...] = wg, wu, wd
        for r0 in range(0, tm, rs):
            hn = hn_ref[r0:r0 + rs, :]
            gt = _dot(hn, wg)
            up = _dot(hn, wu)
            h = (gt * _sigmoid(gt)) * up
            o_ref[r0:r0 + rs, :] += _dot(h.astype(_BF16), wd)

    @pl.when(c == n_ffn)
    def _():
        hn_ref[...] = _rmsnorm(o_ref[...], gp_ref[...]).astype(_BF16)

    @pl.when(c >= n_ffn)
    def _():
        cols = pl.ds(pl.multiple_of((c - n_ffn) * tp, tp), tp)
        wg, wp = _w(pwg_ref), _w(pwp_ref)
        if emit:
            pwg16_ref[...], pwp16_ref[...] = wg, wp
        for r0 in range(0, tm, rs):
            gate = _sigmoid(_dot(hn_ref[r0:r0 + rs, :], wg))
            proj = _dot(p_ref[r0:r0 + rs, :].astype(_BF16), wp)
            o_ref[r0:r0 + rs, cols] += gate * proj

    if final:
        @pl.when(c == pl.num_programs(1) - 1)
        def _():
            o_ref[...] = _rmsnorm(o_ref[...], gl_ref[...])


def _ffn_ple(x, p, layer, g_ffn, g_ple, g_final, w_gate, w_up, w_down, pw_gate, pw_proj, *, tm):
    m = x.shape[0]
    emit = w_gate[0].dtype != _BF16
    tf, tp = (FFN_CHUNK_F32, PLE_CHUNK_F32) if emit else (FFN_CHUNK, PLE_CHUNK)
    n_ffn, n_ple = D_FF // tf, D_MODEL // tp

    def ffn_c(c):
        return jnp.minimum(c, n_ffn - 1)

    def ple_c(c):
        return jnp.maximum(c - n_ffn, 0)

    def ffn_cols(lead):
        return pl.BlockSpec((None, D_MODEL, tf), lambda i, c: (lead, 0, ffn_c(c)))

    def ffn_rows(lead):
        return pl.BlockSpec((None, tf, D_MODEL), lambda i, c: (lead, ffn_c(c), 0))

    def ple_cols(lead, rows):
        return pl.BlockSpec((None, rows, tp), lambda i, c: (lead, 0, ple_c(c)))

    x_spec = pl.BlockSpec((tm, D_MODEL), lambda i, c: (i, 0))
    out_specs, out_shape = x_spec, jax.ShapeDtypeStruct((m, D_MODEL), _F32)
    if emit:
        out_specs = (x_spec, ffn_cols(0), ffn_cols(0), ffn_rows(0), ple_cols(0, D_MODEL), ple_cols(0, PLE_DIM))
        out_shape = (out_shape,
                     jax.ShapeDtypeStruct((1, D_MODEL, D_FF), _BF16), jax.ShapeDtypeStruct((1, D_MODEL, D_FF), _BF16),
                     jax.ShapeDtypeStruct((1, D_FF, D_MODEL), _BF16),
                     jax.ShapeDtypeStruct((1, D_MODEL, D_MODEL), _BF16),
                     jax.ShapeDtypeStruct((1, PLE_DIM, D_MODEL), _BF16))
    kern = functools.partial(_ffn_ple_kernel, n_ffn=n_ffn, tp=tp, final=(layer == DEPTH - 1), emit=emit)
    return pl.pallas_call(
        kern, grid=(m // tm, n_ffn + n_ple),
        in_specs=[
            x_spec,
            pl.BlockSpec((None, 1, D_MODEL), lambda i, c: (layer, 0, 0)),
            ffn_cols(w_gate[1]), ffn_cols(w_up[1]), ffn_rows(w_down[1]),
            pl.BlockSpec((None, 1, D_MODEL), lambda i, c: (layer, 0, 0)),
            ple_cols(pw_gate[1], D_MODEL),
            pl.BlockSpec((None, tm, PLE_DIM), lambda i, c: (layer, i, 0)),
            ple_cols(pw_proj[1], PLE_DIM),
            pl.BlockSpec((1, D_MODEL), lambda i, c: (0, 0)),
        ],
        out_specs=out_specs, out_shape=out_shape,
        scratch_shapes=[pltpu.VMEM((tm, D_MODEL), _BF16)],
        compiler_params=pltpu.CompilerParams(
            dimension_semantics=("arbitrary", "arbitrary"), vmem_limit_bytes=VMEM_LIMIT),
        name=f"ffn_ple_l{layer}_m{m}",
    )(x, g_ffn, w_gate[0], w_up[0], w_down[0], g_ple, pw_gate[0], p, pw_proj[0], g_final)


def _trunk(x, p, weights, *, tm, seq_len=None, conv_state=None, rgc_state=None, rgh_state=None,
           mxu_weights=None):
    (mix_norm, ffn_norm, ple_norm, final_norm, sc_w_in, sc_w_conv, sc_w_out,
     rg_w_x, rg_w_gate, rg_conv_w, rg_conv_b, rg_w_a, rg_b_a, rg_w_i, rg_b_i, rg_lambda, rg_w_out,
     ffn_w_gate, ffn_w_up, ffn_w_down, ple_w_gate, ple_w_proj) = weights
    m = x.shape[0]
    step = seq_len is None
    if not step:
        tiles_per_seq = seq_len // tm
        last_tile = slice(tiles_per_seq - 1, None, tiles_per_seq)
        last_time = slice(SUBLANES - 1, None, SUBLANES)
    nj = D_MODEL // MIX_CHUNK
    new_conv, new_rgc, new_rgh, emitted = [], [], [], []
    for layer in range(DEPTH):
        j = layer // 2
        if layer % 2 == 0:
            if step:
                st = conv_state[j]
                x, u, *mix16 = _conv_mixer(
                    x, layer, mix_norm, sc_w_conv, (sc_w_in, j, 0), (sc_w_in, j, nj), (sc_w_in, j, 2 * nj),
                    (sc_w_out, j), tm=tm, state=st.reshape(m, (SC_WIDTH - 1) * D_MODEL))
                new_conv.append(jnp.concatenate([st[:, 1:], u[:, None, :]], axis=1))
            else:
                wb, wc, wx, wo = mxu_weights[layer][0]
                x, st = _conv_mixer(x, layer, mix_norm, sc_w_conv, (wb, 0, 0), (wc, 0, 0), (wx, 0, 0), (wo, 0),
                                    tm=tm, seq_len=seq_len)
                new_conv.append(st[last_tile, last_time])
        else:
            lru_v = (rg_conv_w, rg_conv_b, rg_b_a, rg_b_i, rg_lambda)
            if step:
                st = rgc_state[j]
                x, xx, h, *mix16 = _lru_mixer(
                    x, layer, mix_norm, *lru_v, (rg_w_gate, j), (rg_w_x, j), (rg_w_a, j), (rg_w_i, j),
                    (rg_w_out, j), tm=tm, state=st.reshape(m, (RG_CONV_WIDTH - 1) * D_MODEL), h0=rgh_state[j])
                new_rgc.append(jnp.concatenate([st[:, 1:], xx[:, None, :]], axis=1))
                new_rgh.append(h)
            else:
                x, st, h = _lru_mixer(x, layer, mix_norm, *lru_v, *[(w, 0) for w in mxu_weights[layer][0]],
                                      tm=tm, seq_len=seq_len)
                new_rgc.append(st[last_tile, last_time])
                new_rgh.append(h[last_tile, 0, :])
        if step:
            x, *ffn16 = _ffn_ple(x, p, layer, ffn_norm, ple_norm, final_norm, (ffn_w_gate, layer),
                                 (ffn_w_up, layer), (ffn_w_down, layer), (ple_w_gate, layer), (ple_w_proj, layer),
                                 tm=tm)
            emitted.append((mix16, ffn16))
        else:
            x = _ffn_ple(x, p, layer, ffn_norm, ple_norm, final_norm, *[(w, 0) for w in mxu_weights[layer][1]],
                         tm=tm)
    return x, jnp.stack(new_conv), jnp.stack(new_rgc), jnp.stack(new_rgh), emitted


def kernel(x_prompt, x_sample, p_prompt, p_sample, state_conv, state_rg_conv, state_rg_h, mix_norm, ffn_norm, ple_norm, final_norm, sc_w_in, sc_w_conv, sc_w_out, rg_w_x, rg_w_gate, rg_conv_w, rg_conv_b, rg_w_a, rg_b_a, rg_w_i, rg_b_i, rg_lambda, rg_w_out, ffn_w_gate, ffn_w_up, ffn_w_down, ple_w_gate, ple_w_proj):
    bsz, seq, _ = x_prompt.shape
    dec = x_sample.shape[0]
    def rows(v):
        return v.reshape(v.shape[0], 1, v.shape[1])

    weights = (rows(mix_norm), rows(ffn_norm), rows(ple_norm), final_norm.reshape(1, D_MODEL),
               sc_w_in, sc_w_conv, sc_w_out, rg_w_x, rg_w_gate, rg_conv_w, rows(rg_conv_b),
               rg_w_a, rows(rg_b_a), rg_w_i, rows(rg_b_i), rows(rg_lambda),
               rg_w_out, ffn_w_gate, ffn_w_up, ffn_w_down, ple_w_gate, ple_w_proj)
    y_s, conv_s, rgc_s, rgh_s, mxu_weights = _trunk(
        x_sample.reshape(dec, D_MODEL), p_sample.reshape(DEPTH, dec, PLE_DIM), weights,
        tm=dec, conv_state=state_conv, rgc_state=state_rg_conv, rgh_state=state_rg_h)
    rs = _row_subblock(ROW_TILE)
    y_p, conv_p, rgc_p, rgh_p, _ = _trunk(
        _interleave_time(x_prompt, rs).reshape(bsz * seq, D_MODEL),
        _interleave_time(p_prompt, rs).reshape(DEPTH, bsz * seq, PLE_DIM), weights,
        tm=ROW_TILE, seq_len=seq, mxu_weights=mxu_weights)
    y_p = _deinterleave_time(y_p.reshape(bsz, seq, D_MODEL), rs)
    return (y_p, y_s.reshape(dec, 1, D_MODEL),
            conv_p, conv_s, rgc_p, rgc_s, rgh_p, rgh_s)
```

```python
import functools

import jax
import jax.numpy as jnp
from jax import lax
from jax.experimental import pallas as pl
from jax.experimental.pallas import tpu as pltpu

D_MODEL = 2048
DEPTH = 4
PLE_DIM = 256
SC_WIDTH = 3
RG_CONV_WIDTH = 4
LRU_BW = 256
LRU_C = 8.0
D_FF = 5632
EPS = 1e-6

SUBLANES = 8
MXU_DIM = 256
ROW_TILE = 1024
ROW_SUBBLOCKS = 2
MIX_CHUNK = LRU_BW
FFN_CHUNK = 512
PLE_CHUNK = 512
FFN_CHUNK_F32 = 256
PLE_CHUNK_F32 = 256
VMEM_LIMIT = 60 * 1024 * 1024

_F32 = jnp.float32
_BF16 = jnp.bfloat16


def _dot(a, b):
    return jnp.dot(a, b, preferred_element_type=_F32)


def _w(ref):
    w = ref[...]
    return w if w.dtype == _BF16 else w.astype(_BF16)


def _mxu_weights(w_refs, w16_refs=None):
    if w16_refs is None:
        return [_w(r) for r in w_refs]
    for src, dst in zip(w_refs, w16_refs):
        dst[...] = _w(src)
    return [dst[...] for dst in w16_refs]


def _rmsnorm(x, g):
    y = x * lax.rsqrt(jnp.mean(x * x, axis=-1, keepdims=True) + EPS)
    return y * g


def _sigmoid(x):
    return jax.nn.sigmoid(x)


def _gelu_tanh(x):
    c = 0.7978845608028654
    return 0.5 * x * (1.0 + jnp.tanh(c * (x + 0.044715 * (x * x * x))))


def _softplus(x):
    return jnp.maximum(x, 0.0) + jnp.log1p(jnp.exp(-jnp.abs(x)))


def _interleave_time(x, rs):
    *lead, t, c = x.shape
    y = x.reshape(*lead, t // rs, SUBLANES, rs // SUBLANES, c)
    return jnp.swapaxes(y, -2, -3).reshape(x.shape)


def _deinterleave_time(x, rs):
    *lead, t, c = x.shape
    y = x.reshape(*lead, t // rs, rs // SUBLANES, SUBLANES, c)
    return jnp.swapaxes(y, -2, -3).reshape(x.shape)


def _time_shifts(buf_ref, vals, prev_tail, width):
    rs = vals.shape[0]
    head = (width - 1) * SUBLANES
    tail = vals[rs - head:rs, :]
    sub = lax.broadcasted_iota(jnp.int32, (SUBLANES, vals.shape[1]), 0)
    for v in range(width - 1):
        grp = slice(v * SUBLANES, (v + 1) * SUBLANES)
        mixed = jnp.where(sub == SUBLANES - 1, prev_tail[grp, :], tail[grp, :])
        buf_ref[grp, :] = pltpu.roll(mixed, 1, 0)
    buf_ref[head:head + rs, :] = vals
    shifted = [buf_ref[head - k * SUBLANES:head - k * SUBLANES + rs, :] for k in range(width - 1, 0, -1)]
    return shifted, tail


def _interleaved_scan(a, b, h_in, hbuf_ref, pbuf_ref):
    rs, tn = a.shape
    groups = rs // SUBLANES
    h = b[0:SUBLANES, :]
    p = a[0:SUBLANES, :]
    hbuf_ref[0:SUBLANES, :] = h
    pbuf_ref[0:SUBLANES, :] = p
    for q in range(1, groups):
        grp = slice(q * SUBLANES, (q + 1) * SUBLANES)
        h = a[grp, :] * h + b[grp, :]
        p = a[grp, :] * p
        hbuf_ref[grp, :] = h
        pbuf_ref[grp, :] = p
    sub = lax.broadcasted_iota(jnp.int32, (SUBLANES, tn), 0)
    carry_s = h_in
    carry = jnp.broadcast_to(h_in, (SUBLANES, tn))
    for s in range(1, SUBLANES):
        carry_s = p[s - 1:s, :] * carry_s + h[s - 1:s, :]
        carry = jnp.where(sub == s, carry_s, carry)
    h_out = p[SUBLANES - 1:SUBLANES, :] * carry_s + h[SUBLANES - 1:SUBLANES, :]
    hs = hbuf_ref[...] + pbuf_ref[...] * jnp.concatenate([carry] * groups, axis=0)
    return hs, h_out


def _row_subblock(tm):
    return tm // ROW_SUBBLOCKS if tm >= ROW_SUBBLOCKS * MXU_DIM else tm


def _start_mixer(x_ref, g_ref, o_ref, hn_ref):
    @pl.when(pl.program_id(1) == 0)
    def _():
        x = x_ref[...]
        hn_ref[...] = _rmsnorm(x, g_ref[...]).astype(_BF16)
        o_ref[...] = x


def _entering(first, cin_ref, carry_ref, j):
    held = jnp.where(pl.program_id(0) == 0, cin_ref[...], carry_ref[j])
    return jnp.where(first, 0.0, held)


def _col_blocks(lead, offset, shape):
    return pl.BlockSpec((None,) + shape, lambda i, c: (lead, 0, offset + c))


def _row_blocks(lead, shape):
    return pl.BlockSpec((None,) + shape, lambda i, c: (lead, c, 0))


def _row_tile_spec(tm, m):
    mode = dict(pipeline_mode=pl.Buffered(1)) if m == tm else {}
    return pl.BlockSpec((tm, D_MODEL), lambda i, c: (i, 0), **mode)


def _compiler_params():
    return pltpu.CompilerParams(dimension_semantics=("arbitrary", "arbitrary"), vmem_limit_bytes=VMEM_LIMIT)


def _conv_kernel(*refs, tm, tiles_per_seq, tile_offset, sample):
    if sample:
        (x_ref, g_ref, wb_ref, wc_ref, wx_ref, cw_ref, wo_ref, cin_ref, xs_ref, s0_ref, s1_ref,
         o_ref, st_ref, os_ref, us_ref, wb16_ref, wc16_ref, wx16_ref, wo16_ref,
         hn_ref, ubuf_ref, carry_ref, hns_ref) = refs
    else:
        (x_ref, g_ref, wb_ref, wc_ref, wx_ref, cw_ref, wo_ref, cin_ref,
         o_ref, st_ref, hn_ref, ubuf_ref, carry_ref) = refs
    i, j = pl.program_id(0), pl.program_id(1)
    _start_mixer(x_ref, g_ref, o_ref, hn_ref)
    wb, wc, wx, wo = _mxu_weights((wb_ref, wc_ref, wx_ref, wo_ref),
                                  (wb16_ref, wc16_ref, wx16_ref, wo16_ref) if sample else None)
    cw = cw_ref[...]
    first = ((i + tile_offset) % tiles_per_seq) == 0
    tail = _entering(first, cin_ref, carry_ref, j)
    rs = _row_subblock(tm)
    ups = []
    for r0 in range(0, tm, rs):
        hn = hn_ref[r0:r0 + rs, :]
        ups.append((_dot(hn, wb), _dot(hn, wc), _dot(hn, wx)))
    for k, (b_gate, c_gate, xin) in enumerate(ups):
        u = c_gate * xin
        (u2, u1), tail = _time_shifts(ubuf_ref.at[k], u, tail, SC_WIDTH)
        conv = u2 * cw[0:1, :] + u1 * cw[1:2, :] + u * cw[2:3, :]
        o_ref[k * rs:(k + 1) * rs, :] += _dot((b_gate * conv).astype(_BF16), wo)
    carry_ref[j] = tail
    st_ref[...] = tail
    if sample:
        _start_mixer(xs_ref, g_ref, os_ref, hns_ref)
        hn = hns_ref[...]
        b_gate = _dot(hn, wb)
        u = _dot(hn, wc) * _dot(hn, wx)
        conv = s0_ref[...] * cw[0:1, :] + s1_ref[...] * cw[1:2, :] + u * cw[2:3, :]
        us_ref[...] = u
        os_ref[...] += _dot((b_gate * conv).astype(_BF16), wo)


def _conv_mixer(x, layer, g, w_conv, w_b, w_c, w_x, w_out, cin, *, tm, seq_len, tile_offset,
                xs=None, state=None):
    m = x.shape[0]
    tn = MIX_CHUNK
    nj = D_MODEL // tn
    j = layer // 2
    head = (SC_WIDTH - 1) * SUBLANES
    rs = _row_subblock(tm)
    sample = xs is not None
    x_spec = _row_tile_spec(tm, m)
    in_specs = [
        x_spec,
        pl.BlockSpec((None, 1, D_MODEL), lambda i, c: (layer, 0, 0)),
        _col_blocks(w_b[1], w_b[2], (D_MODEL, tn)),
        _col_blocks(w_c[1], w_c[2], (D_MODEL, tn)),
        _col_blocks(w_x[1], w_x[2], (D_MODEL, tn)),
        pl.BlockSpec((None, SC_WIDTH, tn), lambda i, c: (j, 0, c)),
        _row_blocks(w_out[1], (tn, D_MODEL)),
        _col_blocks(0, 0, (head, tn)),
    ]
    args = [x, g, w_b[0], w_c[0], w_x[0], w_conv, w_out[0], cin]
    out_shape = [jax.ShapeDtypeStruct((m, D_MODEL), _F32), jax.ShapeDtypeStruct((m // tm, head, D_MODEL), _F32)]
    out_specs = [x_spec, pl.BlockSpec((None, head, tn), lambda i, c: (i, 0, c))]
    scratch = [pltpu.VMEM((tm, D_MODEL), _BF16), pltpu.VMEM((tm // rs, head + rs, tn), _F32),
               pltpu.VMEM((nj, head, tn), _F32)]
    if sample:
        ms = xs.shape[0]
        xs_spec = pl.BlockSpec((ms, D_MODEL), lambda i, c: (0, 0))
        in_specs += [xs_spec, pl.BlockSpec((ms, tn), lambda i, c: (0, c)),
                     pl.BlockSpec((ms, tn), lambda i, c: (0, nj + c))]
        args += [xs, state, state]
        w16 = jax.ShapeDtypeStruct((1, D_MODEL, D_MODEL), _BF16)
        out_shape += [jax.ShapeDtypeStruct((ms, D_MODEL), _F32), jax.ShapeDtypeStruct((ms, D_MODEL), _F32),
                      w16, w16, w16, w16]
        out_specs += [xs_spec, pl.BlockSpec((ms, tn), lambda i, c: (0, c)),
                      _col_blocks(0, 0, (D_MODEL, tn)), _col_blocks(0, 0, (D_MODEL, tn)),
                      _col_blocks(0, 0, (D_MODEL, tn)), _row_blocks(0, (tn, D_MODEL))]
        scratch += [pltpu.VMEM((ms, D_MODEL), _BF16)]
    kern = functools.partial(_conv_kernel, tm=tm, tiles_per_seq=seq_len // tm, tile_offset=tile_offset,
                             sample=sample)
    return pl.pallas_call(
        kern, grid=(m // tm, nj), in_specs=in_specs, out_specs=out_specs, out_shape=out_shape,
        scratch_shapes=scratch, compiler_params=_compiler_params(),
        name=f"conv_mixer_l{layer}_m{m}",
    )(*args)


def _lru_gates(u, wa, ba, wi, bi, sp_neg_lam):
    ub = u.astype(_BF16)
    r = _sigmoid(_dot(ub, wa) + ba)
    gate_i = _sigmoid(_dot(ub, wi) + bi)
    log_a = (-LRU_C * r) * sp_neg_lam
    a = jnp.exp(log_a)
    mult = jnp.sqrt(-jnp.tanh(log_a) * (a * a + 1.0))
    return a, mult * gate_i * u


def _lru_kernel(*refs, tm, tiles_per_seq, tile_offset, sample):
    if sample:
        (x_ref, g_ref, wg_ref, wx_ref, cw_ref, cb_ref, wa_ref, ba_ref, wi_ref, bi_ref, lam_ref, wo_ref,
         cin_ref, hin_ref, xs_ref, s0_ref, s1_ref, s2_ref, h0_ref,
         o_ref, rgc_ref, hl_ref, os_ref, xxs_ref, hs_ref,
         wg16_ref, wx16_ref, wa16_ref, wi16_ref, wo16_ref,
         hn_ref, xbuf_ref, hbuf_ref, pbuf_ref, carry_ref, hcarry_ref, hns_ref) = refs
    else:
        (x_ref, g_ref, wg_ref, wx_ref, cw_ref, cb_ref, wa_ref, ba_ref, wi_ref, bi_ref, lam_ref, wo_ref,
         cin_ref, hin_ref, o_ref, rgc_ref, hl_ref,
         hn_ref, xbuf_ref, hbuf_ref, pbuf_ref, carry_ref, hcarry_ref) = refs
    i, j = pl.program_id(0), pl.program_id(1)
    _start_mixer(x_ref, g_ref, o_ref, hn_ref)
    wg, wx, wo, wa, wi = _mxu_weights((wg_ref, wx_ref, wo_ref, wa_ref, wi_ref),
                                      (wg16_ref, wx16_ref, wo16_ref, wa16_ref, wi16_ref) if sample else None)
    cw, cb = cw_ref[...], cb_ref[...]
    gate_params = (wa, ba_ref[...], wi, bi_ref[...], _softplus(-lam_ref[...]))
    first = ((i + tile_offset) % tiles_per_seq) == 0
    tail = _entering(first, cin_ref, carry_ref, j)
    h_state = _entering(first, hin_ref, hcarry_ref, j)
    rs = _row_subblock(tm)
    ups = []
    for r0 in range(0, tm, rs):
        hn = hn_ref[r0:r0 + rs, :]
        ups.append((_dot(hn, wg), _dot(hn, wx)))
    for k, (gate_pre, xx) in enumerate(ups):
        gate = _gelu_tanh(gate_pre)
        (x3, x2, x1), tail = _time_shifts(xbuf_ref.at[k], xx, tail, RG_CONV_WIDTH)
        u = (x3 * cw[0:1, :] + x2 * cw[1:2, :] + x1 * cw[2:3, :] + xx * cw[3:4, :]) + cb
        a, b = _lru_gates(u, *gate_params)
        hs, h_state = _interleaved_scan(a, b, h_state, hbuf_ref.at[k], pbuf_ref.at[k])
        o_ref[k * rs:(k + 1) * rs, :] += _dot((gate * hs).astype(_BF16), wo)
    carry_ref[j] = tail
    rgc_ref[...] = tail
    hcarry_ref[j] = h_state
    hl_ref[...] = h_state
    if sample:
        _start_mixer(xs_ref, g_ref, os_ref, hns_ref)
        hn = hns_ref[...]
        gate = _gelu_tanh(_dot(hn, wg))
        xx = _dot(hn, wx)
        u = (s0_ref[...] * cw[0:1, :] + s1_ref[...] * cw[1:2, :] + s2_ref[...] * cw[2:3, :]
             + xx * cw[3:4, :]) + cb
        a, b = _lru_gates(u, *gate_params)
        hs = b + a * h0_ref[...]
        xxs_ref[...] = xx
        hs_ref[...] = hs
        os_ref[...] += _dot((gate * hs).astype(_BF16), wo)


def _lru_mixer(x, layer, g, conv_w, conv_b, b_a, b_i, lam, w_gate, w_x, w_a, w_i, w_out, cin, hin,
               *, tm, seq_len, tile_offset, xs=None, state=None, h0=None):
    m = x.shape[0]
    tn = MIX_CHUNK
    nj = D_MODEL // tn
    j = layer // 2
    head = (RG_CONV_WIDTH - 1) * SUBLANES
    rs = _row_subblock(tm)
    nt = m // tm
    sample = xs is not None

    def gate_blocks(lead):
        return pl.BlockSpec((None, None, LRU_BW, LRU_BW), lambda i, c: (lead, c, 0, 0))

    def vec_blocks():
        return pl.BlockSpec((None, 1, tn), lambda i, c: (j, 0, c))

    x_spec = _row_tile_spec(tm, m)
    in_specs = [
        x_spec,
        pl.BlockSpec((None, 1, D_MODEL), lambda i, c: (layer, 0, 0)),
        _col_blocks(w_gate[1], 0, (D_MODEL, tn)),
        _col_blocks(w_x[1], 0, (D_MODEL, tn)),
        pl.BlockSpec((None, RG_CONV_WIDTH, tn), lambda i, c: (j, 0, c)),
        vec_blocks(),
        gate_blocks(w_a[1]),
        vec_blocks(),
        gate_blocks(w_i[1]),
        vec_blocks(),
        vec_blocks(),
        _row_blocks(w_out[1], (tn, D_MODEL)),
        _col_blocks(0, 0, (head, tn)),
        _col_blocks(0, 0, (1, tn)),
    ]
    args = [x, g, w_gate[0], w_x[0], conv_w, conv_b, w_a[0], b_a, w_i[0], b_i, lam, w_out[0], cin, hin]
    out_shape = [jax.ShapeDtypeStruct((m, D_MODEL), _F32), jax.ShapeDtypeStruct((nt, head, D_MODEL), _F32),
                 jax.ShapeDtypeStruct((nt, 1, D_MODEL), _F32)]
    out_specs = [x_spec, pl.BlockSpec((None, head, tn), lambda i, c: (i, 0, c)),
                 pl.BlockSpec((None, 1, tn), lambda i, c: (i, 0, c))]
    scratch = [pltpu.VMEM((tm, D_MODEL), _BF16), pltpu.VMEM((tm // rs, head + rs, tn), _F32),
               pltpu.VMEM((tm // rs, rs, tn), _F32), pltpu.VMEM((tm // rs, rs, tn), _F32),
               pltpu.VMEM((nj, head, tn), _F32), pltpu.VMEM((nj, 1, tn), _F32)]
    if sample:
        ms = xs.shape[0]
        xs_spec = pl.BlockSpec((ms, D_MODEL), lambda i, c: (0, 0))
        col_spec = pl.BlockSpec((ms, tn), lambda i, c: (0, c))
        in_specs += [xs_spec, col_spec, pl.BlockSpec((ms, tn), lambda i, c: (0, nj + c)),
                     pl.BlockSpec((ms, tn), lambda i, c: (0, 2 * nj + c)), col_spec]
        args += [xs, state, state, state, h0]
        w16 = jax.ShapeDtypeStruct((1, D_MODEL, D_MODEL), _BF16)
        g16 = jax.ShapeDtypeStruct((1, nj, LRU_BW, LRU_BW), _BF16)
        row = jax.ShapeDtypeStruct((ms, D_MODEL), _F32)
        out_shape += [row, row, row, w16, w16, g16, g16, w16]
        out_specs += [xs_spec, col_spec, col_spec,
                      _col_blocks(0, 0, (D_MODEL, tn)), _col_blocks(0, 0, (D_MODEL, tn)),
                      gate_blocks(0), gate_blocks(0), _row_blocks(0, (tn, D_MODEL))]
        scratch += [pltpu.VMEM((ms, D_MODEL), _BF16)]
    kern = functools.partial(_lru_kernel, tm=tm, tiles_per_seq=seq_len // tm, tile_offset=tile_offset,
                             sample=sample)
    return pl.pallas_call(
        kern, grid=(nt, nj), in_specs=in_specs, out_specs=out_specs, out_shape=out_shape,
        scratch_shapes=scratch, compiler_params=_compiler_params(),
        name=f"lru_mixer_l{layer}_m{m}",
    )(*args)


def _ffn_ple_kernel(*refs, n_ffn, tp, final, sample):
    if sample:
        (x_ref, gf_ref, wg_ref, wu_ref, wd_ref, gp_ref, pwg_ref, p_ref, pwp_ref, gl_ref, xs_ref, ps_ref,
         o_ref, os_ref, wg16_ref, wu16_ref, wd16_ref, pwg16_ref, pwp16_ref, hn_ref, hns_ref) = refs
        groups = ((x_ref, p_ref, o_ref, hn_ref), (xs_ref, ps_ref, os_ref, hns_ref))
    else:
        (x_ref, gf_ref, wg_ref, wu_ref, wd_ref, gp_ref, pwg_ref, p_ref, pwp_ref, gl_ref,
         o_ref, hn_ref) = refs
        groups = ((x_ref, p_ref, o_ref, hn_ref),)
    c = pl.program_id(1)
    for xg_ref, _, og_ref, hg_ref in groups:
        _start_mixer(xg_ref, gf_ref, og_ref, hg_ref)

    @pl.when(c < n_ffn)
    def _():
        wg, wu, wd = _mxu_weights((wg_ref, wu_ref, wd_ref),
                                  (wg16_ref, wu16_ref, wd16_ref) if sample else None)
        for _, _, og_ref, hg_ref in groups:
            tm = hg_ref.shape[0]
            rs = _row_subblock(tm)
            for r0 in range(0, tm, rs):
                hn = hg_ref[r0:r0 + rs, :]
                gt = _dot(hn, wg)
                up = _dot(hn, wu)
                h = (gt * _sigmoid(gt)) * up
                og_ref[r0:r0 + rs, :] += _dot(h.astype(_BF16), wd)

    @pl.when(c == n_ffn)
    def _():
        for _, _, og_ref, hg_ref in groups:
            hg_ref[...] = _rmsnorm(og_ref[...], gp_ref[...]).astype(_BF16)

    @pl.when(c >= n_ffn)
    def _():
        cols = pl.ds(pl.multiple_of((c - n_ffn) * tp, tp), tp)
        wg, wp = _mxu_weights((pwg_ref, pwp_ref), (pwg16_ref, pwp16_ref) if sample else None)
        for _, pg_ref, og_ref, hg_ref in groups:
            tm = hg_ref.shape[0]
            rs = _row_subblock(tm)
            for r0 in range(0, tm, rs):
                gate = _sigmoid(_dot(hg_ref[r0:r0 + rs, :], wg))
                proj = _dot(pg_ref[r0:r0 + rs, :].astype(_BF16), wp)
                og_ref[r0:r0 + rs, cols] += gate * proj

    if final:
        @pl.when(c == pl.num_programs(1) - 1)
        def _():
            for _, _, og_ref, _ in groups:
                og_ref[...] = _rmsnorm(og_ref[...], gl_ref[...])


def _ffn_ple(x, p, layer, g_ffn, g_ple, g_final, w_gate, w_up, w_down, pw_gate, pw_proj,
             *, tm, tile_offset, xs=None, ps=None):
    m = x.shape[0]
    sample = xs is not None
    tf, tp = (FFN_CHUNK_F32, PLE_CHUNK_F32) if sample else (FFN_CHUNK, PLE_CHUNK)
    n_ffn, n_ple = D_FF // tf, D_MODEL // tp

    def ffn_c(c):
        return jnp.minimum(c, n_ffn - 1)

    def ple_c(c):
        return jnp.maximum(c - n_ffn, 0)

    def ffn_cols(lead):
        return pl.BlockSpec((None, D_MODEL, tf), lambda i, c: (lead, 0, ffn_c(c)))

    def ffn_rows(lead):
        return pl.BlockSpec((None, tf, D_MODEL), lambda i, c: (lead, ffn_c(c), 0))

    def ple_cols(lead, rows):
        return pl.BlockSpec((None, rows, tp), lambda i, c: (lead, 0, ple_c(c)))

    x_spec = _row_tile_spec(tm, m)
    in_specs = [
        x_spec,
        pl.BlockSpec((None, 1, D_MODEL), lambda i, c: (layer, 0, 0)),
        ffn_cols(w_gate[1]), ffn_cols(w_up[1]), ffn_rows(w_down[1]),
        pl.BlockSpec((None, 1, D_MODEL), lambda i, c: (layer, 0, 0)),
        ple_cols(pw_gate[1], D_MODEL),
        pl.BlockSpec((None, tm, PLE_DIM), lambda i, c: (layer, i + tile_offset, 0)),
        ple_cols(pw_proj[1], PLE_DIM),
        pl.BlockSpec((1, D_MODEL), lambda i, c: (0, 0)),
    ]
    args = [x, g_ffn, w_gate[0], w_up[0], w_down[0], g_ple, pw_gate[0], p, pw_proj[0], g_final]
    out_specs = [x_spec]
    out_shape = [jax.ShapeDtypeStruct((m, D_MODEL), _F32)]
    scratch = [pltpu.VMEM((tm, D_MODEL), _BF16)]
    if sample:
        ms = xs.shape[0]
        xs_spec = pl.BlockSpec((ms, D_MODEL), lambda i, c: (0, 0))
        in_specs += [xs_spec, pl.BlockSpec((None, ms, PLE_DIM), lambda i, c: (layer, 0, 0))]
        args += [xs, ps]
        out_specs += [xs_spec, ffn_cols(0), ffn_cols(0), ffn_rows(0), ple_cols(0, D_MODEL), ple_cols(0, PLE_DIM)]
        out_shape += [jax.ShapeDtypeStruct((ms, D_MODEL), _F32),
                      jax.ShapeDtypeStruct((1, D_MODEL, D_FF), _BF16), jax.ShapeDtypeStruct((1, D_MODEL, D_FF), _BF16),
                      jax.ShapeDtypeStruct((1, D_FF, D_MODEL), _BF16),
                      jax.ShapeDtypeStruct((1, D_MODEL, D_MODEL), _BF16),
                      jax.ShapeDtypeStruct((1, PLE_DIM, D_MODEL), _BF16)]
        scratch += [pltpu.VMEM((ms, D_MODEL), _BF16)]
    kern = functools.partial(_ffn_ple_kernel, n_ffn=n_ffn, tp=tp, final=(layer == DEPTH - 1), sample=sample)
    return pl.pallas_call(
        kern, grid=(m // tm, n_ffn + n_ple), in_specs=in_specs, out_specs=out_specs, out_shape=out_shape,
        scratch_shapes=scratch, compiler_params=_compiler_params(),
        name=f"ffn_ple_l{layer}_m{m}",
    )(*args)


def _trunk(x, p, xs, ps, conv_state, rgc_state, rgh_state, weights, *, tm, seq_len):
    (mix_norm, ffn_norm, ple_norm, final_norm, sc_w_in, sc_w_conv, sc_w_out,
     rg_w_x, rg_w_gate, rg_conv_w, rg_conv_b, rg_w_a, rg_b_a, rg_w_i, rg_b_i, rg_lambda, rg_w_out,
     ffn_w_gate, ffn_w_up, ffn_w_down, ple_w_gate, ple_w_proj) = weights
    ms = xs.shape[0]
    tiles_per_seq = seq_len // tm
    assert tiles_per_seq >= 2, "the head call's row tile must not end a sequence"
    last_tile = slice(tiles_per_seq - 2, None, tiles_per_seq)
    last_time = slice(SUBLANES - 1, None, SUBLANES)
    nj = D_MODEL // MIX_CHUNK
    x0, xr = x[:tm], x[tm:]
    zeros = functools.partial(jnp.zeros, dtype=_F32)
    geom = dict(tm=tm, seq_len=seq_len)
    conv_p, conv_s, rgc_p, rgc_s, rgh_p, rgh_s = [], [], [], [], [], []
    for layer in range(DEPTH):
        j = layer // 2
        if layer % 2 == 0:
            st = conv_state[j]
            x0, st0, xs, u, *w16 = _conv_mixer(
                x0, layer, mix_norm, sc_w_conv, (sc_w_in, j, 0), (sc_w_in, j, nj), (sc_w_in, j, 2 * nj),
                (sc_w_out, j), zeros((1, (SC_WIDTH - 1) * SUBLANES, D_MODEL)), tile_offset=0, xs=xs,
                state=st.reshape(ms, (SC_WIDTH - 1) * D_MODEL), **geom)
            wb, wc, wx, wo = w16
            xr, st_r = _conv_mixer(xr, layer, mix_norm, sc_w_conv, (wb, 0, 0), (wc, 0, 0), (wx, 0, 0), (wo, 0),
                                   st0, tile_offset=1, **geom)
            conv_p.append(st_r[last_tile, last_time])
            conv_s.append(jnp.concatenate([st[:, 1:], u[:, None, :]], axis=1))
        else:
            lru_v = (rg_conv_w, rg_conv_b, rg_b_a, rg_b_i, rg_lambda)
            st = rgc_state[j]
            x0, st0, h0, xs, xx, h, *w16 = _lru_mixer(
                x0, layer, mix_norm, *lru_v, (rg_w_gate, j), (rg_w_x, j), (rg_w_a, j), (rg_w_i, j), (rg_w_out, j),
                zeros((1, (RG_CONV_WIDTH - 1) * SUBLANES, D_MODEL)), zeros((1, 1, D_MODEL)), tile_offset=0,
                xs=xs, state=st.reshape(ms, (RG_CONV_WIDTH - 1) * D_MODEL), h0=rgh_state[j], **geom)
            xr, st_r, h_r = _lru_mixer(xr, layer, mix_norm, *lru_v, *[(w, 0) for w in w16], st0, h0,
                                       tile_offset=1, **geom)
            rgc_p.append(st_r[last_tile, last_time])
            rgh_p.append(h_r[last_tile, 0, :])
            rgc_s.append(jnp.concatenate([st[:, 1:], xx[:, None, :]], axis=1))
            rgh_s.append(h)
        norms = (ffn_norm, ple_norm, final_norm)
        x0, xs, *w16 = _ffn_ple(x0, p, layer, *norms, (ffn_w_gate, layer), (ffn_w_up, layer), (ffn_w_down, layer),
                                (ple_w_gate, layer), (ple_w_proj, layer), tm=tm, tile_offset=0, xs=xs, ps=ps)
        (xr,) = _ffn_ple(xr, p, layer, *norms, *[(w, 0) for w in w16], tm=tm, tile_offset=1)
    y = jnp.concatenate([x0, xr], axis=0)
    return (y, xs, jnp.stack(conv_p), jnp.stack(conv_s), jnp.stack(rgc_p), jnp.stack(rgc_s),
            jnp.stack(rgh_p), jnp.stack(rgh_s))


def kernel(x_prompt, x_sample, p_prompt, p_sample, state_conv, state_rg_conv, state_rg_h, mix_norm, ffn_norm, ple_norm, final_norm, sc_w_in, sc_w_conv, sc_w_out, rg_w_x, rg_w_gate, rg_conv_w, rg_conv_b, rg_w_a, rg_b_a, rg_w_i, rg_b_i, rg_lambda, rg_w_out, ffn_w_gate, ffn_w_up, ffn_w_down, ple_w_gate, ple_w_proj):
    bsz, seq, _ = x_prompt.shape
    dec = x_sample.shape[0]

    def rows(v):
        return v.reshape(v.shape[0], 1, v.shape[1])

    weights = (rows(mix_norm), rows(ffn_norm), rows(ple_norm), final_norm.reshape(1, D_MODEL),
               sc_w_in, sc_w_conv, sc_w_out, rg_w_x, rg_w_gate, rg_conv_w, rows(rg_conv_b),
               rg_w_a, rows(rg_b_a), rg_w_i, rows(rg_b_i), rows(rg_lambda),
               rg_w_out, ffn_w_gate, ffn_w_up, ffn_w_down, ple_w_gate, ple_w_proj)
    rs = _row_subblock(ROW_TILE)
    y_p, y_s, conv_p, conv_s, rgc_p, rgc_s, rgh_p, rgh_s = _trunk(
        _interleave_time(x_prompt, rs).reshape(bsz * seq, D_MODEL),
        _interleave_time(p_prompt, rs).reshape(DEPTH, bsz * seq, PLE_DIM),
        x_sample.reshape(dec, D_MODEL), p_sample.reshape(DEPTH, dec, PLE_DIM),
        state_conv, state_rg_conv, state_rg_h, weights, tm=ROW_TILE, seq_len=seq)
    y_p = _deinterleave_time(y_p.reshape(bsz, seq, D_MODEL), rs)
    return (y_p, y_s.reshape(dec, 1, D_MODEL), conv_p, conv_s, rgc_p, rgc_s, rgh_p, rgh_s)
```

```python
import functools

import jax
import jax.numpy as jnp
from jax import lax
from jax.experimental import pallas as pl
from jax.experimental.pallas import tpu as pltpu

D_MODEL = 2048
DEPTH = 4
PLE_DIM = 256
SC_WIDTH = 3
RG_CONV_WIDTH = 4
LRU_BW = 256
LRU_C = 8.0
D_FF = 5632
EPS = 1e-6

SUBLANES = 8
MXU_DIM = 256
ROW_TILE = 1024
ROW_SUBBLOCKS = 2
MIX_CHUNK = LRU_BW
FFN_CHUNK = 512
PLE_CHUNK = 512
FFN_CHUNK_F32 = 256
PLE_CHUNK_F32 = 256
VMEM_LIMIT = 60 * 1024 * 1024

_F32 = jnp.float32
_BF16 = jnp.bfloat16


def _dot(a, b):
    return jnp.dot(a, b, preferred_element_type=_F32)


def _w(ref):
    w = ref[...]
    return w if w.dtype == _BF16 else w.astype(_BF16)


def _mxu_weights(w_refs, w16_refs=None):
    if w16_refs is None:
        return [_w(r) for r in w_refs]
    for src, dst in zip(w_refs, w16_refs):
        dst[...] = _w(src)
    return [dst[...] for dst in w16_refs]


def _rmsnorm(x, g):
    y = x * lax.rsqrt(jnp.mean(x * x, axis=-1, keepdims=True) + EPS)
    return y * g


def _sigmoid(x):
    return jax.nn.sigmoid(x)


def _gelu_tanh(x):
    c = 0.7978845608028654
    return 0.5 * x * (1.0 + jnp.tanh(c * (x + 0.044715 * (x * x * x))))


def _softplus(x):
    return jnp.maximum(x, 0.0) + jnp.log1p(jnp.exp(-jnp.abs(x)))


def _interleave_time(x, rs):
    *lead, t, c = x.shape
    y = x.reshape(*lead, t // rs, SUBLANES, rs // SUBLANES, c)
    return jnp.swapaxes(y, -2, -3).reshape(x.shape)


def _deinterleave_time(x, rs):
    *lead, t, c = x.shape
    y = x.reshape(*lead, t // rs, rs // SUBLANES, SUBLANES, c)
    return jnp.swapaxes(y, -2, -3).reshape(x.shape)


def _time_shifts(buf_ref, vals, prev_tail, width):
    rs = vals.shape[0]
    head = (width - 1) * SUBLANES
    tail = vals[rs - head:rs, :]
    sub = lax.broadcasted_iota(jnp.int32, (SUBLANES, vals.shape[1]), 0)
    for v in range(width - 1):
        grp = slice(v * SUBLANES, (v + 1) * SUBLANES)
        mixed = jnp.where(sub == SUBLANES - 1, prev_tail[grp, :], tail[grp, :])
        buf_ref[grp, :] = pltpu.roll(mixed, 1, 0)
    buf_ref[head:head + rs, :] = vals
    shifted = [buf_ref[head - k * SUBLANES:head - k * SUBLANES + rs, :] for k in range(width - 1, 0, -1)]
    return shifted, tail


def _interleaved_scan(a, b, h_in, hbuf_ref, pbuf_ref):
    rs, tn = a.shape
    groups = rs // SUBLANES
    h = b[0:SUBLANES, :]
    p = a[0:SUBLANES, :]
    hbuf_ref[0:SUBLANES, :] = h
    pbuf_ref[0:SUBLANES, :] = p
    for q in range(1, groups):
        grp = slice(q * SUBLANES, (q + 1) * SUBLANES)
        h = a[grp, :] * h + b[grp, :]
        p = a[grp, :] * p
        hbuf_ref[grp, :] = h
        pbuf_ref[grp, :] = p
    sub = lax.broadcasted_iota(jnp.int32, (SUBLANES, tn), 0)
    carry_s = h_in
    carry = jnp.broadcast_to(h_in, (SUBLANES, tn))
    for s in range(1, SUBLANES):
        carry_s = p[s - 1:s, :] * carry_s + h[s - 1:s, :]
        carry = jnp.where(sub == s, carry_s, carry)
    h_out = p[SUBLANES - 1:SUBLANES, :] * carry_s + h[SUBLANES - 1:SUBLANES, :]
    hs = hbuf_ref[...] + pbuf_ref[...] * jnp.concatenate([carry] * groups, axis=0)
    return hs, h_out


def _row_subblock(tm):
    return tm // ROW_SUBBLOCKS if tm >= ROW_SUBBLOCKS * MXU_DIM else tm


def _start_mixer(x_ref, g_ref, o_ref, hn_ref):
    @pl.when(pl.program_id(1) == 0)
    def _():
        x = x_ref[...]
        hn_ref[...] = _rmsnorm(x, g_ref[...]).astype(_BF16)
        o_ref[...] = x


def _entering(first, cin_ref, carry_ref, j):
    held = jnp.where(pl.program_id(0) == 0, cin_ref[...], carry_ref[j])
    return jnp.where(first, 0.0, held)


def _col_blocks(lead, offset, shape):
    return pl.BlockSpec((None,) + shape, lambda i, c: (lead, 0, offset + c))


def _row_blocks(lead, shape):
    return pl.BlockSpec((None,) + shape, lambda i, c: (lead, c, 0))


def _row_tile_spec(tm, nt, tile_offset):
    mode = dict(pipeline_mode=pl.Buffered(1)) if nt == 1 else {}
    return pl.BlockSpec((tm, D_MODEL), lambda i, c: (i + tile_offset, 0), **mode)


_ALIAS_ROWS = {0: 0}


def _compiler_params():
    return pltpu.CompilerParams(dimension_semantics=("arbitrary", "arbitrary"), vmem_limit_bytes=VMEM_LIMIT)


def _conv_kernel(*refs, tm, tiles_per_seq, tile_offset, sample):
    if sample:
        (x_ref, g_ref, wb_ref, wc_ref, wx_ref, cw_ref, wo_ref, cin_ref, xs_ref, s0_ref, s1_ref,
         o_ref, st_ref, os_ref, us_ref, wb16_ref, wc16_ref, wx16_ref, wo16_ref,
         hn_ref, ubuf_ref, carry_ref, hns_ref) = refs
    else:
        (x_ref, g_ref, wb_ref, wc_ref, wx_ref, cw_ref, wo_ref, cin_ref,
         o_ref, st_ref, hn_ref, ubuf_ref, carry_ref) = refs
    i, j = pl.program_id(0), pl.program_id(1)
    _start_mixer(x_ref, g_ref, o_ref, hn_ref)
    wb, wc, wx, wo = _mxu_weights((wb_ref, wc_ref, wx_ref, wo_ref),
                                  (wb16_ref, wc16_ref, wx16_ref, wo16_ref) if sample else None)
    cw = cw_ref[...]
    first = ((i + tile_offset) % tiles_per_seq) == 0
    tail = _entering(first, cin_ref, carry_ref, j)
    rs = _row_subblock(tm)
    ups = []
    for r0 in range(0, tm, rs):
        hn = hn_ref[r0:r0 + rs, :]
        ups.append((_dot(hn, wb), _dot(hn, wc), _dot(hn, wx)))
    for k, (b_gate, c_gate, xin) in enumerate(ups):
        u = c_gate * xin
        (u2, u1), tail = _time_shifts(ubuf_ref.at[k], u, tail, SC_WIDTH)
        conv = u2 * cw[0:1, :] + u1 * cw[1:2, :] + u * cw[2:3, :]
        o_ref[k * rs:(k + 1) * rs, :] += _dot((b_gate * conv).astype(_BF16), wo)
    carry_ref[j] = tail
    st_ref[...] = tail
    if sample:
        _start_mixer(xs_ref, g_ref, os_ref, hns_ref)
        hn = hns_ref[...]
        b_gate = _dot(hn, wb)
        u = _dot(hn, wc) * _dot(hn, wx)
        conv = s0_ref[...] * cw[0:1, :] + s1_ref[...] * cw[1:2, :] + u * cw[2:3, :]
        us_ref[...] = u
        os_ref[...] += _dot((b_gate * conv).astype(_BF16), wo)


def _conv_mixer(x, layer, g, w_conv, w_b, w_c, w_x, w_out, cin, *, tm, seq_len, tile_offset,
                xs=None, state=None):
    m = x.shape[0]
    tn = MIX_CHUNK
    nj = D_MODEL // tn
    j = layer // 2
    head = (SC_WIDTH - 1) * SUBLANES
    rs = _row_subblock(tm)
    nt = m // tm - tile_offset
    sample = xs is not None
    x_spec = _row_tile_spec(tm, nt, tile_offset)
    in_specs = [
        x_spec,
        pl.BlockSpec((None, 1, D_MODEL), lambda i, c: (layer, 0, 0)),
        _col_blocks(w_b[1], w_b[2], (D_MODEL, tn)),
        _col_blocks(w_c[1], w_c[2], (D_MODEL, tn)),
        _col_blocks(w_x[1], w_x[2], (D_MODEL, tn)),
        pl.BlockSpec((None, SC_WIDTH, tn), lambda i, c: (j, 0, c)),
        _row_blocks(w_out[1], (tn, D_MODEL)),
        _col_blocks(0, 0, (head, tn)),
    ]
    args = [x, g, w_b[0], w_c[0], w_x[0], w_conv, w_out[0], cin]
    out_shape = [jax.ShapeDtypeStruct((m, D_MODEL), _F32), jax.ShapeDtypeStruct((nt, head, D_MODEL), _F32)]
    out_specs = [x_spec, pl.BlockSpec((None, head, tn), lambda i, c: (i, 0, c))]
    scratch = [pltpu.VMEM((tm, D_MODEL), _BF16), pltpu.VMEM((tm // rs, head + rs, tn), _F32),
               pltpu.VMEM((nj, head, tn), _F32)]
    if sample:
        ms = xs.shape[0]
        xs_spec = pl.BlockSpec((ms, D_MODEL), lambda i, c: (0, 0))
        in_specs += [xs_spec, pl.BlockSpec((ms, tn), lambda i, c: (0, c)),
                     pl.BlockSpec((ms, tn), lambda i, c: (0, nj + c))]
        args += [xs, state, state]
        w16 = jax.ShapeDtypeStruct((1, D_MODEL, D_MODEL), _BF16)
        out_shape += [jax.ShapeDtypeStruct((ms, D_MODEL), _F32), jax.ShapeDtypeStruct((ms, D_MODEL), _F32),
                      w16, w16, w16, w16]
        out_specs += [xs_spec, pl.BlockSpec((ms, tn), lambda i, c: (0, c)),
                      _col_blocks(0, 0, (D_MODEL, tn)), _col_blocks(0, 0, (D_MODEL, tn)),
                      _col_blocks(0, 0, (D_MODEL, tn)), _row_blocks(0, (tn, D_MODEL))]
        scratch += [pltpu.VMEM((ms, D_MODEL), _BF16)]
    kern = functools.partial(_conv_kernel, tm=tm, tiles_per_seq=seq_len // tm, tile_offset=tile_offset,
                             sample=sample)
    return pl.pallas_call(
        kern, grid=(nt, nj), in_specs=in_specs, out_specs=out_specs, out_shape=out_shape,
        scratch_shapes=scratch, compiler_params=_compiler_params(), input_output_aliases=_ALIAS_ROWS,
        name=f"conv_mixer_l{layer}_m{nt * tm}",
    )(*args)


def _lru_gates(u, wa, ba, wi, bi, sp_neg_lam):
    ub = u.astype(_BF16)
    r = _sigmoid(_dot(ub, wa) + ba)
    gate_i = _sigmoid(_dot(ub, wi) + bi)
    log_a = (-LRU_C * r) * sp_neg_lam
    a = jnp.exp(log_a)
    mult = jnp.sqrt(-jnp.tanh(log_a) * (a * a + 1.0))
    return a, mult * gate_i * u


def _lru_kernel(*refs, tm, tiles_per_seq, tile_offset, sample):
    if sample:
        (x_ref, g_ref, wg_ref, wx_ref, cw_ref, cb_ref, wa_ref, ba_ref, wi_ref, bi_ref, lam_ref, wo_ref,
         cin_ref, hin_ref, xs_ref, s0_ref, s1_ref, s2_ref, h0_ref,
         o_ref, rgc_ref, hl_ref, os_ref, xxs_ref, hs_ref,
         wg16_ref, wx16_ref, wa16_ref, wi16_ref, wo16_ref,
         hn_ref, xbuf_ref, hbuf_ref, pbuf_ref, carry_ref, hcarry_ref, hns_ref) = refs
    else:
        (x_ref, g_ref, wg_ref, wx_ref, cw_ref, cb_ref, wa_ref, ba_ref, wi_ref, bi_ref, lam_ref, wo_ref,
         cin_ref, hin_ref, o_ref, rgc_ref, hl_ref,
         hn_ref, xbuf_ref, hbuf_ref, pbuf_ref, carry_ref, hcarry_ref) = refs
    i, j = pl.program_id(0), pl.program_id(1)
    _start_mixer(x_ref, g_ref, o_ref, hn_ref)
    wg, wx, wo, wa, wi = _mxu_weights((wg_ref, wx_ref, wo_ref, wa_ref, wi_ref),
                                      (wg16_ref, wx16_ref, wo16_ref, wa16_ref, wi16_ref) if sample else None)
    cw, cb = cw_ref[...], cb_ref[...]
    gate_params = (wa, ba_ref[...], wi, bi_ref[...], _softplus(-lam_ref[...]))
    first = ((i + tile_offset) % tiles_per_seq) == 0
    tail = _entering(first, cin_ref, carry_ref, j)
    h_state = _entering(first, hin_ref, hcarry_ref, j)
    rs = _row_subblock(tm)
    ups = []
    for r0 in range(0, tm, rs):
        hn = hn_ref[r0:r0 + rs, :]
        ups.append((_dot(hn, wg), _dot(hn, wx)))
    for k, (gate_pre, xx) in enumerate(ups):
        gate = _gelu_tanh(gate_pre)
        (x3, x2, x1), tail = _time_shifts(xbuf_ref.at[k], xx, tail, RG_CONV_WIDTH)
        u = (x3 * cw[0:1, :] + x2 * cw[1:2, :] + x1 * cw[2:3, :] + xx * cw[3:4, :]) + cb
        a, b = _lru_gates(u, *gate_params)
        hs, h_state = _interleaved_scan(a, b, h_state, hbuf_ref.at[k], pbuf_ref.at[k])
        o_ref[k * rs:(k + 1) * rs, :] += _dot((gate * hs).astype(_BF16), wo)
    carry_ref[j] = tail
    rgc_ref[...] = tail
    hcarry_ref[j] = h_state
    hl_ref[...] = h_state
    if sample:
        _start_mixer(xs_ref, g_ref, os_ref, hns_ref)
        hn = hns_ref[...]
        gate = _gelu_tanh(_dot(hn, wg))
        xx = _dot(hn, wx)
        u = (s0_ref[...] * cw[0:1, :] + s1_ref[...] * cw[1:2, :] + s2_ref[...] * cw[2:3, :]
             + xx * cw[3:4, :]) + cb
        a, b = _lru_gates(u, *gate_params)
        hs = b + a * h0_ref[...]
        xxs_ref[...] = xx
        hs_ref[...] = hs
        os_ref[...] += _dot((gate * hs).astype(_BF16), wo)


def _lru_mixer(x, layer, g, conv_w, conv_b, b_a, b_i, lam, w_gate, w_x, w_a, w_i, w_out, cin, hin,
               *, tm, seq_len, tile_offset, xs=None, state=None, h0=None):
    m = x.shape[0]
    tn = MIX_CHUNK
    nj = D_MODEL // tn
    j = layer // 2
    head = (RG_CONV_WIDTH - 1) * SUBLANES
    rs = _row_subblock(tm)
    nt = m // tm - tile_offset
    sample = xs is not None

    def gate_blocks(lead):
        return pl.BlockSpec((None, None, LRU_BW, LRU_BW), lambda i, c: (lead, c, 0, 0))

    def vec_blocks():
        return pl.BlockSpec((None, 1, tn), lambda i, c: (j, 0, c))

    x_spec = _row_tile_spec(tm, nt, tile_offset)
    in_specs = [
        x_spec,
        pl.BlockSpec((None, 1, D_MODEL), lambda i, c: (layer, 0, 0)),
        _col_blocks(w_gate[1], 0, (D_MODEL, tn)),
        _col_blocks(w_x[1], 0, (D_MODEL, tn)),
        pl.BlockSpec((None, RG_CONV_WIDTH, tn), lambda i, c: (j, 0, c)),
        vec_blocks(),
        gate_blocks(w_a[1]),
        vec_blocks(),
        gate_blocks(w_i[1]),
        vec_blocks(),
        vec_blocks(),
        _row_blocks(w_out[1], (tn, D_MODEL)),
        _col_blocks(0, 0, (head, tn)),
        _col_blocks(0, 0, (1, tn)),
    ]
    args = [x, g, w_gate[0], w_x[0], conv_w, conv_b, w_a[0], b_a, w_i[0], b_i, lam, w_out[0], cin, hin]
    out_shape = [jax.ShapeDtypeStruct((m, D_MODEL), _F32), jax.ShapeDtypeStruct((nt, head, D_MODEL), _F32),
                 jax.ShapeDtypeStruct((nt, 1, D_MODEL), _F32)]
    out_specs = [x_spec, pl.BlockSpec((None, head, tn), lambda i, c: (i, 0, c)),
                 pl.BlockSpec((None, 1, tn), lambda i, c: (i, 0, c))]
    scratch = [pltpu.VMEM((tm, D_MODEL), _BF16), pltpu.VMEM((tm // rs, head + rs, tn), _F32),
               pltpu.VMEM((tm // rs, rs, tn), _F32), pltpu.VMEM((tm // rs, rs, tn), _F32),
               pltpu.VMEM((nj, head, tn), _F32), pltpu.VMEM((nj, 1, tn), _F32)]
    if sample:
        ms = xs.shape[0]
        xs_spec = pl.BlockSpec((ms, D_MODEL), lambda i, c: (0, 0))
        col_spec = pl.BlockSpec((ms, tn), lambda i, c: (0, c))
        in_specs += [xs_spec, col_spec, pl.BlockSpec((ms, tn), lambda i, c: (0, nj + c)),
                     pl.BlockSpec((ms, tn), lambda i, c: (0, 2 * nj + c)), col_spec]
        args += [xs, state, state, state, h0]
        w16 = jax.ShapeDtypeStruct((1, D_MODEL, D_MODEL), _BF16)
        g16 = jax.ShapeDtypeStruct((1, nj, LRU_BW, LRU_BW), _BF16)
        row = jax.ShapeDtypeStruct((ms, D_MODEL), _F32)
        out_shape += [row, row, row, w16, w16, g16, g16, w16]
        out_specs += [xs_spec, col_spec, col_spec,
                      _col_blocks(0, 0, (D_MODEL, tn)), _col_blocks(0, 0, (D_MODEL, tn)),
                      gate_blocks(0), gate_blocks(0), _row_blocks(0, (tn, D_MODEL))]
        scratch += [pltpu.VMEM((ms, D_MODEL), _BF16)]
    kern = functools.partial(_lru_kernel, tm=tm, tiles_per_seq=seq_len // tm, tile_offset=tile_offset,
                             sample=sample)
    return pl.pallas_call(
        kern, grid=(nt, nj), in_specs=in_specs, out_specs=out_specs, out_shape=out_shape,
        scratch_shapes=scratch, compiler_params=_compiler_params(), input_output_aliases=_ALIAS_ROWS,
        name=f"lru_mixer_l{layer}_m{nt * tm}",
    )(*args)


def _ffn_ple_kernel(*refs, n_ffn, tp, final, sample):
    if sample:
        (x_ref, gf_ref, wg_ref, wu_ref, wd_ref, gp_ref, pwg_ref, p_ref, pwp_ref, gl_ref, xs_ref, ps_ref,
         o_ref, os_ref, wg16_ref, wu16_ref, wd16_ref, pwg16_ref, pwp16_ref, hn_ref, hns_ref) = refs
        groups = ((x_ref, p_ref, o_ref, hn_ref), (xs_ref, ps_ref, os_ref, hns_ref))
    else:
        (x_ref, gf_ref, wg_ref, wu_ref, wd_ref, gp_ref, pwg_ref, p_ref, pwp_ref, gl_ref,
         o_ref, hn_ref) = refs
        groups = ((x_ref, p_ref, o_ref, hn_ref),)
    c = pl.program_id(1)
    for xg_ref, _, og_ref, hg_ref in groups:
        _start_mixer(xg_ref, gf_ref, og_ref, hg_ref)

    @pl.when(c < n_ffn)
    def _():
        wg, wu, wd = _mxu_weights((wg_ref, wu_ref, wd_ref),
                                  (wg16_ref, wu16_ref, wd16_ref) if sample else None)
        for _, _, og_ref, hg_ref in groups:
            tm = hg_ref.shape[0]
            rs = _row_subblock(tm)
            for r0 in range(0, tm, rs):
                hn = hg_ref[r0:r0 + rs, :]
                gt = _dot(hn, wg)
                up = _dot(hn, wu)
                h = (gt * _sigmoid(gt)) * up
                og_ref[r0:r0 + rs, :] += _dot(h.astype(_BF16), wd)

    @pl.when(c == n_ffn)
    def _():
        for _, _, og_ref, hg_ref in groups:
            hg_ref[...] = _rmsnorm(og_ref[...], gp_ref[...]).astype(_BF16)

    @pl.when(c >= n_ffn)
    def _():
        cols = pl.ds(pl.multiple_of((c - n_ffn) * tp, tp), tp)
        wg, wp = _mxu_weights((pwg_ref, pwp_ref), (pwg16_ref, pwp16_ref) if sample else None)
        for _, pg_ref, og_ref, hg_ref in groups:
            tm = hg_ref.shape[0]
            rs = _row_subblock(tm)
            for r0 in range(0, tm, rs):
                gate = _sigmoid(_dot(hg_ref[r0:r0 + rs, :], wg))
                proj = _dot(pg_ref[r0:r0 + rs, :].astype(_BF16), wp)
                og_ref[r0:r0 + rs, cols] += gate * proj

    if final:
        @pl.when(c == pl.num_programs(1) - 1)
        def _():
            for _, _, og_ref, _ in groups:
                og_ref[...] = _rmsnorm(og_ref[...], gl_ref[...])


def _ffn_ple(x, p, layer, g_ffn, g_ple, g_final, w_gate, w_up, w_down, pw_gate, pw_proj,
             *, tm, tile_offset, xs=None, ps=None):
    m = x.shape[0]
    nt = m // tm - tile_offset
    sample = xs is not None
    tf, tp =(FFN_CHUNK_F32, PLE_CHUNK_F32) if sample else (FFN_CHUNK, PLE_CHUNK)
    n_ffn, n_ple = D_FF // tf, D_MODEL // tp

    def ffn_c(c):
        return jnp.minimum(c, n_ffn - 1)

    def ple_c(c):
        return jnp.maximum(c - n_ffn, 0)

    def ffn_cols(lead):
        return pl.BlockSpec((None, D_MODEL, tf), lambda i, c: (lead, 0, ffn_c(c)))

    def ffn_rows(lead):
        return pl.BlockSpec((None, tf, D_MODEL), lambda i, c: (lead, ffn_c(c), 0))

    def ple_cols(lead, rows):
        return pl.BlockSpec((None, rows, tp), lambda i, c: (lead, 0, ple_c(c)))

    x_spec = _row_tile_spec(tm, nt, tile_offset)
    in_specs = [
        x_spec,
        pl.BlockSpec((None, 1, D_MODEL), lambda i, c: (layer, 0, 0)),
        ffn_cols(w_gate[1]), ffn_cols(w_up[1]), ffn_rows(w_down[1]),
        pl.BlockSpec((None, 1, D_MODEL), lambda i, c: (layer, 0, 0)),
        ple_cols(pw_gate[1], D_MODEL),
        pl.BlockSpec((None, tm, PLE_DIM), lambda i, c: (layer, i + tile_offset, 0)),
        ple_cols(pw_proj[1], PLE_DIM),
        pl.BlockSpec((1, D_MODEL), lambda i, c: (0, 0)),
    ]
    args = [x, g_ffn, w_gate[0], w_up[0], w_down[0], g_ple, pw_gate[0], p, pw_proj[0], g_final]
    out_specs = [x_spec]
    out_shape = [jax.ShapeDtypeStruct((m, D_MODEL), _F32)]
    scratch = [pltpu.VMEM((tm, D_MODEL), _BF16)]
    if sample:
        ms = xs.shape[0]
        xs_spec = pl.BlockSpec((ms, D_MODEL), lambda i, c: (0, 0))
        in_specs += [xs_spec, pl.BlockSpec((None, ms, PLE_DIM), lambda i, c: (layer, 0, 0))]
        args += [xs, ps]
        out_specs += [xs_spec, ffn_cols(0), ffn_cols(0), ffn_rows(0), ple_cols(0, D_MODEL), ple_cols(0, PLE_DIM)]
        out_shape += [jax.ShapeDtypeStruct((ms, D_MODEL), _F32),
                      jax.ShapeDtypeStruct((1, D_MODEL, D_FF), _BF16), jax.ShapeDtypeStruct((1, D_MODEL, D_FF), _BF16),
                      jax.ShapeDtypeStruct((1, D_FF, D_MODEL), _BF16),
                      jax.ShapeDtypeStruct((1, D_MODEL, D_MODEL), _BF16),
                      jax.ShapeDtypeStruct((1, PLE_DIM, D_MODEL), _BF16)]
        scratch += [pltpu.VMEM((ms, D_MODEL), _BF16)]
    kern = functools.partial(_ffn_ple_kernel, n_ffn=n_ffn, tp=tp, final=(layer == DEPTH - 1), sample=sample)
    return pl.pallas_call(
        kern, grid=(nt, n_ffn + n_ple), in_specs=in_specs, out_specs=out_specs, out_shape=out_shape,
        scratch_shapes=scratch, compiler_params=_compiler_params(), input_output_aliases=_ALIAS_ROWS,
        name=f"ffn_ple_l{layer}_m{nt * tm}",
    )(*args)


def _trunk(x, p, xs, ps, conv_state, rgc_state, rgh_state, weights, *, tm, seq_len):
    (mix_norm, ffn_norm, ple_norm, final_norm, sc_w_in, sc_w_conv, sc_w_out,
     rg_w_x, rg_w_gate, rg_conv_w, rg_conv_b, rg_w_a, rg_b_a, rg_w_i, rg_b_i, rg_lambda, rg_w_out,
     ffn_w_gate, ffn_w_up, ffn_w_down, ple_w_gate, ple_w_proj) = weights
    ms = xs.shape[0]
    tiles_per_seq = seq_len // tm
    assert tiles_per_seq >= 2, "the head call's row tile must not end a sequence"
    last_tile = slice(tiles_per_seq - 2, None, tiles_per_seq)
    last_time = slice(SUBLANES - 1, None, SUBLANES)
    nj = D_MODEL // MIX_CHUNK
    x0, xr = x[:tm], x
    zeros = functools.partial(jnp.zeros, dtype=_F32)
    geom = dict(tm=tm, seq_len=seq_len)
    conv_p, conv_s, rgc_p, rgc_s, rgh_p, rgh_s = [], [], [], [], [], []
    for layer in range(DEPTH):
        j = layer // 2
        if layer % 2 == 0:
            st = conv_state[j]
            x0, st0, xs, u, *w16 = _conv_mixer(
                x0, layer, mix_norm, sc_w_conv, (sc_w_in, j, 0), (sc_w_in, j, nj), (sc_w_in, j, 2 * nj),
                (sc_w_out, j), zeros((1, (SC_WIDTH - 1) * SUBLANES, D_MODEL)), tile_offset=0, xs=xs,
                state=st.reshape(ms, (SC_WIDTH - 1) * D_MODEL), **geom)
            wb, wc, wx, wo = w16
            xr, st_r = _conv_mixer(xr, layer, mix_norm, sc_w_conv, (wb, 0, 0), (wc, 0, 0), (wx, 0, 0), (wo, 0),
                                   st0, tile_offset=1, **geom)
            conv_p.append(st_r[last_tile, last_time])
            conv_s.append(jnp.concatenate([st[:, 1:], u[:, None, :]], axis=1))
        else:
            lru_v = (rg_conv_w, rg_conv_b, rg_b_a, rg_b_i, rg_lambda)
            st = rgc_state[j]
            x0, st0, h0, xs, xx, h, *w16 = _lru_mixer(
                x0, layer, mix_norm, *lru_v, (rg_w_gate, j), (rg_w_x, j), (rg_w_a, j), (rg_w_i, j), (rg_w_out, j),
                zeros((1, (RG_CONV_WIDTH - 1) * SUBLANES, D_MODEL)), zeros((1, 1, D_MODEL)), tile_offset=0,
                xs=xs, state=st.reshape(ms, (RG_CONV_WIDTH - 1) * D_MODEL), h0=rgh_state[j], **geom)
            xr, st_r, h_r = _lru_mixer(xr, layer, mix_norm, *lru_v, *[(w, 0) for w in w16], st0, h0,
                                       tile_offset=1, **geom)
            rgc_p.append(st_r[last_tile, last_time])
            rgh_p.append(h_r[last_tile, 0, :])
            rgc_s.append(jnp.concatenate([st[:, 1:], xx[:, None, :]], axis=1))
            rgh_s.append(h)
        norms = (ffn_norm, ple_norm, final_norm)
        x0, xs, *w16 = _ffn_ple(x0, p, layer, *norms, (ffn_w_gate, layer), (ffn_w_up, layer), (ffn_w_down, layer),
                                (ple_w_gate, layer), (ple_w_proj, layer), tm=tm, tile_offset=0, xs=xs, ps=ps)
        (xr,) = _ffn_ple(xr, p, layer, *norms, *[(w, 0) for w in w16], tm=tm, tile_offset=1)
    y = lax.dynamic_update_slice(xr, x0, (0, 0))
    return (y, xs, jnp.stack(conv_p), jnp.stack(conv_s), jnp.stack(rgc_p), jnp.stack(rgc_s),
            jnp.stack(rgh_p), jnp.stack(rgh_s))


def kernel(x_prompt, x_sample, p_prompt, p_sample, state_conv, state_rg_conv, state_rg_h, mix_norm, ffn_norm, ple_norm, final_norm, sc_w_in, sc_w_conv, sc_w_out, rg_w_x, rg_w_gate, rg_conv_w, rg_conv_b, rg_w_a, rg_b_a, rg_w_i, rg_b_i, rg_lambda, rg_w_out, ffn_w_gate, ffn_w_up, ffn_w_down, ple_w_gate, ple_w_proj):
    bsz, seq, _ = x_prompt.shape
    dec = x_sample.shape[0]

    def rows(v):
        return v.reshape(v.shape[0], 1, v.shape[1])

    weights = (rows(mix_norm), rows(ffn_norm), rows(ple_norm), final_norm.reshape(1, D_MODEL),
               sc_w_in, sc_w_conv, sc_w_out, rg_w_x, rg_w_gate, rg_conv_w, rows(rg_conv_b),
               rg_w_a, rows(rg_b_a), rg_w_i, rows(rg_b_i), rows(rg_lambda),
               rg_w_out, ffn_w_gate, ffn_w_up, ffn_w_down, ple_w_gate, ple_w_proj)
    rs = _row_subblock(ROW_TILE)
    y_p, y_s, conv_p, conv_s, rgc_p, rgc_s, rgh_p, rgh_s = _trunk(
        _interleave_time(x_prompt, rs).reshape(bsz * seq, D_MODEL),
        _interleave_time(p_prompt, rs).reshape(DEPTH, bsz * seq, PLE_DIM),
        x_sample.reshape(dec, D_MODEL), p_sample.reshape(DEPTH, dec, PLE_DIM),
        state_conv, state_rg_conv, state_rg_h, weights, tm=ROW_TILE, seq_len=seq)
    y_p = _deinterleave_time(y_p.reshape(bsz, seq, D_MODEL), rs)
    return (y_p, y_s.reshape(dec, 1, D_MODEL), conv_p, conv_s, rgc_p, rgc_s, rgh_p, rgh_s)
```

```python
import functools

import jax
import jax.numpy as jnp
from jax import lax
from jax.experimental import pallas as pl
from jax.experimental.pallas import tpu as pltpu

D_MODEL = 2048
DEPTH = 4
PLE_DIM = 256
SC_WIDTH = 3
RG_CONV_WIDTH = 4
LRU_BW = 256
LRU_C = 8.0
D_FF = 5632
EPS = 1e-6

SUBLANES = 8
MXU_DIM = 256
ROW_TILE = 1024
ROW_SUBBLOCKS = 2
MIX_CHUNK = LRU_BW
FFN_CHUNK = 512
PLE_CHUNK = 512
FFN_CHUNK_F32 = 256
PLE_CHUNK_F32 = 256
VMEM_LIMIT = 60 * 1024 * 1024

_F32 = jnp.float32
_BF16 = jnp.bfloat16


def _dot(a, b):
    return jnp.dot(a, b, preferred_element_type=_F32)


def _w(ref):
    w = ref[...]
    return w if w.dtype == _BF16 else w.astype(_BF16)


def _mxu_weights(w_refs, w16_refs=None):
    if w16_refs is None:
        return [_w(r) for r in w_refs]
    for src, dst in zip(w_refs, w16_refs):
        dst[...] = _w(src)
    return [dst[...] for dst in w16_refs]


def _rmsnorm(x, g):
    y = x * lax.rsqrt(jnp.mean(x * x, axis=-1, keepdims=True) + EPS)
    return y * g


def _sigmoid(x):
    return jax.nn.sigmoid(x)


def _gelu_tanh(x):
    c = 0.7978845608028654
    return 0.5 * x * (1.0 + jnp.tanh(c * (x + 0.044715 * (x * x * x))))


def _softplus(x):
    return jnp.maximum(x, 0.0) + jnp.log1p(jnp.exp(-jnp.abs(x)))


def _interleave_time(x, rs):
    *lead, t, c = x.shape
    y = x.reshape(*lead, t // rs, SUBLANES, rs // SUBLANES, c)
    return jnp.swapaxes(y, -2, -3).reshape(x.shape)


def _deinterleave_time(x, rs):
    *lead, t, c = x.shape
    y = x.reshape(*lead, t // rs, rs // SUBLANES, SUBLANES, c)
    return jnp.swapaxes(y, -2, -3).reshape(x.shape)


def _time_shifts(buf_ref, vals, prev_tail, width):
    rs = vals.shape[0]
    head = (width - 1) * SUBLANES
    tail = vals[rs - head:rs, :]
    sub = lax.broadcasted_iota(jnp.int32, (SUBLANES, vals.shape[1]), 0)
    for v in range(width - 1):
        grp = slice(v * SUBLANES, (v + 1) * SUBLANES)
        mixed = jnp.where(sub == SUBLANES - 1, prev_tail[grp, :], tail[grp, :])
        buf_ref[grp, :] = pltpu.roll(mixed, 1, 0)
    buf_ref[head:head + rs, :] = vals
    shifted = [buf_ref[head - k * SUBLANES:head - k * SUBLANES + rs, :] for k in range(width - 1, 0, -1)]
    return shifted, tail


def _interleaved_scan(a, b, h_in, hbuf_ref, pbuf_ref):
    rs, tn = a.shape
    groups = rs // SUBLANES
    h = b[0:SUBLANES, :]
    p = a[0:SUBLANES, :]
    hbuf_ref[0:SUBLANES, :] = h
    pbuf_ref[0:SUBLANES, :] = p
    for q in range(1, groups):
        grp = slice(q * SUBLANES, (q + 1) * SUBLANES)
        h = a[grp, :] * h + b[grp, :]
        p = a[grp, :] * p
        hbuf_ref[grp, :] = h
        pbuf_ref[grp, :] = p
    sub = lax.broadcasted_iota(jnp.int32, (SUBLANES, tn), 0)
    carry_s = h_in
    carry = jnp.broadcast_to(h_in, (SUBLANES, tn))
    for s in range(1, SUBLANES):
        carry_s = p[s - 1:s, :] * carry_s + h[s - 1:s, :]
        carry = jnp.where(sub == s, carry_s, carry)
    h_out = p[SUBLANES - 1:SUBLANES, :] * carry_s + h[SUBLANES - 1:SUBLANES, :]
    hs = hbuf_ref[...] + pbuf_ref[...] * jnp.concatenate([carry] * groups, axis=0)
    return hs, h_out


def _row_subblock(tm):
    return tm // ROW_SUBBLOCKS if tm >= ROW_SUBBLOCKS * MXU_DIM else tm


def _row_blocks_of(tm, rows):
    rs = _row_subblock(tm)
    bounds = [(r0, r0 + rs) for r0 in range(0, tm, rs)]
    bounds[-1] = (bounds[-1][0], rows)
    return bounds


def _start_mixer(x_ref, g_ref, o_ref, hn_ref):
    @pl.when(pl.program_id(1) == 0)
    def _():
        x = x_ref[...]
        hn_ref[...] = _rmsnorm(x, g_ref[...]).astype(_BF16)
        o_ref[...] = x


def _entering(first, cin_ref, carry_ref, j):
    held = jnp.where(pl.program_id(0) == 0, cin_ref[...], carry_ref[j])
    return jnp.where(first, 0.0, held)


def _col_blocks(lead, offset, shape):
    return pl.BlockSpec((None,) + shape, lambda i, c: (lead, 0, offset + c))


def _row_blocks(lead, shape):
    return pl.BlockSpec((None,) + shape, lambda i, c: (lead, c, 0))


def _row_tile_spec(tm, nt, tile_offset):
    mode = dict(pipeline_mode=pl.Buffered(1)) if nt == 1 else {}
    return pl.BlockSpec((tm, D_MODEL), lambda i, c: (i + tile_offset, 0), **mode)


_ALIAS_ROWS = {0: 0}


def _compiler_params():
    return pltpu.CompilerParams(dimension_semantics=("arbitrary", "arbitrary"), vmem_limit_bytes=VMEM_LIMIT)


def _conv_kernel(*refs, tm, tiles_per_seq, tile_offset, sample):
    if sample:
        (x_ref, g_ref, wb_ref, wc_ref, wx_ref, cw_ref, wo_ref, cin_ref, xs_ref, s0_ref, s1_ref,
         o_ref, st_ref, os_ref, us_ref, wb16_ref, wc16_ref, wx16_ref, wo16_ref,
         hn_ref, ubuf_ref, carry_ref) = refs
        _start_mixer(xs_ref, g_ref, os_ref, hn_ref.at[tm:hn_ref.shape[0]])
    else:
        (x_ref, g_ref, wb_ref, wc_ref, wx_ref, cw_ref, wo_ref, cin_ref,
         o_ref, st_ref, hn_ref, ubuf_ref, carry_ref) = refs
    i, j = pl.program_id(0), pl.program_id(1)
    _start_mixer(x_ref, g_ref, o_ref, hn_ref.at[0:tm])
    wb, wc, wx, wo = _mxu_weights((wb_ref, wc_ref, wx_ref, wo_ref),
                                  (wb16_ref, wc16_ref, wx16_ref, wo16_ref) if sample else None)
    cw = cw_ref[...]
    first = ((i + tile_offset) % tiles_per_seq) == 0
    tail = _entering(first, cin_ref, carry_ref, j)
    rs = _row_subblock(tm)
    bounds = _row_blocks_of(tm, hn_ref.shape[0])
    ups = []
    for r0, r1 in bounds:
        hn = hn_ref[r0:r1, :]
        ups.append((_dot(hn, wb), _dot(hn, wc), _dot(hn, wx)))
    for k, ((r0, r1), (b_gate, c_gate, xin)) in enumerate(zip(bounds, ups)):
        u = c_gate[0:rs] * xin[0:rs]
        (u2, u1), tail = _time_shifts(ubuf_ref.at[k], u, tail, SC_WIDTH)
        conv = u2 * cw[0:1, :] + u1 * cw[1:2, :] + u * cw[2:3, :]
        y = (b_gate[0:rs] * conv).astype(_BF16)
        if r1 - r0 > rs:
            u = c_gate[rs:] * xin[rs:]
            conv = s0_ref[...] * cw[0:1, :] + s1_ref[...] * cw[1:2, :] + u * cw[2:3, :]
            us_ref[...] = u
            y = jnp.concatenate([y, (b_gate[rs:] * conv).astype(_BF16)], axis=0)
        out = _dot(y, wo)
        o_ref[r0:r0 + rs, :] += out[0:rs]
        if r1 - r0 > rs:
            os_ref[...] += out[rs:]
    carry_ref[j] = tail
    st_ref[...] = tail


def _conv_mixer(x, layer, g, w_conv, w_b, w_c, w_x, w_out, cin, *, tm, seq_len, tile_offset,
                xs=None, state=None):
    m = x.shape[0]
    tn = MIX_CHUNK
    nj = D_MODEL // tn
    j = layer // 2
    head = (SC_WIDTH - 1) * SUBLANES
    rs = _row_subblock(tm)
    nt = m // tm - tile_offset
    sample = xs is not None
    ms = xs.shape[0] if sample else 0
    x_spec = _row_tile_spec(tm, nt, tile_offset)
    in_specs = [
        x_spec,
        pl.BlockSpec((None, 1, D_MODEL), lambda i, c: (layer, 0, 0)),
        _col_blocks(w_b[1], w_b[2], (D_MODEL, tn)),
        _col_blocks(w_c[1], w_c[2], (D_MODEL, tn)),
        _col_blocks(w_x[1], w_x[2], (D_MODEL, tn)),
        pl.BlockSpec((None, SC_WIDTH, tn), lambda i, c: (j, 0, c)),
        _row_blocks(w_out[1], (tn, D_MODEL)),
        _col_blocks(0, 0, (head, tn)),
    ]
    args = [x, g, w_b[0], w_c[0], w_x[0], w_conv, w_out[0], cin]
    out_shape = [jax.ShapeDtypeStruct((m, D_MODEL), _F32), jax.ShapeDtypeStruct((nt, head, D_MODEL), _F32)]
    out_specs = [x_spec, pl.BlockSpec((None, head, tn), lambda i, c: (i, 0, c))]
    scratch = [pltpu.VMEM((tm + ms, D_MODEL), _BF16), pltpu.VMEM((tm // rs, head + rs, tn), _F32),
               pltpu.VMEM((nj, head, tn), _F32)]
    if sample:
        xs_spec = pl.BlockSpec((ms, D_MODEL), lambda i, c: (0, 0))
        in_specs += [xs_spec, pl.BlockSpec((ms, tn), lambda i, c: (0, c)),
                     pl.BlockSpec((ms, tn), lambda i, c: (0, nj + c))]
        args += [xs, state, state]
        w16 = jax.ShapeDtypeStruct((1, D_MODEL, D_MODEL), _BF16)
        out_shape += [jax.ShapeDtypeStruct((ms, D_MODEL), _F32), jax.ShapeDtypeStruct((ms, D_MODEL), _F32),
                      w16, w16, w16, w16]
        out_specs += [xs_spec, pl.BlockSpec((ms, tn), lambda i, c: (0, c)),
                      _col_blocks(0, 0, (D_MODEL, tn)), _col_blocks(0, 0, (D_MODEL, tn)),
                      _col_blocks(0, 0, (D_MODEL, tn)), _row_blocks(0, (tn, D_MODEL))]
    kern = functools.partial(_conv_kernel, tm=tm, tiles_per_seq=seq_len // tm, tile_offset=tile_offset,
                             sample=sample)
    return pl.pallas_call(
        kern, grid=(nt, nj), in_specs=in_specs, out_specs=out_specs, out_shape=out_shape,
        scratch_shapes=scratch, compiler_params=_compiler_params(), input_output_aliases=_ALIAS_ROWS,
        name=f"conv_mixer_l{layer}_m{nt * tm}",
    )(*args)


def _lru_gates(u, wa, ba, wi, bi, sp_neg_lam):
    ub = u.astype(_BF16)
    r = _sigmoid(_dot(ub, wa) + ba)
    gate_i = _sigmoid(_dot(ub, wi) + bi)
    log_a = (-LRU_C * r) * sp_neg_lam
    a = jnp.exp(log_a)
    mult = jnp.sqrt(-jnp.tanh(log_a) * (a * a + 1.0))
    return a, mult * gate_i * u


def _lru_kernel(*refs, tm, tiles_per_seq, tile_offset, sample):
    if sample:
        (x_ref, g_ref, wg_ref, wx_ref, cw_ref, cb_ref, wa_ref, ba_ref, wi_ref, bi_ref, lam_ref, wo_ref,
         cin_ref, hin_ref, xs_ref, s0_ref, s1_ref, s2_ref, h0_ref,
         o_ref, rgc_ref, hl_ref, os_ref, xxs_ref, hs_ref,
         wg16_ref, wx16_ref, wa16_ref, wi16_ref, wo16_ref,
         hn_ref, xbuf_ref, hbuf_ref, pbuf_ref, carry_ref, hcarry_ref) = refs
        _start_mixer(xs_ref, g_ref, os_ref, hn_ref.at[tm:hn_ref.shape[0]])
    else:
        (x_ref, g_ref, wg_ref, wx_ref, cw_ref, cb_ref, wa_ref, ba_ref, wi_ref, bi_ref, lam_ref, wo_ref,
         cin_ref, hin_ref, o_ref, rgc_ref, hl_ref,
         hn_ref, xbuf_ref, hbuf_ref, pbuf_ref, carry_ref, hcarry_ref) = refs
    i, j = pl.program_id(0), pl.program_id(1)
    _start_mixer(x_ref, g_ref, o_ref, hn_ref.at[0:tm])
    wg, wx, wo, wa, wi = _mxu_weights((wg_ref, wx_ref, wo_ref, wa_ref, wi_ref),
                                      (wg16_ref, wx16_ref, wo16_ref, wa16_ref, wi16_ref) if sample else None)
    cw, cb = cw_ref[...], cb_ref[...]
    gate_params = (wa, ba_ref[...], wi, bi_ref[...], _softplus(-lam_ref[...]))
    first = ((i + tile_offset) % tiles_per_seq) == 0
    tail = _entering(first, cin_ref, carry_ref, j)
    h_state = _entering(first, hin_ref, hcarry_ref, j)
    rs = _row_subblock(tm)
    bounds = _row_blocks_of(tm, hn_ref.shape[0])
    ups = []
    for r0, r1 in bounds:
        hn = hn_ref[r0:r1, :]
        ups.append((_dot(hn, wg), _dot(hn, wx)))
    for k, ((r0, r1), (gate_pre, xx_all)) in enumerate(zip(bounds, ups)):
        with_sample = r1 - r0 > rs
        gate = _gelu_tanh(gate_pre)
        xx = xx_all[0:rs]
        (x3, x2, x1), tail = _time_shifts(xbuf_ref.at[k], xx, tail, RG_CONV_WIDTH)
        u = (x3 * cw[0:1, :] + x2 * cw[1:2, :] + x1 * cw[2:3, :] + xx * cw[3:4, :]) + cb
        if with_sample:
            xxs = xx_all[rs:]
            us = (s0_ref[...] * cw[0:1, :] + s1_ref[...] * cw[1:2, :] + s2_ref[...] * cw[2:3, :]
                  + xxs * cw[3:4, :]) + cb
            xxs_ref[...] = xxs
            u = jnp.concatenate([u, us], axis=0)
        a, b = _lru_gates(u, *gate_params)
        hs, h_state = _interleaved_scan(a[0:rs], b[0:rs], h_state, hbuf_ref.at[k], pbuf_ref.at[k])
        if with_sample:
            h_new = b[rs:] + a[rs:] * h0_ref[...]
            hs_ref[...] = h_new
            hs = jnp.concatenate([hs, h_new], axis=0)
        out = _dot((gate * hs).astype(_BF16), wo)
        o_ref[r0:r0 + rs, :] += out[0:rs]
        if with_sample:
            os_ref[...] += out[rs:]
    carry_ref[j] = tail
    rgc_ref[...] = tail
    hcarry_ref[j] = h_state
    hl_ref[...] = h_state


def _lru_mixer(x, layer, g, conv_w, conv_b, b_a, b_i, lam, w_gate, w_x, w_a, w_i, w_out, cin, hin,
               *, tm, seq_len, tile_offset, xs=None, state=None, h0=None):
    m = x.shape[0]
    tn = MIX_CHUNK
    nj = D_MODEL // tn
    j = layer // 2
    head = (RG_CONV_WIDTH - 1) * SUBLANES
    rs = _row_subblock(tm)
    nt = m // tm - tile_offset
    sample = xs is not None
    ms = xs.shape[0] if sample else 0

    def gate_blocks(lead):
        return pl.BlockSpec((None, None, LRU_BW, LRU_BW), lambda i, c: (lead, c, 0, 0))

    def vec_blocks():
        return pl.BlockSpec((None, 1, tn), lambda i, c: (j, 0, c))

    x_spec = _row_tile_spec(tm, nt, tile_offset)
    in_specs = [
        x_spec,
        pl.BlockSpec((None, 1, D_MODEL), lambda i, c: (layer, 0, 0)),
        _col_blocks(w_gate[1], 0, (D_MODEL, tn)),
        _col_blocks(w_x[1], 0, (D_MODEL, tn)),
        pl.BlockSpec((None, RG_CONV_WIDTH, tn), lambda i, c: (j, 0, c)),
        vec_blocks(),
        gate_blocks(w_a[1]),
        vec_blocks(),
        gate_blocks(w_i[1]),
        vec_blocks(),
        vec_blocks(),
        _row_blocks(w_out[1], (tn, D_MODEL)),
        _col_blocks(0, 0, (head, tn)),
        _col_blocks(0, 0, (1, tn)),
    ]
    args = [x, g, w_gate[0], w_x[0], conv_w, conv_b, w_a[0], b_a, w_i[0], b_i, lam, w_out[0], cin, hin]
    out_shape = [jax.ShapeDtypeStruct((m, D_MODEL), _F32), jax.ShapeDtypeStruct((nt, head, D_MODEL), _F32),
                 jax.ShapeDtypeStruct((nt, 1, D_MODEL), _F32)]
    out_specs = [x_spec, pl.BlockSpec((None, head, tn), lambda i, c: (i, 0, c)),
                 pl.BlockSpec((None, 1, tn), lambda i, c: (i, 0, c))]
    scratch = [pltpu.VMEM((tm + ms, D_MODEL), _BF16), pltpu.VMEM((tm // rs, head + rs, tn), _F32),
               pltpu.VMEM((tm // rs, rs, tn), _F32), pltpu.VMEM((tm // rs, rs, tn), _F32),
               pltpu.VMEM((nj, head, tn), _F32), pltpu.VMEM((nj, 1, tn), _F32)]
    if sample:
        xs_spec = pl.BlockSpec((ms, D_MODEL), lambda i, c: (0, 0))
        col_spec = pl.BlockSpec((ms, tn), lambda i, c: (0, c))
        in_specs += [xs_spec, col_spec, pl.BlockSpec((ms, tn), lambda i, c: (0, nj + c)),
                     pl.BlockSpec((ms, tn), lambda i, c: (0, 2 * nj + c)), col_spec]
        args += [xs, state, state, state, h0]
        w16 = jax.ShapeDtypeStruct((1, D_MODEL, D_MODEL), _BF16)
        g16 = jax.ShapeDtypeStruct((1, nj, LRU_BW, LRU_BW), _BF16)
        row = jax.ShapeDtypeStruct((ms, D_MODEL), _F32)
        out_shape += [row, row, row, w16, w16, g16, g16, w16]
        out_specs += [xs_spec, col_spec, col_spec,
                      _col_blocks(0, 0, (D_MODEL, tn)), _col_blocks(0, 0, (D_MODEL, tn)),
                      gate_blocks(0), gate_blocks(0), _row_blocks(0, (tn, D_MODEL))]
    kern = functools.partial(_lru_kernel, tm=tm, tiles_per_seq=seq_len // tm, tile_offset=tile_offset,
                             sample=sample)
    return pl.pallas_call(
        kern, grid=(nt, nj), in_specs=in_specs, out_specs=out_specs, out_shape=out_shape,
        scratch_shapes=scratch, compiler_params=_compiler_params(), input_output_aliases=_ALIAS_ROWS,
        name=f"lru_mixer_l{layer}_m{nt * tm}",
    )(*args)


def _ffn_ple_kernel(*refs, n_ffn, tp, final, sample):
    if sample:
        (x_ref, gf_ref, wg_ref, wu_ref, wd_ref, gp_ref, pwg_ref, p_ref, pwp_ref, gl_ref, xs_ref, ps_ref,
         o_ref, os_ref, wg16_ref, wu16_ref, wd16_ref, pwg16_ref, pwp16_ref, hn_ref) = refs
    else:
        (x_ref, gf_ref, wg_ref, wu_ref, wd_ref, gp_ref, pwg_ref, p_ref, pwp_ref, gl_ref,
         o_ref, hn_ref) = refs
    c = pl.program_id(1)
    tm = x_ref.shape[0]
    rows = hn_ref.shape[0]
    bounds = _row_blocks_of(tm, rows)
    groups = [(x_ref, o_ref, hn_ref.at[0:tm])] + ([(xs_ref, os_ref, hn_ref.at[tm:rows])] if sample else [])
    for xg_ref, og_ref, hg_ref in groups:
        _start_mixer(xg_ref, gf_ref, og_ref, hg_ref)

    def add_rows(r0, r1, cols, upd):
        top = min(r1, tm)
        o_ref[r0:top, cols] += upd[0:top - r0]
        if r1 > tm:
            os_ref[:, cols] += upd[top - r0:]

    @pl.when(c < n_ffn)
    def _():
        wg, wu, wd = _mxu_weights((wg_ref, wu_ref, wd_ref),
                                  (wg16_ref, wu16_ref, wd16_ref) if sample else None)
        for r0, r1 in bounds:
            hn = hn_ref[r0:r1, :]
            gt = _dot(hn, wg)
            up = _dot(hn, wu)
            h = (gt * _sigmoid(gt)) * up
            add_rows(r0, r1, slice(None), _dot(h.astype(_BF16), wd))

    @pl.when(c == n_ffn)
    def _():
        for _, og_ref, hg_ref in groups:
            hg_ref[...] = _rmsnorm(og_ref[...], gp_ref[...]).astype(_BF16)

    @pl.when(c >= n_ffn)
    def _():
        cols = pl.ds(pl.multiple_of((c - n_ffn) * tp, tp), tp)
        wg, wp = _mxu_weights((pwg_ref, pwp_ref), (pwg16_ref, pwp16_ref) if sample else None)
        for r0, r1 in bounds:
            gate = _sigmoid(_dot(hn_ref[r0:r1, :], wg))
            top = min(r1, tm)
            p_rows = p_ref[r0:top, :].astype(_BF16)
            if r1 > tm:
                p_rows = jnp.concatenate([p_rows, ps_ref[...].astype(_BF16)], axis=0)
            add_rows(r0, r1, cols, gate * _dot(p_rows, wp))

    if final:
        @pl.when(c == pl.num_programs(1) - 1)
        def _():
            for _, og_ref, _ in groups:
                og_ref[...] = _rmsnorm(og_ref[...], gl_ref[...])


def _ffn_ple(x, p, layer, g_ffn, g_ple, g_final, w_gate, w_up, w_down, pw_gate, pw_proj,
             *, tm, tile_offset, xs=None, ps=None):
    m = x.shape[0]
    nt = m // tm - tile_offset
    sample = xs is not None
    ms = xs.shape[0] if sample else 0
    tf, tp =(FFN_CHUNK_F32, PLE_CHUNK_F32) if sample else (FFN_CHUNK, PLE_CHUNK)
    n_ffn, n_ple = D_FF // tf, D_MODEL // tp

    def ffn_c(c):
        return jnp.minimum(c, n_ffn - 1)

    def ple_c(c):
        return jnp.maximum(c - n_ffn, 0)

    def ffn_cols(lead):
        return pl.BlockSpec((None, D_MODEL, tf), lambda i, c: (lead, 0, ffn_c(c)))

    def ffn_rows(lead):
        return pl.BlockSpec((None, tf, D_MODEL), lambda i, c: (lead, ffn_c(c), 0))

    def ple_cols(lead, rows):
        return pl.BlockSpec((None, rows, tp), lambda i, c: (lead, 0, ple_c(c)))

    x_spec = _row_tile_spec(tm, nt, tile_offset)
    in_specs = [
        x_spec,
        pl.BlockSpec((None, 1, D_MODEL), lambda i, c: (layer, 0, 0)),
        ffn_cols(w_gate[1]), ffn_cols(w_up[1]), ffn_rows(w_down[1]),
        pl.BlockSpec((None, 1, D_MODEL), lambda i, c: (layer, 0, 0)),
        ple_cols(pw_gate[1], D_MODEL),
        pl.BlockSpec((None, tm, PLE_DIM), lambda i, c: (layer, i + tile_offset, 0)),
        ple_cols(pw_proj[1], PLE_DIM),
        pl.BlockSpec((1, D_MODEL), lambda i, c: (0, 0)),
    ]
    args = [x, g_ffn, w_gate[0], w_up[0], w_down[0], g_ple, pw_gate[0], p, pw_proj[0], g_final]
    out_specs = [x_spec]
    out_shape = [jax.ShapeDtypeStruct((m, D_MODEL), _F32)]
    scratch = [pltpu.VMEM((tm + ms, D_MODEL), _BF16)]
    if sample:
        xs_spec = pl.BlockSpec((ms, D_MODEL), lambda i, c: (0, 0))
        in_specs += [xs_spec, pl.BlockSpec((None, ms, PLE_DIM), lambda i, c: (layer, 0, 0))]
        args += [xs, ps]
        out_specs += [xs_spec, ffn_cols(0), ffn_cols(0), ffn_rows(0), ple_cols(0, D_MODEL), ple_cols(0, PLE_DIM)]
        out_shape += [jax.ShapeDtypeStruct((ms, D_MODEL), _F32),
                      jax.ShapeDtypeStruct((1, D_MODEL, D_FF), _BF16), jax.ShapeDtypeStruct((1, D_MODEL, D_FF), _BF16),
                      jax.ShapeDtypeStruct((1, D_FF, D_MODEL), _BF16),
                      jax.ShapeDtypeStruct((1, D_MODEL, D_MODEL), _BF16),
                      jax.ShapeDtypeStruct((1, PLE_DIM, D_MODEL), _BF16)]
    kern = functools.partial(_ffn_ple_kernel, n_ffn=n_ffn, tp=tp, final=(layer == DEPTH - 1), sample=sample)
    return pl.pallas_call(
        kern, grid=(nt, n_ffn + n_ple), in_specs=in_specs, out_specs=out_specs, out_shape=out_shape,
        scratch_shapes=scratch, compiler_params=_compiler_params(), input_output_aliases=_ALIAS_ROWS,
        name=f"ffn_ple_l{layer}_m{nt * tm}",
    )(*args)


def _trunk(x, p, xs, ps, conv_state, rgc_state, rgh_state, weights, *, tm, seq_len):
    (mix_norm, ffn_norm, ple_norm, final_norm, sc_w_in, sc_w_conv, sc_w_out,
     rg_w_x, rg_w_gate, rg_conv_w, rg_conv_b, rg_w_a, rg_b_a, rg_w_i, rg_b_i, rg_lambda, rg_w_out,
     ffn_w_gate, ffn_w_up, ffn_w_down, ple_w_gate, ple_w_proj) = weights
    ms = xs.shape[0]
    tiles_per_seq = seq_len // tm
    assert tiles_per_seq >= 2, "the head call's row tile must not end a sequence"
    last_tile = slice(tiles_per_seq - 2, None, tiles_per_seq)
    last_time = slice(SUBLANES - 1, None, SUBLANES)
    nj = D_MODEL // MIX_CHUNK
    x0, xr = x[:tm], x
    zeros = functools.partial(jnp.zeros, dtype=_F32)
    geom = dict(tm=tm, seq_len=seq_len)
    conv_p, conv_s, rgc_p, rgc_s, rgh_p, rgh_s = [], [], [], [], [], []
    for layer in range(DEPTH):
        j = layer // 2
        if layer % 2 == 0:
            st = conv_state[j]
            x0, st0, xs, u, *w16 = _conv_mixer(
                x0, layer, mix_norm, sc_w_conv, (sc_w_in, j, 0), (sc_w_in, j, nj), (sc_w_in, j, 2 * nj),
                (sc_w_out, j), zeros((1, (SC_WIDTH - 1) * SUBLANES, D_MODEL)), tile_offset=0, xs=xs,
                state=st.reshape(ms, (SC_WIDTH - 1) * D_MODEL), **geom)
            wb, wc, wx, wo = w16
            xr, st_r = _conv_mixer(xr, layer, mix_norm, sc_w_conv, (wb, 0, 0), (wc, 0, 0), (wx, 0, 0), (wo, 0),
                                   st0, tile_offset=1, **geom)
            conv_p.append(st_r[last_tile, last_time])
            conv_s.append(jnp.concatenate([st[:, 1:], u[:, None, :]], axis=1))
        else:
            lru_v = (rg_conv_w, rg_conv_b, rg_b_a, rg_b_i, rg_lambda)
            st = rgc_state[j]
            x0, st0, h0, xs, xx, h, *w16 = _lru_mixer(
                x0, layer, mix_norm, *lru_v, (rg_w_gate, j), (rg_w_x, j), (rg_w_a, j), (rg_w_i, j), (rg_w_out, j),
                zeros((1, (RG_CONV_WIDTH - 1) * SUBLANES, D_MODEL)), zeros((1, 1, D_MODEL)), tile_offset=0,
                xs=xs, state=st.reshape(ms, (RG_CONV_WIDTH - 1) * D_MODEL), h0=rgh_state[j], **geom)
            xr, st_r, h_r = _lru_mixer(xr, layer, mix_norm, *lru_v, *[(w, 0) for w in w16], st0, h0,
                                       tile_offset=1, **geom)
            rgc_p.append(st_r[last_tile, last_time])
            rgh_p.append(h_r[last_tile, 0, :])
            rgc_s.append(jnp.concatenate([st[:, 1:], xx[:, None, :]], axis=1))
            rgh_s.append(h)
        norms = (ffn_norm, ple_norm, final_norm)
        x0, xs, *w16 = _ffn_ple(x0, p, layer, *norms, (ffn_w_gate, layer), (ffn_w_up, layer), (ffn_w_down, layer),
                                (ple_w_gate, layer), (ple_w_proj, layer), tm=tm, tile_offset=0, xs=xs, ps=ps)
        (xr,) = _ffn_ple(xr, p, layer, *norms, *[(w, 0) for w in w16], tm=tm, tile_offset=1)
    y = lax.dynamic_update_slice(xr, x0, (0, 0))
    return (y, xs, jnp.stack(conv_p), jnp.stack(conv_s), jnp.stack(rgc_p), jnp.stack(rgc_s),
            jnp.stack(rgh_p), jnp.stack(rgh_s))


def kernel(x_prompt, x_sample, p_prompt, p_sample, state_conv, state_rg_conv, state_rg_h, mix_norm, ffn_norm, ple_norm, final_norm, sc_w_in, sc_w_conv, sc_w_out, rg_w_x, rg_w_gate, rg_conv_w, rg_conv_b, rg_w_a, rg_b_a, rg_w_i, rg_b_i, rg_lambda, rg_w_out, ffn_w_gate, ffn_w_up, ffn_w_down, ple_w_gate, ple_w_proj):
    bsz, seq, _ = x_prompt.shape
    dec = x_sample.shape[0]

    def rows(v):
        return v.reshape(v.shape[0], 1, v.shape[1])

    weights = (rows(mix_norm), rows(ffn_norm), rows(ple_norm), final_norm.reshape(1, D_MODEL),
               sc_w_in, sc_w_conv, sc_w_out, rg_w_x, rg_w_gate, rg_conv_w, rows(rg_conv_b),
               rg_w_a, rows(rg_b_a), rg_w_i, rows(rg_b_i), rows(rg_lambda),
               rg_w_out, ffn_w_gate, ffn_w_up, ffn_w_down, ple_w_gate, ple_w_proj)
    rs = _row_subblock(ROW_TILE)
    y_p, y_s, conv_p, conv_s, rgc_p, rgc_s, rgh_p, rgh_s = _trunk(
        _interleave_time(x_prompt, rs).reshape(bsz * seq, D_MODEL),
        _interleave_time(p_prompt, rs).reshape(DEPTH, bsz * seq, PLE_DIM),
        x_sample.reshape(dec, D_MODEL), p_sample.reshape(DEPTH, dec, PLE_DIM),
        state_conv, state_rg_conv, state_rg_h, weights, tm=ROW_TILE, seq_len=seq)
    y_p = _deinterleave_time(y_p.reshape(bsz, seq, D_MODEL), rs)
    return (y_p, y_s.reshape(dec, 1, D_MODEL), conv_p, conv_s, rgc_p, rgc_s, rgh_p, rgh_s)
```

```python
import functools

import jax
import jax.numpy as jnp
from jax import lax
from jax.experimental import pallas as pl
from jax.experimental.pallas import tpu as pltpu

D_MODEL = 2048
DEPTH = 4
PLE_DIM = 256
SC_WIDTH = 3
RG_CONV_WIDTH = 4
LRU_BW = 256
LRU_C = 8.0
D_FF = 5632
EPS = 1e-6

SUBLANES = 8
MXU_DIM = 256
ROW_TILE = 1024
ROW_SUBBLOCKS = 2
MIX_CHUNK = LRU_BW
FFN_CHUNK = 512
PLE_CHUNK = 512
FFN_CHUNK_F32 = 256
PLE_CHUNK_F32 = 256
VMEM_LIMIT = 60 * 1024 * 1024

_F32 = jnp.float32
_BF16 = jnp.bfloat16


def _dot(a, b):
    return jnp.dot(a, b, preferred_element_type=_F32)


def _w(ref):
    w = ref[...]
    return w if w.dtype == _BF16 else w.astype(_BF16)


def _mxu_weights(w_refs, w16_refs=None):
    if w16_refs is None:
        return [_w(r) for r in w_refs]
    for src, dst in zip(w_refs, w16_refs):
        dst[...] = _w(src)
    return [dst[...] for dst in w16_refs]


def _rmsnorm(x, g):
    y = x * lax.rsqrt(jnp.mean(x * x, axis=-1, keepdims=True) + EPS)
    return y * g


def _sigmoid(x):
    return jax.nn.sigmoid(x)


def _gelu_tanh(x):
    c2 = 2.0 * 0.7978845608028654
    return x * _sigmoid(x * (c2 + (c2 * 0.044715) * (x * x)))


def _softplus(x):
    return jnp.maximum(x, 0.0) + jnp.log1p(jnp.exp(-jnp.abs(x)))


def _interleave_time(x, rs):
    *lead, t, c = x.shape
    y = x.reshape(*lead, t // rs, SUBLANES, rs // SUBLANES, c)
    return jnp.swapaxes(y, -2, -3).reshape(x.shape)


def _deinterleave_time(x, rs):
    *lead, t, c = x.shape
    y = x.reshape(*lead, t // rs, rs // SUBLANES, SUBLANES, c)
    return jnp.swapaxes(y, -2, -3).reshape(x.shape)


def _time_shifts(buf_ref, vals, prev_tail, width):
    rs = vals.shape[0]
    head = (width - 1) * SUBLANES
    tail = vals[rs - head:rs, :]
    sub = lax.broadcasted_iota(jnp.int32, (SUBLANES, vals.shape[1]), 0)
    for v in range(width - 1):
        grp = slice(v * SUBLANES, (v + 1) * SUBLANES)
        mixed = jnp.where(sub == SUBLANES - 1, prev_tail[grp, :], tail[grp, :])
        buf_ref[grp, :] = pltpu.roll(mixed, 1, 0)
    buf_ref[head:head + rs, :] = vals
    shifted = [buf_ref[head - k * SUBLANES:head - k * SUBLANES + rs, :] for k in range(width - 1, 0, -1)]
    return shifted, tail


def _interleaved_scan(a, b, h_in, hbuf_ref, pbuf_ref):
    rs, tn = a.shape
    groups = rs // SUBLANES
    h = b[0:SUBLANES, :]
    p = a[0:SUBLANES, :]
    hbuf_ref[0:SUBLANES, :] = h
    pbuf_ref[0:SUBLANES, :] = p
    for q in range(1, groups):
        grp = slice(q * SUBLANES, (q + 1) * SUBLANES)
        h = a[grp, :] * h + b[grp, :]
        p = a[grp, :] * p
        hbuf_ref[grp, :] = h
        pbuf_ref[grp, :] = p
    sub = lax.broadcasted_iota(jnp.int32, (SUBLANES, tn), 0)
    carry_s = h_in
    carry = jnp.broadcast_to(h_in, (SUBLANES, tn))
    for s in range(1, SUBLANES):
        carry_s = p[s - 1:s, :] * carry_s + h[s - 1:s, :]
        carry = jnp.where(sub == s, carry_s, carry)
    h_out = p[SUBLANES - 1:SUBLANES, :] * carry_s + h[SUBLANES - 1:SUBLANES, :]
    hs = hbuf_ref[...] + pbuf_ref[...] * jnp.concatenate([carry] * groups, axis=0)
    return hs, h_out


def _row_subblock(tm):
    return tm // ROW_SUBBLOCKS if tm >= ROW_SUBBLOCKS * MXU_DIM else tm


def _row_blocks_of(tm, rows):
    rs = _row_subblock(tm)
    bounds = [(r0, r0 + rs) for r0 in range(0, tm, rs)]
    bounds[-1] = (bounds[-1][0], rows)
    return bounds


def _start_mixer(x_ref, g_ref, o_ref, hn_ref):
    @pl.when(pl.program_id(1) == 0)
    def _():
        x = x_ref[...]
        hn_ref[...] = _rmsnorm(x, g_ref[...]).astype(_BF16)
        o_ref[...] = x


def _entering(first, cin_ref, carry_ref, j):
    held = jnp.where(pl.program_id(0) == 0, cin_ref[...], carry_ref[j])
    return jnp.where(first, 0.0, held)


def _col_blocks(lead, offset, shape):
    return pl.BlockSpec((None,) + shape, lambda i, c: (lead, 0, offset + c))


def _row_blocks(lead, shape):
    return pl.BlockSpec((None,) + shape, lambda i, c: (lead, c, 0))


def _row_tile_spec(tm, nt, tile_offset):
    mode = dict(pipeline_mode=pl.Buffered(1)) if nt == 1 else {}
    return pl.BlockSpec((tm, D_MODEL), lambda i, c: (i + tile_offset, 0), **mode)


_ALIAS_ROWS = {0: 0}


def _compiler_params():
    return pltpu.CompilerParams(dimension_semantics=("arbitrary", "arbitrary"), vmem_limit_bytes=VMEM_LIMIT)


def _conv_kernel(*refs, tm, tiles_per_seq, tile_offset, sample):
    if sample:
        (x_ref, g_ref, wb_ref, wc_ref, wx_ref, cw_ref, wo_ref, cin_ref, xs_ref, s0_ref, s1_ref,
         o_ref, st_ref, os_ref, us_ref, wb16_ref, wc16_ref, wx16_ref, wo16_ref,
         hn_ref, ubuf_ref, carry_ref) = refs
        _start_mixer(xs_ref, g_ref, os_ref, hn_ref.at[tm:hn_ref.shape[0]])
    else:
        (x_ref, g_ref, wb_ref, wc_ref, wx_ref, cw_ref, wo_ref, cin_ref,
         o_ref, st_ref, hn_ref, ubuf_ref, carry_ref) = refs
    i, j = pl.program_id(0), pl.program_id(1)
    _start_mixer(x_ref, g_ref, o_ref, hn_ref.at[0:tm])
    wb, wc, wx, wo = _mxu_weights((wb_ref, wc_ref, wx_ref, wo_ref),
                                  (wb16_ref, wc16_ref, wx16_ref, wo16_ref) if sample else None)
    cw = cw_ref[...]
    first = ((i + tile_offset) % tiles_per_seq) == 0
    tail = _entering(first, cin_ref, carry_ref, j)
    rs = _row_subblock(tm)
    bounds = _row_blocks_of(tm, hn_ref.shape[0])
    ups = []
    for r0, r1 in bounds:
        hn = hn_ref[r0:r1, :]
        ups.append((_dot(hn, wb), _dot(hn, wc), _dot(hn, wx)))
    for k, ((r0, r1), (b_gate, c_gate, xin)) in enumerate(zip(bounds, ups)):
        u = c_gate[0:rs] * xin[0:rs]
        (u2, u1), tail = _time_shifts(ubuf_ref.at[k], u, tail, SC_WIDTH)
        conv = u2 * cw[0:1, :] + u1 * cw[1:2, :] + u * cw[2:3, :]
        y = (b_gate[0:rs] * conv).astype(_BF16)
        if r1 - r0 > rs:
            u = c_gate[rs:] * xin[rs:]
            conv = s0_ref[...] * cw[0:1, :] + s1_ref[...] * cw[1:2, :] + u * cw[2:3, :]
            us_ref[...] = u
            y = jnp.concatenate([y, (b_gate[rs:] * conv).astype(_BF16)], axis=0)
        out = _dot(y, wo)
        o_ref[r0:r0 + rs, :] += out[0:rs]
        if r1 - r0 > rs:
            os_ref[...] += out[rs:]
    carry_ref[j] = tail
    st_ref[...] = tail


def _conv_mixer(x, layer, g, w_conv, w_b, w_c, w_x, w_out, cin, *, tm, seq_len, tile_offset,
                xs=None, state=None):
    m = x.shape[0]
    tn = MIX_CHUNK
    nj = D_MODEL // tn
    j = layer // 2
    head = (SC_WIDTH - 1) * SUBLANES
    rs = _row_subblock(tm)
    nt = 1 if xs is not None else m // tm - tile_offset
    sample = xs is not None
    ms = xs.shape[0] if sample else 0
    x_spec = _row_tile_spec(tm, nt, tile_offset)
    in_specs = [
        x_spec,
        pl.BlockSpec((None, 1, D_MODEL), lambda i, c: (layer, 0, 0)),
        _col_blocks(w_b[1], w_b[2], (D_MODEL, tn)),
        _col_blocks(w_c[1], w_c[2], (D_MODEL, tn)),
        _col_blocks(w_x[1], w_x[2], (D_MODEL, tn)),
        pl.BlockSpec((None, SC_WIDTH, tn), lambda i, c: (j, 0, c)),
        _row_blocks(w_out[1], (tn, D_MODEL)),
        _col_blocks(0, 0, (head, tn)),
    ]
    args = [x, g, w_b[0], w_c[0], w_x[0], w_conv, w_out[0], cin]
    out_shape = [jax.ShapeDtypeStruct((m, D_MODEL), _F32), jax.ShapeDtypeStruct((nt, head, D_MODEL), _F32)]
    out_specs = [x_spec, pl.BlockSpec((None, head, tn), lambda i, c: (i, 0, c))]
    scratch = [pltpu.VMEM((tm + ms, D_MODEL), _BF16), pltpu.VMEM((tm // rs, head + rs, tn), _F32),
               pltpu.VMEM((nj, head, tn), _F32)]
    if sample:
        xs_spec = pl.BlockSpec((ms, D_MODEL), lambda i, c: (0, 0))
        in_specs += [xs_spec, pl.BlockSpec((ms, tn), lambda i, c: (0, c)),
                     pl.BlockSpec((ms, tn), lambda i, c: (0, nj + c))]
        args += [xs, state, state]
        w16 = jax.ShapeDtypeStruct((1, D_MODEL, D_MODEL), _BF16)
        out_shape += [jax.ShapeDtypeStruct((ms, D_MODEL), _F32), jax.ShapeDtypeStruct((ms, D_MODEL), _F32),
                      w16, w16, w16, w16]
        out_specs += [xs_spec, pl.BlockSpec((ms, tn), lambda i, c: (0, c)),
                      _col_blocks(0, 0, (D_MODEL, tn)), _col_blocks(0, 0, (D_MODEL, tn)),
                      _col_blocks(0, 0, (D_MODEL, tn)), _row_blocks(0, (tn, D_MODEL))]
    kern = functools.partial(_conv_kernel, tm=tm, tiles_per_seq=seq_len // tm, tile_offset=tile_offset,
                             sample=sample)
    return pl.pallas_call(
        kern, grid=(nt, nj), in_specs=in_specs, out_specs=out_specs, out_shape=out_shape,
        scratch_shapes=scratch, compiler_params=_compiler_params(), input_output_aliases=_ALIAS_ROWS,
        name=f"conv_mixer_l{layer}_m{nt * tm}",
    )(*args)


def _lru_gates(u, wa, ba, wi, bi, sp_neg_lam):
    ub = u.astype(_BF16)
    r = _sigmoid(_dot(ub, wa) + ba)
    gate_i = _sigmoid(_dot(ub, wi) + bi)
    log_a = (-LRU_C * r) * sp_neg_lam
    a = jnp.exp(log_a)
    m2 = -jnp.tanh(log_a) * (a * a + 1.0)
    mult = jnp.where(m2 > 0.0, m2 * lax.rsqrt(m2), 0.0)
    return a, mult * gate_i * u


def _lru_kernel(*refs, tm, tiles_per_seq, tile_offset, sample):
    if sample:
        (x_ref, g_ref, wg_ref, wx_ref, cw_ref, cb_ref, wa_ref, ba_ref, wi_ref, bi_ref, lam_ref, wo_ref,
         cin_ref, hin_ref, xs_ref, s0_ref, s1_ref, s2_ref, h0_ref,
         o_ref, rgc_ref, hl_ref, os_ref, xxs_ref, hs_ref,
         wg16_ref, wx16_ref, wa16_ref, wi16_ref, wo16_ref,
         hn_ref, xbuf_ref, hbuf_ref, pbuf_ref, carry_ref, hcarry_ref) = refs
        _start_mixer(xs_ref, g_ref, os_ref, hn_ref.at[tm:hn_ref.shape[0]])
    else:
        (x_ref, g_ref, wg_ref, wx_ref, cw_ref, cb_ref, wa_ref, ba_ref, wi_ref, bi_ref, lam_ref, wo_ref,
         cin_ref, hin_ref, o_ref, rgc_ref, hl_ref,
         hn_ref, xbuf_ref, hbuf_ref, pbuf_ref, carry_ref, hcarry_ref) = refs
    i, j = pl.program_id(0), pl.program_id(1)
    _start_mixer(x_ref, g_ref, o_ref, hn_ref.at[0:tm])
    wg, wx, wo, wa, wi = _mxu_weights((wg_ref, wx_ref, wo_ref, wa_ref, wi_ref),
                                      (wg16_ref, wx16_ref, wo16_ref, wa16_ref, wi16_ref) if sample else None)
    cw, cb = cw_ref[...], cb_ref[...]
    gate_params = (wa, ba_ref[...], wi, bi_ref[...], _softplus(-lam_ref[...]))
    first = ((i + tile_offset) % tiles_per_seq) == 0
    tail = _entering(first, cin_ref, carry_ref, j)
    h_state = _entering(first, hin_ref, hcarry_ref, j)
    rs = _row_subblock(tm)
    bounds = _row_blocks_of(tm, hn_ref.shape[0])
    ups = []
    for r0, r1 in bounds:
        hn = hn_ref[r0:r1, :]
        ups.append((_dot(hn, wg), _dot(hn, wx)))
    for k, ((r0, r1), (gate_pre, xx_all)) in enumerate(zip(bounds, ups)):
        with_sample = r1 - r0 > rs
        gate = _gelu_tanh(gate_pre)
        xx = xx_all[0:rs]
        (x3, x2, x1), tail = _time_shifts(xbuf_ref.at[k], xx, tail, RG_CONV_WIDTH)
        u = (x3 * cw[0:1, :] + x2 * cw[1:2, :] + x1 * cw[2:3, :] + xx * cw[3:4, :]) + cb
        if with_sample:
            xxs = xx_all[rs:]
            us = (s0_ref[...] * cw[0:1, :] + s1_ref[...] * cw[1:2, :] + s2_ref[...] * cw[2:3, :]
                  + xxs * cw[3:4, :]) + cb
            xxs_ref[...] = xxs
            u = jnp.concatenate([u, us], axis=0)
        a, b = _lru_gates(u, *gate_params)
        hs, h_state = _interleaved_scan(a[0:rs], b[0:rs], h_state, hbuf_ref.at[k], pbuf_ref.at[k])
        if with_sample:
            h_new = b[rs:] + a[rs:] * h0_ref[...]
            hs_ref[...] = h_new
            hs = jnp.concatenate([hs, h_new], axis=0)
        out = _dot((gate * hs).astype(_BF16), wo)
        o_ref[r0:r0 + rs, :] += out[0:rs]
        if with_sample:
            os_ref[...] += out[rs:]
    carry_ref[j] = tail
    rgc_ref[...] = tail
    hcarry_ref[j] = h_state
    hl_ref[...] = h_state


def _lru_mixer(x, layer, g, conv_w, conv_b, b_a, b_i, lam, w_gate, w_x, w_a, w_i, w_out, cin, hin,
               *, tm, seq_len, tile_offset, xs=None, state=None, h0=None):
    m = x.shape[0]
    tn = MIX_CHUNK
    nj = D_MODEL // tn
    j = layer // 2
    head = (RG_CONV_WIDTH - 1) * SUBLANES
    rs = _row_subblock(tm)
    nt = 1 if xs is not None else m // tm - tile_offset
    sample = xs is not None
    ms = xs.shape[0] if sample else 0

    def gate_blocks(lead):
        return pl.BlockSpec((None, None, LRU_BW, LRU_BW), lambda i, c: (lead, c, 0, 0))

    def vec_blocks():
        return pl.BlockSpec((None, 1, tn), lambda i, c: (j, 0, c))

    x_spec = _row_tile_spec(tm, nt, tile_offset)
    in_specs = [
        x_spec,
        pl.BlockSpec((None, 1, D_MODEL), lambda i, c: (layer, 0, 0)),
        _col_blocks(w_gate[1], 0, (D_MODEL, tn)),
        _col_blocks(w_x[1], 0, (D_MODEL, tn)),
        pl.BlockSpec((None, RG_CONV_WIDTH, tn), lambda i, c: (j, 0, c)),
        vec_blocks(),
        gate_blocks(w_a[1]),
        vec_blocks(),
        gate_blocks(w_i[1]),
        vec_blocks(),
        vec_blocks(),
        _row_blocks(w_out[1], (tn, D_MODEL)),
        _col_blocks(0, 0, (head, tn)),
        _col_blocks(0, 0, (1, tn)),
    ]
    args = [x, g, w_gate[0], w_x[0], conv_w, conv_b, w_a[0], b_a, w_i[0], b_i, lam, w_out[0], cin, hin]
    out_shape = [jax.ShapeDtypeStruct((m, D_MODEL), _F32), jax.ShapeDtypeStruct((nt, head, D_MODEL), _F32),
                 jax.ShapeDtypeStruct((nt, 1, D_MODEL), _F32)]
    out_specs = [x_spec, pl.BlockSpec((None, head, tn), lambda i, c: (i, 0, c)),
                 pl.BlockSpec((None, 1, tn), lambda i, c: (i, 0, c))]
    scratch = [pltpu.VMEM((tm + ms, D_MODEL), _BF16), pltpu.VMEM((tm // rs, head + rs, tn), _F32),
               pltpu.VMEM((tm // rs, rs, tn), _F32), pltpu.VMEM((tm // rs, rs, tn), _F32),
               pltpu.VMEM((nj, head, tn), _F32), pltpu.VMEM((nj, 1, tn), _F32)]
    if sample:
        xs_spec = pl.BlockSpec((ms, D_MODEL), lambda i, c: (0, 0))
        col_spec = pl.BlockSpec((ms, tn), lambda i, c: (0, c))
        in_specs += [xs_spec, col_spec, pl.BlockSpec((ms, tn), lambda i, c: (0, nj + c)),
                     pl.BlockSpec((ms, tn), lambda i, c: (0, 2 * nj + c)), col_spec]
        args += [xs, state, state, state, h0]
        w16 = jax.ShapeDtypeStruct((1, D_MODEL, D_MODEL), _BF16)
        g16 = jax.ShapeDtypeStruct((1, nj, LRU_BW, LRU_BW), _BF16)
        row = jax.ShapeDtypeStruct((ms, D_MODEL), _F32)
        out_shape += [row, row, row, w16, w16, g16, g16, w16]
        out_specs += [xs_spec, col_spec, col_spec,
                      _col_blocks(0, 0, (D_MODEL, tn)), _col_blocks(0, 0, (D_MODEL, tn)),
                      gate_blocks(0), gate_blocks(0), _row_blocks(0, (tn, D_MODEL))]
    kern = functools.partial(_lru_kernel, tm=tm, tiles_per_seq=seq_len // tm, tile_offset=tile_offset,
                             sample=sample)
    return pl.pallas_call(
        kern, grid=(nt, nj), in_specs=in_specs, out_specs=out_specs, out_shape=out_shape,
        scratch_shapes=scratch, compiler_params=_compiler_params(), input_output_aliases=_ALIAS_ROWS,
        name=f"lru_mixer_l{layer}_m{nt * tm}",
    )(*args)


def _ffn_ple_kernel(*refs, n_ffn, tp, final, sample):
    if sample:
        (x_ref, gf_ref, wg_ref, wu_ref, wd_ref, gp_ref, pwg_ref, p_ref, pwp_ref, gl_ref, xs_ref, ps_ref,
         o_ref, os_ref, wg16_ref, wu16_ref, wd16_ref, pwg16_ref, pwp16_ref, hn_ref) = refs
    else:
        (x_ref, gf_ref, wg_ref, wu_ref, wd_ref, gp_ref, pwg_ref, p_ref, pwp_ref, gl_ref,
         o_ref, hn_ref) = refs
    c = pl.program_id(1)
    tm = x_ref.shape[0]
    rows = hn_ref.shape[0]
    bounds = _row_blocks_of(tm, rows)
    groups = [(x_ref, o_ref, hn_ref.at[0:tm])] + ([(xs_ref, os_ref, hn_ref.at[tm:rows])] if sample else [])
    for xg_ref, og_ref, hg_ref in groups:
        _start_mixer(xg_ref, gf_ref, og_ref, hg_ref)

    def add_rows(r0, r1, cols, upd):
        top = min(r1, tm)
        o_ref[r0:top, cols] += upd[0:top - r0]
        if r1 > tm:
            os_ref[:, cols] += upd[top - r0:]

    @pl.when(c < n_ffn)
    def _():
        wg, wu, wd = _mxu_weights((wg_ref, wu_ref, wd_ref),
                                  (wg16_ref, wu16_ref, wd16_ref) if sample else None)
        for r0, r1 in bounds:
            hn = hn_ref[r0:r1, :]
            gt = _dot(hn, wg)
            up = _dot(hn, wu)
            h = (gt * _sigmoid(gt)) * up
            add_rows(r0, r1, slice(None), _dot(h.astype(_BF16), wd))

    @pl.when(c == n_ffn)
    def _():
        for _, og_ref, hg_ref in groups:
            hg_ref[...] = _rmsnorm(og_ref[...], gp_ref[...]).astype(_BF16)

    @pl.when(c >= n_ffn)
    def _():
        cols = pl.ds(pl.multiple_of((c - n_ffn) * tp, tp), tp)
        wg, wp = _mxu_weights((pwg_ref, pwp_ref), (pwg16_ref, pwp16_ref) if sample else None)
        for r0, r1 in bounds:
            gate = _sigmoid(_dot(hn_ref[r0:r1, :], wg))
            top = min(r1, tm)
            p_rows = p_ref[r0:top, :].astype(_BF16)
            if r1 > tm:
                p_rows = jnp.concatenate([p_rows, ps_ref[...].astype(_BF16)], axis=0)
            add_rows(r0, r1, cols, gate * _dot(p_rows, wp))

    if final:
        @pl.when(c == pl.num_programs(1) - 1)
        def _():
            for _, og_ref, _ in groups:
                og_ref[...] = _rmsnorm(og_ref[...], gl_ref[...])


def _ffn_ple(x, p, layer, g_ffn, g_ple, g_final, w_gate, w_up, w_down, pw_gate, pw_proj,
             *, tm, tile_offset, xs=None, ps=None):
    m = x.shape[0]
    nt = 1 if xs is not None else m // tm - tile_offset
    sample = xs is not None
    ms = xs.shape[0] if sample else 0
    tf, tp =(FFN_CHUNK_F32, PLE_CHUNK_F32) if sample else (FFN_CHUNK, PLE_CHUNK)
    n_ffn, n_ple = D_FF // tf, D_MODEL // tp

    def ffn_c(c):
        return jnp.minimum(c, n_ffn - 1)

    def ple_c(c):
        return jnp.maximum(c - n_ffn, 0)

    def ffn_cols(lead):
        return pl.BlockSpec((None, D_MODEL, tf), lambda i, c: (lead, 0, ffn_c(c)))

    def ffn_rows(lead):
        return pl.BlockSpec((None, tf, D_MODEL), lambda i, c: (lead, ffn_c(c), 0))

    def ple_cols(lead, rows):
        return pl.BlockSpec((None, rows, tp), lambda i, c: (lead, 0, ple_c(c)))

    x_spec = _row_tile_spec(tm, nt, tile_offset)
    in_specs = [
        x_spec,
        pl.BlockSpec((None, 1, D_MODEL), lambda i, c: (layer, 0, 0)),
        ffn_cols(w_gate[1]), ffn_cols(w_up[1]), ffn_rows(w_down[1]),
        pl.BlockSpec((None, 1, D_MODEL), lambda i, c: (layer, 0, 0)),
        ple_cols(pw_gate[1], D_MODEL),
        pl.BlockSpec((None, tm, PLE_DIM), lambda i, c: (layer, i + tile_offset, 0)),
        ple_cols(pw_proj[1], PLE_DIM),
        pl.BlockSpec((1, D_MODEL), lambda i, c: (0, 0)),
    ]
    args = [x, g_ffn, w_gate[0], w_up[0], w_down[0], g_ple, pw_gate[0], p, pw_proj[0], g_final]
    out_specs = [x_spec]
    out_shape = [jax.ShapeDtypeStruct((m, D_MODEL), _F32)]
    scratch = [pltpu.VMEM((tm + ms, D_MODEL), _BF16)]
    if sample:
        xs_spec = pl.BlockSpec((ms, D_MODEL), lambda i, c: (0, 0))
        in_specs += [xs_spec, pl.BlockSpec((None, ms, PLE_DIM), lambda i, c: (layer, 0, 0))]
        args += [xs, ps]
        out_specs += [xs_spec, ffn_cols(0), ffn_cols(0), ffn_rows(0), ple_cols(0, D_MODEL), ple_cols(0, PLE_DIM)]
        out_shape += [jax.ShapeDtypeStruct((ms, D_MODEL), _F32),
                      jax.ShapeDtypeStruct((1, D_MODEL, D_FF), _BF16), jax.ShapeDtypeStruct((1, D_MODEL, D_FF), _BF16),
                      jax.ShapeDtypeStruct((1, D_FF, D_MODEL), _BF16),
                      jax.ShapeDtypeStruct((1, D_MODEL, D_MODEL), _BF16),
                      jax.ShapeDtypeStruct((1, PLE_DIM, D_MODEL), _BF16)]
    kern = functools.partial(_ffn_ple_kernel, n_ffn=n_ffn, tp=tp, final=(layer == DEPTH - 1), sample=sample)
    return pl.pallas_call(
        kern, grid=(nt, n_ffn + n_ple), in_specs=in_specs, out_specs=out_specs, out_shape=out_shape,
        scratch_shapes=scratch, compiler_params=_compiler_params(), input_output_aliases=_ALIAS_ROWS,
        name=f"ffn_ple_l{layer}_m{nt * tm}",
    )(*args)


def _trunk(x, p, xs, ps, conv_state, rgc_state, rgh_state, weights, *, tm, seq_len):
    (mix_norm, ffn_norm, ple_norm, final_norm, sc_w_in, sc_w_conv, sc_w_out,
     rg_w_x, rg_w_gate, rg_conv_w, rg_conv_b, rg_w_a, rg_b_a, rg_w_i, rg_b_i, rg_lambda, rg_w_out,
     ffn_w_gate, ffn_w_up, ffn_w_down, ple_w_gate, ple_w_proj) = weights
    ms = xs.shape[0]
    tiles_per_seq = seq_len // tm
    assert tiles_per_seq >= 2, "the head call's row tile must not end a sequence"
    last_tile = slice(tiles_per_seq - 2, None, tiles_per_seq)
    last_time = slice(SUBLANES - 1, None, SUBLANES)
    nj = D_MODEL // MIX_CHUNK
    zeros = functools.partial(jnp.zeros, dtype=_F32)
    geom = dict(tm=tm, seq_len=seq_len)
    conv_p, conv_s, rgc_p, rgc_s, rgh_p, rgh_s = [], [], [], [], [], []
    for layer in range(DEPTH):
        j = layer // 2
        if layer % 2 == 0:
            st = conv_state[j]
            x, st0, xs, u, *w16 = _conv_mixer(
                x, layer, mix_norm, sc_w_conv, (sc_w_in, j, 0), (sc_w_in, j, nj), (sc_w_in, j, 2 * nj),
                (sc_w_out, j), zeros((1, (SC_WIDTH - 1) * SUBLANES, D_MODEL)), tile_offset=0, xs=xs,
                state=st.reshape(ms, (SC_WIDTH - 1) * D_MODEL), **geom)
            wb, wc, wx, wo = w16
            x, st_r = _conv_mixer(x, layer, mix_norm, sc_w_conv, (wb, 0, 0), (wc, 0, 0), (wx, 0, 0), (wo, 0),
                                   st0, tile_offset=1, **geom)
            conv_p.append(st_r[last_tile, last_time])
            conv_s.append(jnp.concatenate([st[:, 1:], u[:, None, :]], axis=1))
        else:
            lru_v = (rg_conv_w, rg_conv_b, rg_b_a, rg_b_i, rg_lambda)
            st = rgc_state[j]
            x, st0, h0, xs, xx, h, *w16 = _lru_mixer(
                x, layer, mix_norm, *lru_v, (rg_w_gate, j), (rg_w_x, j), (rg_w_a, j), (rg_w_i, j), (rg_w_out, j),
                zeros((1, (RG_CONV_WIDTH - 1) * SUBLANES, D_MODEL)), zeros((1, 1, D_MODEL)), tile_offset=0,
                xs=xs, state=st.reshape(ms, (RG_CONV_WIDTH - 1) * D_MODEL), h0=rgh_state[j], **geom)
            x, st_r, h_r = _lru_mixer(x, layer, mix_norm, *lru_v, *[(w, 0) for w in w16], st0, h0,
                                       tile_offset=1, **geom)
            rgc_p.append(st_r[last_tile, last_time])
            rgh_p.append(h_r[last_tile, 0, :])
            rgc_s.append(jnp.concatenate([st[:, 1:], xx[:, None, :]], axis=1))
            rgh_s.append(h)
        norms = (ffn_norm, ple_norm, final_norm)
        x, xs, *w16 = _ffn_ple(x, p, layer, *norms, (ffn_w_gate, layer), (ffn_w_up, layer), (ffn_w_down, layer),
                                (ple_w_gate, layer), (ple_w_proj, layer), tm=tm, tile_offset=0, xs=xs, ps=ps)
        (x,) = _ffn_ple(x, p, layer, *norms, *[(w, 0) for w in w16], tm=tm, tile_offset=1)
    return (x, xs, jnp.stack(conv_p), jnp.stack(conv_s), jnp.stack(rgc_p), jnp.stack(rgc_s),
            jnp.stack(rgh_p), jnp.stack(rgh_s))


def kernel(x_prompt, x_sample, p_prompt, p_sample, state_conv, state_rg_conv, state_rg_h, mix_norm, ffn_norm, ple_norm, final_norm, sc_w_in, sc_w_conv, sc_w_out, rg_w_x, rg_w_gate, rg_conv_w, rg_conv_b, rg_w_a, rg_b_a, rg_w_i, rg_b_i, rg_lambda, rg_w_out, ffn_w_gate, ffn_w_up, ffn_w_down, ple_w_gate, ple_w_proj):
    bsz, seq, _ = x_prompt.shape
    dec = x_sample.shape[0]

    def rows(v):
        return v.reshape(v.shape[0], 1, v.shape[1])

    weights = (rows(mix_norm), rows(ffn_norm), rows(ple_norm), final_norm.reshape(1, D_MODEL),
               sc_w_in, sc_w_conv, sc_w_out, rg_w_x, rg_w_gate, rg_conv_w, rows(rg_conv_b),
               rg_w_a, rows(rg_b_a), rg_w_i, rows(rg_b_i), rows(rg_lambda),
               rg_w_out, ffn_w_gate, ffn_w_up, ffn_w_down, ple_w_gate, ple_w_proj)
    rs = _row_subblock(ROW_TILE)
    y_p, y_s, conv_p, conv_s, rgc_p, rgc_s, rgh_p, rgh_s = _trunk(
        _interleave_time(x_prompt, rs).reshape(bsz * seq, D_MODEL),
        _interleave_time(p_prompt, rs).reshape(DEPTH, bsz * seq, PLE_DIM),
        x_sample.reshape(dec, D_MODEL), p_sample.reshape(DEPTH, dec, PLE_DIM),
        state_conv, state_rg_conv, state_rg_h, weights, tm=ROW_TILE, seq_len=seq)
    y_p = _deinterleave_time(y_p.reshape(bsz, seq, D_MODEL), rs)
    return (y_p, y_s.reshape(dec, 1, D_MODEL), conv_p, conv_s, rgc_p, rgc_s, rgh_p, rgh_s)
```

```python
import functools

import jax
import jax.numpy as jnp
from jax import lax
from jax.experimental import pallas as pl
from jax.experimental.pallas import tpu as pltpu

D_MODEL = 2048
DEPTH = 4
PLE_DIM = 256
SC_WIDTH = 3
RG_CONV_WIDTH = 4
LRU_BW = 256
LRU_C = 8.0
D_FF = 5632
EPS = 1e-6

SUBLANES = 8
MXU_DIM = 256
ROW_TILE = 1024
ROW_SUBBLOCKS = 2
MIX_CHUNK = LRU_BW
FFN_CHUNK = 512
PLE_CHUNK = 512
FFN_CHUNK_F32 = 256
PLE_CHUNK_F32 = 256
VMEM_LIMIT = 60 * 1024 * 1024

_F32 = jnp.float32
_BF16 = jnp.bfloat16


def _dot(a, b):
    return jnp.dot(a, b, preferred_element_type=_F32)


def _w(ref):
    w = ref[...]
    return w if w.dtype == _BF16 else w.astype(_BF16)


def _mxu_weights(w_refs, w16_refs=None):
    if w16_refs is None:
        return [_w(r) for r in w_refs]
    for src, dst in zip(w_refs, w16_refs):
        dst[...] = _w(src)
    return [dst[...] for dst in w16_refs]


def _rmsnorm(x, g):
    y = x * lax.rsqrt(jnp.mean(x * x, axis=-1, keepdims=True) + EPS)
    return y * g


def _sigmoid(x):
    return jax.nn.sigmoid(x)


def _gelu_tanh(x):
    c2 = 2.0 * 0.7978845608028654
    return x * _sigmoid(x * (c2 + (c2 * 0.044715) * (x * x)))


def _softplus(x):
    return jnp.maximum(x, 0.0) + jnp.log1p(jnp.exp(-jnp.abs(x)))


def _interleave_time(x, rs):
    *lead, t, c = x.shape
    y = x.reshape(*lead, t // rs, SUBLANES, rs // SUBLANES, c)
    return jnp.swapaxes(y, -2, -3).reshape(x.shape)


def _deinterleave_time(x, rs):
    *lead, t, c = x.shape
    y = x.reshape(*lead, t // rs, rs // SUBLANES, SUBLANES, c)
    return jnp.swapaxes(y, -2, -3).reshape(x.shape)


def _time_shifts(buf_ref, vals, prev_tail, width):
    rs = vals.shape[0]
    head = (width - 1) * SUBLANES
    tail = vals[rs - head:rs, :]
    sub = lax.broadcasted_iota(jnp.int32, (SUBLANES, vals.shape[1]), 0)
    for v in range(width - 1):
        grp = slice(v * SUBLANES, (v + 1) * SUBLANES)
        mixed = jnp.where(sub == SUBLANES - 1, prev_tail[grp, :], tail[grp, :])
        buf_ref[grp, :] = pltpu.roll(mixed, 1, 0)
    buf_ref[head:head + rs, :] = vals
    shifted = [buf_ref[head - k * SUBLANES:head - k * SUBLANES + rs, :] for k in range(width - 1, 0, -1)]
    return shifted, tail


def _interleaved_scan(a, b, h_in, hbuf_ref, pbuf_ref):
    rs, tn = a.shape
    groups = rs // SUBLANES
    h = b[0:SUBLANES, :]
    p = a[0:SUBLANES, :]
    hbuf_ref[0:SUBLANES, :] = h
    pbuf_ref[0:SUBLANES, :] = p
    for q in range(1, groups):
        grp = slice(q * SUBLANES, (q + 1) * SUBLANES)
        h = a[grp, :] * h + b[grp, :]
        p = a[grp, :] * p
        hbuf_ref[grp, :] = h
        pbuf_ref[grp, :] = p
    sub = lax.broadcasted_iota(jnp.int32, (SUBLANES, tn), 0)
    carry_s = h_in
    carry = jnp.broadcast_to(h_in, (SUBLANES, tn))
    for s in range(1, SUBLANES):
        carry_s = p[s - 1:s, :] * carry_s + h[s - 1:s, :]
        carry = jnp.where(sub == s, carry_s, carry)
    h_out = p[SUBLANES - 1:SUBLANES, :] * carry_s + h[SUBLANES - 1:SUBLANES, :]
    hs = hbuf_ref[...] + pbuf_ref[...] * jnp.concatenate([carry] * groups, axis=0)
    return hs, h_out


def _row_subblock(tm):
    return tm // ROW_SUBBLOCKS if tm >= ROW_SUBBLOCKS * MXU_DIM else tm


def _row_blocks_of(tm, rows):
    rs = _row_subblock(tm)
    bounds = [(r0, r0 + rs) for r0 in range(0, tm, rs)]
    bounds[-1] = (bounds[-1][0], rows)
    return bounds


def _start_mixer(x_ref, g_ref, o_ref, hn_ref):
    @pl.when(pl.program_id(1) == 0)
    def _():
        x = x_ref[...]
        hn_ref[...] = _rmsnorm(x, g_ref[...]).astype(_BF16)
        o_ref[...] = x


def _entering(first, cin_ref, carry_ref, j):
    held = jnp.where(pl.program_id(0) == 0, cin_ref[...], carry_ref[j])
    return jnp.where(first, 0.0, held)


def _col_blocks(lead, offset, shape):
    return pl.BlockSpec((None,) + shape, lambda i, c: (lead, 0, offset + c))


def _row_blocks(lead, shape):
    return pl.BlockSpec((None,) + shape, lambda i, c: (lead, c, 0))


def _row_tile_spec(tm, nt, tile_offset):
    mode = dict(pipeline_mode=pl.Buffered(1)) if nt == 1 else {}
    return pl.BlockSpec((tm, D_MODEL), lambda i, c: (i + tile_offset, 0), **mode)


_ALIAS_ROWS = {0: 0}


def _compiler_params():
    return pltpu.CompilerParams(dimension_semantics=("arbitrary", "arbitrary"), vmem_limit_bytes=VMEM_LIMIT)


def _conv_kernel(*refs, tm, tiles_per_seq, tile_offset, sample):
    if sample:
        (x_ref, g_ref, wb_ref, wc_ref, wx_ref, cw_ref, wo_ref, cin_ref, xs_ref, s0_ref, s1_ref,
         o_ref, st_ref, os_ref, us_ref, wb16_ref, wc16_ref, wx16_ref, wo16_ref,
         hn_ref, ubuf_ref, carry_ref) = refs
        _start_mixer(xs_ref, g_ref, os_ref, hn_ref.at[tm:hn_ref.shape[0]])
    else:
        (x_ref, g_ref, wb_ref, wc_ref, wx_ref, cw_ref, wo_ref, cin_ref,
         o_ref, st_ref, hn_ref, ubuf_ref, carry_ref) = refs
    i, j = pl.program_id(0), pl.program_id(1)
    _start_mixer(x_ref, g_ref, o_ref, hn_ref.at[0:tm])
    wb, wc, wx, wo = _mxu_weights((wb_ref, wc_ref, wx_ref, wo_ref),
                                  (wb16_ref, wc16_ref, wx16_ref, wo16_ref) if sample else None)
    cw = cw_ref[...]
    first = ((i + tile_offset) % tiles_per_seq) == 0
    tail = _entering(first, cin_ref, carry_ref, j)
    rs = _row_subblock(tm)
    bounds = _row_blocks_of(tm, hn_ref.shape[0])
    ups = []
    for r0, r1 in bounds:
        hn = hn_ref[r0:r1, :]
        ups.append((_dot(hn, wb), _dot(hn, wc), _dot(hn, wx)))
    for k, ((r0, r1), (b_gate, c_gate, xin)) in enumerate(zip(bounds, ups)):
        u = c_gate[0:rs] * xin[0:rs]
        (u2, u1), tail = _time_shifts(ubuf_ref.at[k], u, tail, SC_WIDTH)
        conv = u2 * cw[0:1, :] + u1 * cw[1:2, :] + u * cw[2:3, :]
        y = (b_gate[0:rs] * conv).astype(_BF16)
        if r1 - r0 > rs:
            u = c_gate[rs:] * xin[rs:]
            conv = s0_ref[...] * cw[0:1, :] + s1_ref[...] * cw[1:2, :] + u * cw[2:3, :]
            us_ref[...] = u
            y = jnp.concatenate([y, (b_gate[rs:] * conv).astype(_BF16)], axis=0)
        out = _dot(y, wo)
        o_ref[r0:r0 + rs, :] += out[0:rs]
        if r1 - r0 > rs:
            os_ref[...] += out[rs:]
    carry_ref[j] = tail
    st_ref[...] = tail


def _conv_mixer(x, layer, g, w_conv, w_b, w_c, w_x, w_out, cin, *, tm, seq_len, tile_offset,
                xs=None, state=None):
    m = x.shape[0]
    tn = MIX_CHUNK
    nj = D_MODEL // tn
    j = layer // 2
    head = (SC_WIDTH - 1) * SUBLANES
    rs = _row_subblock(tm)
    nt = 1 if xs is not None else m // tm - tile_offset
    sample = xs is not None
    ms = xs.shape[0] if sample else 0
    x_spec = _row_tile_spec(tm, nt, tile_offset)
    in_specs = [
        x_spec,
        pl.BlockSpec((None, 1, D_MODEL), lambda i, c: (layer, 0, 0)),
        _col_blocks(w_b[1], w_b[2], (D_MODEL, tn)),
        _col_blocks(w_c[1], w_c[2], (D_MODEL, tn)),
        _col_blocks(w_x[1], w_x[2], (D_MODEL, tn)),
        pl.BlockSpec((None, SC_WIDTH, tn), lambda i, c: (j, 0, c)),
        _row_blocks(w_out[1], (tn, D_MODEL)),
        _col_blocks(0, 0, (head, tn)),
    ]
    args = [x, g, w_b[0], w_c[0], w_x[0], w_conv, w_out[0], cin]
    out_shape = [jax.ShapeDtypeStruct((m, D_MODEL), _F32), jax.ShapeDtypeStruct((nt, head, D_MODEL), _F32)]
    out_specs = [x_spec, pl.BlockSpec((None, head, tn), lambda i, c: (i, 0, c))]
    scratch = [pltpu.VMEM((tm + ms, D_MODEL), _BF16), pltpu.VMEM((tm // rs, head + rs, tn), _F32),
               pltpu.VMEM((nj, head, tn), _F32)]
    if sample:
        xs_spec = pl.BlockSpec((ms, D_MODEL), lambda i, c: (0, 0))
        in_specs += [xs_spec, pl.BlockSpec((ms, tn), lambda i, c: (0, c)),
                     pl.BlockSpec((ms, tn), lambda i, c: (0, nj + c))]
        args += [xs, state, state]
        w16 = jax.ShapeDtypeStruct((1, D_MODEL, D_MODEL), _BF16)
        out_shape += [jax.ShapeDtypeStruct((ms, D_MODEL), _F32), jax.ShapeDtypeStruct((ms, D_MODEL), _F32),
                      w16, w16, w16, w16]
        out_specs += [xs_spec, pl.BlockSpec((ms, tn), lambda i, c: (0, c)),
                      _col_blocks(0, 0, (D_MODEL, tn)), _col_blocks(0, 0, (D_MODEL, tn)),
                      _col_blocks(0, 0, (D_MODEL, tn)), _row_blocks(0, (tn, D_MODEL))]
    kern = functools.partial(_conv_kernel, tm=tm, tiles_per_seq=seq_len // tm, tile_offset=tile_offset,
                             sample=sample)
    return pl.pallas_call(
        kern, grid=(nt, nj), in_specs=in_specs, out_specs=out_specs, out_shape=out_shape,
        scratch_shapes=scratch, compiler_params=_compiler_params(), input_output_aliases=_ALIAS_ROWS,
        name=f"conv_mixer_l{layer}_m{nt * tm}",
    )(*args)


def _lru_gates(u, wa, ba, wi, bi, sp_neg_lam):
    ub = u.astype(_BF16)
    r = _sigmoid(_dot(ub, wa) + ba)
    gate_i = _sigmoid(_dot(ub, wi) + bi)
    log_a = (-LRU_C * r) * sp_neg_lam
    a = jnp.exp(log_a)
    m2 = -jnp.tanh(log_a) * (a * a + 1.0)
    mult = jnp.where(m2 > 0.0, m2 * lax.rsqrt(m2), 0.0)
    return a, mult * gate_i * u


def _lru_kernel(*refs, tm, tiles_per_seq, tile_offset, sample):
    if sample:
        (x_ref, g_ref, wg_ref, wx_ref, cw_ref, cb_ref, wa_ref, ba_ref, wi_ref, bi_ref, lam_ref, wo_ref,
         cin_ref, hin_ref, xs_ref, s0_ref, s1_ref, s2_ref, h0_ref,
         o_ref, rgc_ref, hl_ref, os_ref, xxs_ref, hs_ref,
         wg16_ref, wx16_ref, wa16_ref, wi16_ref, wo16_ref,
         hn_ref, xbuf_ref, hbuf_ref, pbuf_ref, carry_ref, hcarry_ref) = refs
        _start_mixer(xs_ref, g_ref, os_ref, hn_ref.at[tm:hn_ref.shape[0]])
    else:
        (x_ref, g_ref, wg_ref, wx_ref, cw_ref, cb_ref, wa_ref, ba_ref, wi_ref, bi_ref, lam_ref, wo_ref,
         cin_ref, hin_ref, o_ref, rgc_ref, hl_ref,
         hn_ref, xbuf_ref, hbuf_ref, pbuf_ref, carry_ref, hcarry_ref) = refs
    i, j = pl.program_id(0), pl.program_id(1)
    _start_mixer(x_ref, g_ref, o_ref, hn_ref.at[0:tm])
    wg, wx, wo, wa, wi = _mxu_weights((wg_ref, wx_ref, wo_ref, wa_ref, wi_ref),
                                      (wg16_ref, wx16_ref, wo16_ref, wa16_ref, wi16_ref) if sample else None)
    cw, cb = cw_ref[...], cb_ref[...]
    gate_params = (wa, ba_ref[...], wi, bi_ref[...], _softplus(-lam_ref[...]))
    first = ((i + tile_offset) % tiles_per_seq) == 0
    tail = _entering(first, cin_ref, carry_ref, j)
    h_state = _entering(first, hin_ref, hcarry_ref, j)
    rs = _row_subblock(tm)
    bounds = _row_blocks_of(tm, hn_ref.shape[0])
    ups = []
    for r0, r1 in bounds:
        hn = hn_ref[r0:r1, :]
        ups.append((_dot(hn, wg), _dot(hn, wx)))
    for k, ((r0, r1), (gate_pre, xx_all)) in enumerate(zip(bounds, ups)):
        with_sample = r1 - r0 > rs
        gate = _gelu_tanh(gate_pre)
        xx = xx_all[0:rs]
        (x3, x2, x1), tail = _time_shifts(xbuf_ref.at[k], xx, tail, RG_CONV_WIDTH)
        u = (x3 * cw[0:1, :] + x2 * cw[1:2, :] + x1 * cw[2:3, :] + xx * cw[3:4, :]) + cb
        if with_sample:
            xxs = xx_all[rs:]
            us = (s0_ref[...] * cw[0:1, :] + s1_ref[...] * cw[1:2, :] + s2_ref[...] * cw[2:3, :]
                  + xxs * cw[3:4, :]) + cb
            xxs_ref[...] = xxs
            u = jnp.concatenate([u, us], axis=0)
        a, b = _lru_gates(u, *gate_params)
        hs, h_state = _interleaved_scan(a[0:rs], b[0:rs], h_state, hbuf_ref.at[k], pbuf_ref.at[k])
        if with_sample:
            h_new = b[rs:] + a[rs:] * h0_ref[...]
            hs_ref[...] = h_new
            hs = jnp.concatenate([hs, h_new], axis=0)
        out = _dot((gate * hs).astype(_BF16), wo)
        o_ref[r0:r0 + rs, :] += out[0:rs]
        if with_sample:
            os_ref[...] += out[rs:]
    carry_ref[j] = tail
    rgc_ref[...] = tail
    hcarry_ref[j] = h_state
    hl_ref[...] = h_state


def _lru_mixer(x, layer, g, conv_w, conv_b, b_a, b_i, lam, w_gate, w_x, w_a, w_i, w_out, cin, hin,
               *, tm, seq_len, tile_offset, xs=None, state=None, h0=None):
    m = x.shape[0]
    tn = MIX_CHUNK
    nj = D_MODEL // tn
    j = layer // 2
    head = (RG_CONV_WIDTH - 1) * SUBLANES
    rs = _row_subblock(tm)
    nt = 1 if xs is not None else m // tm - tile_offset
    sample = xs is not None
    ms = xs.shape[0] if sample else 0

    def gate_blocks(lead):
        return pl.BlockSpec((None, None, LRU_BW, LRU_BW), lambda i, c: (lead, c, 0, 0))

    def vec_blocks():
        return pl.BlockSpec((None, 1, tn), lambda i, c: (j, 0, c))

    x_spec = _row_tile_spec(tm, nt, tile_offset)
    in_specs = [
        x_spec,
        pl.BlockSpec((None, 1, D_MODEL), lambda i, c: (layer, 0, 0)),
        _col_blocks(w_gate[1], 0, (D_MODEL, tn)),
        _col_blocks(w_x[1], 0, (D_MODEL, tn)),
        pl.BlockSpec((None, RG_CONV_WIDTH, tn), lambda i, c: (j, 0, c)),
        vec_blocks(),
        gate_blocks(w_a[1]),
        vec_blocks(),
        gate_blocks(w_i[1]),
        vec_blocks(),
        vec_blocks(),
        _row_blocks(w_out[1], (tn, D_MODEL)),
        _col_blocks(0, 0, (head, tn)),
        _col_blocks(0, 0, (1, tn)),
    ]
    args = [x, g, w_gate[0], w_x[0], conv_w, conv_b, w_a[0], b_a, w_i[0], b_i, lam, w_out[0], cin, hin]
    out_shape = [jax.ShapeDtypeStruct((m, D_MODEL), _F32), jax.ShapeDtypeStruct((nt, head, D_MODEL), _F32),
                 jax.ShapeDtypeStruct((nt, 1, D_MODEL), _F32)]
    out_specs = [x_spec, pl.BlockSpec((None, head, tn), lambda i, c: (i, 0, c)),
                 pl.BlockSpec((None, 1, tn), lambda i, c: (i, 0, c))]
    scratch = [pltpu.VMEM((tm + ms, D_MODEL), _BF16), pltpu.VMEM((tm // rs, head + rs, tn), _F32),
               pltpu.VMEM((tm // rs, rs, tn), _F32), pltpu.VMEM((tm // rs, rs, tn), _F32),
               pltpu.VMEM((nj, head, tn), _F32), pltpu.VMEM((nj, 1, tn), _F32)]
    if sample:
        xs_spec = pl.BlockSpec((ms, D_MODEL), lambda i, c: (0, 0))
        col_spec = pl.BlockSpec((ms, tn), lambda i, c: (0, c))
        in_specs += [xs_spec, col_spec, pl.BlockSpec((ms, tn), lambda i, c: (0, nj + c)),
                     pl.BlockSpec((ms, tn), lambda i, c: (0, 2 * nj + c)), col_spec]
        args += [xs, state, state, state, h0]
        w16 = jax.ShapeDtypeStruct((1, D_MODEL, D_MODEL), _BF16)
        g16 = jax.ShapeDtypeStruct((1, nj, LRU_BW, LRU_BW), _BF16)
        row = jax.ShapeDtypeStruct((ms, D_MODEL), _F32)
        out_shape += [row, row, row, w16, w16, g16, g16, w16]
        out_specs += [xs_spec, col_spec, col_spec,
                      _col_blocks(0, 0, (D_MODEL, tn)), _col_blocks(0, 0, (D_MODEL, tn)),
                      gate_blocks(0), gate_blocks(0), _row_blocks(0, (tn, D_MODEL))]
    kern = functools.partial(_lru_kernel, tm=tm, tiles_per_seq=seq_len // tm, tile_offset=tile_offset,
                             sample=sample)
    return pl.pallas_call(
        kern, grid=(nt, nj), in_specs=in_specs, out_specs=out_specs, out_shape=out_shape,
        scratch_shapes=scratch, compiler_params=_compiler_params(), input_output_aliases=_ALIAS_ROWS,
        name=f"lru_mixer_l{layer}_m{nt * tm}",
    )(*args)


def _ffn_ple_kernel(*refs, n_ffn, tp, final, sample):
    if sample:
        (x_ref, gf_ref, wg_ref, wu_ref, wd_ref, gp_ref, pwg_ref, p_ref, pwp_ref, gl_ref, xs_ref, ps_ref,
         o_ref, os_ref, wg16_ref, wu16_ref, wd16_ref, pwg16_ref, pwp16_ref, hn_ref) = refs
    else:
        (x_ref, gf_ref, wg_ref, wu_ref, wd_ref, gp_ref, pwg_ref, p_ref, pwp_ref, gl_ref,
         o_ref, hn_ref) = refs
    c = pl.program_id(1)
    tm = x_ref.shape[0]
    rows = hn_ref.shape[0]
    bounds = _row_blocks_of(tm, rows)
    groups = [(x_ref, o_ref, hn_ref.at[0:tm])] + ([(xs_ref, os_ref, hn_ref.at[tm:rows])] if sample else [])
    for xg_ref, og_ref, hg_ref in groups:
        _start_mixer(xg_ref, gf_ref, og_ref, hg_ref)

    def add_rows(r0, r1, cols, upd):
        top = min(r1, tm)
        o_ref[r0:top, cols] += upd[0:top - r0]
        if r1 > tm:
            os_ref[:, cols] += upd[top - r0:]

    @pl.when(c < n_ffn)
    def _():
        wg, wu, wd = _mxu_weights((wg_ref, wu_ref, wd_ref),
                                  (wg16_ref, wu16_ref, wd16_ref) if sample else None)
        for r0, r1 in bounds:
            hn = hn_ref[r0:r1, :]
            gt = _dot(hn, wg)
            up = _dot(hn, wu)
            h = (gt * _sigmoid(gt)) * up
            add_rows(r0, r1, slice(None), _dot(h.astype(_BF16), wd))

    @pl.when(c == n_ffn)
    def _():
        for _, og_ref, hg_ref in groups:
            hg_ref[...] = _rmsnorm(og_ref[...], gp_ref[...]).astype(_BF16)

    @pl.when(c >= n_ffn)
    def _():
        cols = pl.ds(pl.multiple_of((c - n_ffn) * tp, tp), tp)
        wg, wp = _mxu_weights((pwg_ref, pwp_ref), (pwg16_ref, pwp16_ref) if sample else None)
        for r0, r1 in bounds:
            gate = _sigmoid(_dot(hn_ref[r0:r1, :], wg))
            top = min(r1, tm)
            p_rows = p_ref[r0:top, :].astype(_BF16)
            if r1 > tm:
                p_rows = jnp.concatenate([p_rows, ps_ref[...].astype(_BF16)], axis=0)
            add_rows(r0, r1, cols, gate * _dot(p_rows, wp))

    if final:
        @pl.when(c == pl.num_programs(1) - 1)
        def _():
            for _, og_ref, _ in groups:
                og_ref[...] = _rmsnorm(og_ref[...], gl_ref[...])


def _ffn_ple(x, p, layer, g_ffn, g_ple, g_final, w_gate, w_up, w_down, pw_gate, pw_proj,
             *, tm, tile_offset, xs=None, ps=None):
    m = x.shape[0]
    nt = 1 if xs is not None else m // tm - tile_offset
    sample = xs is not None
    ms = xs.shape[0] if sample else 0
    tf, tp =(FFN_CHUNK_F32, PLE_CHUNK_F32) if sample else (FFN_CHUNK, PLE_CHUNK)
    n_ffn, n_ple = D_FF // tf, D_MODEL // tp

    def ffn_c(c):
        return jnp.minimum(c, n_ffn - 1)

    def ple_c(c):
        return jnp.maximum(c - n_ffn, 0)

    def ffn_cols(lead):
        return pl.BlockSpec((None, D_MODEL, tf), lambda i, c: (lead, 0, ffn_c(c)))

    def ffn_rows(lead):
        return pl.BlockSpec((None, tf, D_MODEL), lambda i, c: (lead, ffn_c(c), 0))

    def ple_cols(lead, rows):
        return pl.BlockSpec((None, rows, tp), lambda i, c: (lead, 0, ple_c(c)))

    x_spec = _row_tile_spec(tm, nt, tile_offset)
    in_specs = [
        x_spec,
        pl.BlockSpec((None, 1, D_MODEL), lambda i, c: (layer, 0, 0)),
        ffn_cols(w_gate[1]), ffn_cols(w_up[1]), ffn_rows(w_down[1]),
        pl.BlockSpec((None, 1, D_MODEL), lambda i, c: (layer, 0, 0)),
        ple_cols(pw_gate[1], D_MODEL),
        pl.BlockSpec((None, tm, PLE_DIM), lambda i, c: (layer, i + tile_offset, 0)),
        ple_cols(pw_proj[1], PLE_DIM),
        pl.BlockSpec((1, D_MODEL), lambda i, c: (0, 0)),
    ]
    args = [x, g_ffn, w_gate[0], w_up[0], w_down[0], g_ple, pw_gate[0], p, pw_proj[0], g_final]
    out_specs = [x_spec]
    out_shape = [jax.ShapeDtypeStruct((m, D_MODEL), _F32)]
    scratch = [pltpu.VMEM((tm + ms, D_MODEL), _BF16)]
    if sample:
        xs_spec = pl.BlockSpec((ms, D_MODEL), lambda i, c: (0, 0))
        in_specs += [xs_spec, pl.BlockSpec((None, ms, PLE_DIM), lambda i, c: (layer, 0, 0))]
        args += [xs, ps]
        out_specs += [xs_spec, ffn_cols(0), ffn_cols(0), ffn_rows(0), ple_cols(0, D_MODEL), ple_cols(0, PLE_DIM)]
        out_shape += [jax.ShapeDtypeStruct((ms, D_MODEL), _F32),
                      jax.ShapeDtypeStruct((1, D_MODEL, D_FF), _BF16), jax.ShapeDtypeStruct((1, D_MODEL, D_FF), _BF16),
                      jax.ShapeDtypeStruct((1, D_FF, D_MODEL), _BF16),
                      jax.ShapeDtypeStruct((1, D_MODEL, D_MODEL), _BF16),
                      jax.ShapeDtypeStruct((1, PLE_DIM, D_MODEL), _BF16)]
    kern = functools.partial(_ffn_ple_kernel, n_ffn=n_ffn, tp=tp, final=(layer == DEPTH - 1), sample=sample)
    return pl.pallas_call(
        kern, grid=(nt, n_ffn + n_ple), in_specs=in_specs, out_specs=out_specs, out_shape=out_shape,
        scratch_shapes=scratch, compiler_params=_compiler_params(), input_output_aliases=_ALIAS_ROWS,
        name=f"ffn_ple_l{layer}_m{nt * tm}",
    )(*args)


def _trunk(x, p, xs, ps, conv_state, rgc_state, rgh_state, weights, *, tm, seq_len):
    (mix_norm, ffn_norm, ple_norm, final_norm, sc_w_in, sc_w_conv, sc_w_out,
     rg_w_x, rg_w_gate, rg_conv_w, rg_conv_b, rg_w_a, rg_b_a, rg_w_i, rg_b_i, rg_lambda, rg_w_out,
     ffn_w_gate, ffn_w_up, ffn_w_down, ple_w_gate, ple_w_proj) = weights
    ms = xs.shape[0]
    tiles_per_seq = seq_len // tm
    assert tiles_per_seq >= 2, "the head call's row tile must not end a sequence"
    last_tile = slice(tiles_per_seq - 2, None, tiles_per_seq)
    last_time = slice(SUBLANES - 1, None, SUBLANES)
    nj = D_MODEL // MIX_CHUNK
    zeros = functools.partial(jnp.zeros, dtype=_F32)
    geom = dict(tm=tm, seq_len=seq_len)
    conv_p, conv_s, rgc_p, rgc_s, rgh_p, rgh_s = [], [], [], [], [], []
    for layer in range(DEPTH):
        j = layer // 2
        if layer % 2 == 0:
            st = conv_state[j]
            x, st0, xs, u, *w16 = _conv_mixer(
                x, layer, mix_norm, sc_w_conv, (sc_w_in, j, 0), (sc_w_in, j, nj), (sc_w_in, j, 2 * nj),
                (sc_w_out, j), zeros((1, (SC_WIDTH - 1) * SUBLANES, D_MODEL)), tile_offset=0, xs=xs,
                state=st.reshape(ms, (SC_WIDTH - 1) * D_MODEL), **geom)
            wb, wc, wx, wo = w16
            x, st_r = _conv_mixer(x, layer, mix_norm, sc_w_conv, (wb, 0, 0), (wc, 0, 0), (wx, 0, 0), (wo, 0),
                                  st0, tile_offset=1, **geom)
            conv_p.append(st_r[last_tile, last_time])
            conv_s.append(jnp.concatenate([st[:, 1:], u[:, None, :]], axis=1))
        else:
            lru_v = (rg_conv_w, rg_conv_b, rg_b_a, rg_b_i, rg_lambda)
            st = rgc_state[j]
            x, st0, h0, xs, xx, h, *w16 = _lru_mixer(
                x, layer, mix_norm, *lru_v, (rg_w_gate, j), (rg_w_x, j), (rg_w_a, j), (rg_w_i, j), (rg_w_out, j),
                zeros((1, (RG_CONV_WIDTH - 1) * SUBLANES, D_MODEL)), zeros((1, 1, D_MODEL)), tile_offset=0,
                xs=xs, state=st.reshape(ms, (RG_CONV_WIDTH - 1) * D_MODEL), h0=rgh_state[j], **geom)
            x, st_r, h_r = _lru_mixer(x, layer, mix_norm, *lru_v, *[(w, 0) for w in w16], st0, h0,
                                      tile_offset=1, **geom)
            rgc_p.append(st_r[last_tile, last_time])
            rgh_p.append(h_r[last_tile, 0, :])
            rgc_s.append(jnp.concatenate([st[:, 1:], xx[:, None, :]], axis=1))
            rgh_s.append(h)
        norms = (ffn_norm, ple_norm, final_norm)
        x, xs, *w16 = _ffn_ple(x, p, layer, *norms, (ffn_w_gate, layer), (ffn_w_up, layer), (ffn_w_down, layer),
                               (ple_w_gate, layer), (ple_w_proj, layer), tm=tm, tile_offset=0, xs=xs, ps=ps)
        (x,) = _ffn_ple(x, p, layer, *norms, *[(w, 0) for w in w16], tm=tm, tile_offset=1)
    return (x, xs, jnp.stack(conv_p), jnp.stack(conv_s), jnp.stack(rgc_p), jnp.stack(rgc_s),
            jnp.stack(rgh_p), jnp.stack(rgh_s))


def kernel(x_prompt, x_sample, p_prompt, p_sample, state_conv, state_rg_conv, state_rg_h, mix_norm, ffn_norm, ple_norm, final_norm, sc_w_in, sc_w_conv, sc_w_out, rg_w_x, rg_w_gate, rg_conv_w, rg_conv_b, rg_w_a, rg_b_a, rg_w_i, rg_b_i, rg_lambda, rg_w_out, ffn_w_gate, ffn_w_up, ffn_w_down, ple_w_gate, ple_w_proj):
    bsz, seq, _ = x_prompt.shape
    dec = x_sample.shape[0]

    def rows(v):
        return v.reshape(v.shape[0], 1, v.shape[1])

    weights = (rows(mix_norm), rows(ffn_norm), rows(ple_norm), final_norm.reshape(1, D_MODEL),
               sc_w_in, sc_w_conv, sc_w_out, rg_w_x, rg_w_gate, rg_conv_w, rows(rg_conv_b),
               rg_w_a, rows(rg_b_a), rg_w_i, rows(rg_b_i), rows(rg_lambda),
               rg_w_out, ffn_w_gate, ffn_w_up, ffn_w_down, ple_w_gate, ple_w_proj)
    rs = _row_subblock(ROW_TILE)
    y_p, y_s, conv_p, conv_s, rgc_p, rgc_s, rgh_p, rgh_s = _trunk(
        _interleave_time(x_prompt, rs).reshape(bsz * seq, D_MODEL),
        _interleave_time(p_prompt, rs).reshape(DEPTH, bsz * seq, PLE_DIM),
        x_sample.reshape(dec, D_MODEL), p_sample.reshape(DEPTH, dec, PLE_DIM),
        state_conv, state_rg_conv, state_rg_h, weights, tm=ROW_TILE, seq_len=seq)
    y_p = _deinterleave_time(y_p.reshape(bsz, seq, D_MODEL), rs)
    return (y_p, y_s.reshape(dec, 1, D_MODEL), conv_p, conv_s, rgc_p, rgc_s, rgh_p, rgh_s)
```

```python
import functools

import jax
import jax.numpy as jnp
from jax import lax
from jax.experimental import pallas as pl
from jax.experimental.pallas import tpu as pltpu

D_MODEL = 2048
DEPTH = 4
PLE_DIM = 256
SC_WIDTH = 3
RG_CONV_WIDTH = 4
LRU_BW = 256
LRU_C = 8.0
D_FF = 5632
EPS = 1e-6

SUBLANES = 8
MXU_DIM = 256
ROW_TILE = 1024
ROW_SUBBLOCKS = 2
MIX_CHUNK = LRU_BW
FFN_CHUNK = 512
PLE_CHUNK = 512
FFN_CHUNK_F32 = 256
PLE_CHUNK_F32 = 256
VMEM_LIMIT = 60 * 1024 * 1024

_F32 = jnp.float32
_BF16 = jnp.bfloat16


def _dot(a, b):
    return jnp.dot(a, b, preferred_element_type=_F32)


def _w(ref):
    w = ref[...]
    return w if w.dtype == _BF16 else w.astype(_BF16)


def _mxu_weights(w_refs, w16_refs=None):
    if w16_refs is None:
        return [_w(r) for r in w_refs]
    for src, dst in zip(w_refs, w16_refs):
        dst[...] = _w(src)
    return [dst[...] for dst in w16_refs]


def _rmsnorm(x, g):
    y = x * lax.rsqrt(jnp.mean(x * x, axis=-1, keepdims=True) + EPS)
    return y * g


def _sigmoid(x):
    return jax.nn.sigmoid(x)


def _gelu_tanh(x):
    c2 = 2.0 * 0.7978845608028654
    return x * _sigmoid(x * (c2 + (c2 * 0.044715) * (x * x)))


def _softplus(x):
    return jnp.maximum(x, 0.0) + jnp.log1p(jnp.exp(-jnp.abs(x)))


def _interleave_time(x, rs):
    *lead, t, c = x.shape
    y = x.reshape(*lead, t // rs, SUBLANES, rs // SUBLANES, c)
    return jnp.swapaxes(y, -2, -3).reshape(x.shape)


def _deinterleave_time(x, rs):
    *lead, t, c = x.shape
    y = x.reshape(*lead, t // rs, rs // SUBLANES, SUBLANES, c)
    return jnp.swapaxes(y, -2, -3).reshape(x.shape)


def _time_shifts(buf_ref, vals, prev_tail, width):
    rs = vals.shape[0]
    head = (width - 1) * SUBLANES
    tail = vals[rs - head:rs, :]
    sub = lax.broadcasted_iota(jnp.int32, (SUBLANES, vals.shape[1]), 0)
    for v in range(width - 1):
        grp = slice(v * SUBLANES, (v + 1) * SUBLANES)
        mixed = jnp.where(sub == SUBLANES - 1, prev_tail[grp, :], tail[grp, :])
        buf_ref[grp, :] = pltpu.roll(mixed, 1, 0)
    buf_ref[head:head + rs, :] = vals
    shifted = [buf_ref[head - k * SUBLANES:head - k * SUBLANES + rs, :] for k in range(width - 1, 0, -1)]
    return shifted, tail


def _interleaved_scan(a, b, h_in, hbuf_ref, pbuf_ref):
    rs, tn = a.shape
    groups = rs // SUBLANES
    h = b[0:SUBLANES, :]
    p = a[0:SUBLANES, :]
    hbuf_ref[0:SUBLANES, :] = h
    pbuf_ref[0:SUBLANES, :] = p
    for q in range(1, groups):
        grp = slice(q * SUBLANES, (q + 1) * SUBLANES)
        h = a[grp, :] * h + b[grp, :]
        p = a[grp, :] * p
        hbuf_ref[grp, :] = h
        pbuf_ref[grp, :] = p
    sub = lax.broadcasted_iota(jnp.int32, (SUBLANES, tn), 0)
    carry_s = h_in
    carry = jnp.broadcast_to(h_in, (SUBLANES, tn))
    for s in range(1, SUBLANES):
        carry_s = p[s - 1:s, :] * carry_s + h[s - 1:s, :]
        carry = jnp.where(sub == s, carry_s, carry)
    h_out = p[SUBLANES - 1:SUBLANES, :] * carry_s + h[SUBLANES - 1:SUBLANES, :]
    hs = hbuf_ref[...] + pbuf_ref[...] * jnp.concatenate([carry] * groups, axis=0)
    return hs, h_out


def _row_subblock(tm):
    return tm // ROW_SUBBLOCKS if tm >= ROW_SUBBLOCKS * MXU_DIM else tm


def _row_blocks_of(tm, rows):
    rs = _row_subblock(tm)
    bounds = [(r0, r0 + rs) for r0 in range(0, tm, rs)]
    bounds[-1] = (bounds[-1][0], rows)
    return bounds


def _start_mixer(x_ref, g_ref, o_ref, hn_ref):
    @pl.when(pl.program_id(1) == 0)
    def _():
        x = x_ref[...]
        hn_ref[...] = _rmsnorm(x, g_ref[...]).astype(_BF16)
        o_ref[...] = x


def _entering(first, cin_ref, carry_ref, j):
    held = jnp.where(pl.program_id(0) == 0, cin_ref[...], carry_ref[j])
    return jnp.where(first, 0.0, held)


def _col_blocks(lead, offset, shape):
    return pl.BlockSpec((None,) + shape, lambda i, c: (lead, 0, offset + c))


def _row_blocks(lead, shape):
    return pl.BlockSpec((None,) + shape, lambda i, c: (lead, c, 0))


def _row_tile_spec(tm, nt, tile_offset):
    mode = dict(pipeline_mode=pl.Buffered(1)) if nt == 1 else {}
    return pl.BlockSpec((tm, D_MODEL), lambda i, c: (i + tile_offset, 0), **mode)


_ALIAS_ROWS = {0: 0}


def _compiler_params():
    return pltpu.CompilerParams(dimension_semantics=("arbitrary", "arbitrary"), vmem_limit_bytes=VMEM_LIMIT)


def _conv_kernel(*refs, tm, tiles_per_seq, tile_offset, sample):
    if sample:
        (x_ref, g_ref, wb_ref, wc_ref, wx_ref, cw_ref, wo_ref, cin_ref, xs_ref, s0_ref, s1_ref,
         o_ref, st_ref, os_ref, us_ref, wb16_ref, wc16_ref, wx16_ref, wo16_ref,
         hn_ref, ubuf_ref, carry_ref) = refs
        _start_mixer(xs_ref, g_ref, os_ref, hn_ref.at[tm:hn_ref.shape[0]])
    else:
        (x_ref, g_ref, wb_ref, wc_ref, wx_ref, cw_ref, wo_ref, cin_ref,
         o_ref, st_ref, hn_ref, ubuf_ref, carry_ref) = refs
    i, j = pl.program_id(0), pl.program_id(1)
    _start_mixer(x_ref, g_ref, o_ref, hn_ref.at[0:tm])
    wb, wc, wx, wo = _mxu_weights((wb_ref, wc_ref, wx_ref, wo_ref),
                                  (wb16_ref, wc16_ref, wx16_ref, wo16_ref) if sample else None)
    cw = cw_ref[...]
    first = ((i + tile_offset) % tiles_per_seq) == 0
    tail = _entering(first, cin_ref, carry_ref, j)
    rs = _row_subblock(tm)
    bounds = _row_blocks_of(tm, hn_ref.shape[0])
    ups = []
    for r0, r1 in bounds:
        hn = hn_ref[r0:r1, :]
        ups.append((_dot(hn, wb), _dot(hn, wc), _dot(hn, wx)))
    for k, ((r0, r1), (b_gate, c_gate, xin)) in enumerate(zip(bounds, ups)):
        u = c_gate[0:rs] * xin[0:rs]
        (u2, u1), tail = _time_shifts(ubuf_ref.at[k], u, tail, SC_WIDTH)
        conv = u2 * cw[0:1, :] + u1 * cw[1:2, :] + u * cw[2:3, :]
        y = (b_gate[0:rs] * conv).astype(_BF16)
        if r1 - r0 > rs:
            u = c_gate[rs:] * xin[rs:]
            conv = s0_ref[...] * cw[0:1, :] + s1_ref[...] * cw[1:2, :] + u * cw[2:3, :]
            us_ref[...] = u
            y = jnp.concatenate([y, (b_gate[rs:] * conv).astype(_BF16)], axis=0)
        out = _dot(y, wo)
        o_ref[r0:r0 + rs, :] += out[0:rs]
        if r1 - r0 > rs:
            os_ref[...] += out[rs:]
    carry_ref[j] = tail
    st_ref[...] = tail


def _conv_mixer(x, layer, g, w_conv, w_b, w_c, w_x, w_out, cin, *, tm, seq_len, tile_offset,
                xs=None, state=None):
    m = x.shape[0]
    tn = MIX_CHUNK
    nj = D_MODEL // tn
    j = layer // 2
    head = (SC_WIDTH - 1) * SUBLANES
    rs = _row_subblock(tm)
    nt = 1 if xs is not None else m // tm - tile_offset
    sample = xs is not None
    ms = xs.shape[0] if sample else 0
    x_spec = _row_tile_spec(tm, nt, tile_offset)
    in_specs = [
        x_spec,
        pl.BlockSpec((None, 1, D_MODEL), lambda i, c: (layer, 0, 0)),
        _col_blocks(w_b[1], w_b[2], (D_MODEL, tn)),
        _col_blocks(w_c[1], w_c[2], (D_MODEL, tn)),
        _col_blocks(w_x[1], w_x[2], (D_MODEL, tn)),
        pl.BlockSpec((None, SC_WIDTH, tn), lambda i, c: (j, 0, c)),
        _row_blocks(w_out[1], (tn, D_MODEL)),
        _col_blocks(0, 0, (head, tn)),
    ]
    args = [x, g, w_b[0], w_c[0], w_x[0], w_conv, w_out[0], cin]
    out_shape = [jax.ShapeDtypeStruct((m, D_MODEL), _F32), jax.ShapeDtypeStruct((nt, head, D_MODEL), _F32)]
    out_specs = [x_spec, pl.BlockSpec((None, head, tn), lambda i, c: (i, 0, c))]
    scratch = [pltpu.VMEM((tm + ms, D_MODEL), _BF16), pltpu.VMEM((tm // rs, head + rs, tn), _F32),
               pltpu.VMEM((nj, head, tn), _F32)]
    if sample:
        xs_spec = pl.BlockSpec((ms, D_MODEL), lambda i, c: (0, 0))
        in_specs += [xs_spec, pl.BlockSpec((ms, tn), lambda i, c: (0, c)),
                     pl.BlockSpec((ms, tn), lambda i, c: (0, nj + c))]
        args += [xs, state, state]
        w16 = jax.ShapeDtypeStruct((1, D_MODEL, D_MODEL), _BF16)
        out_shape += [jax.ShapeDtypeStruct((ms, D_MODEL), _F32), jax.ShapeDtypeStruct((ms, D_MODEL), _F32),
                      w16, w16, w16, w16]
        out_specs += [xs_spec, pl.BlockSpec((ms, tn), lambda i, c: (0, c)),
                      _col_blocks(0, 0, (D_MODEL, tn)), _col_blocks(0, 0, (D_MODEL, tn)),
                      _col_blocks(0, 0, (D_MODEL, tn)), _row_blocks(0, (tn, D_MODEL))]
    kern = functools.partial(_conv_kernel, tm=tm, tiles_per_seq=seq_len // tm, tile_offset=tile_offset,
                             sample=sample)
    return pl.pallas_call(
        kern, grid=(nt, nj), in_specs=in_specs, out_specs=out_specs, out_shape=out_shape,
        scratch_shapes=scratch, compiler_params=_compiler_params(), input_output_aliases=_ALIAS_ROWS,
        name=f"conv_mixer_l{layer}_m{nt * tm}",
    )(*args)


def _lru_gates(u, wa, ba, wi, bi, sp_neg_lam):
    ub = u.astype(_BF16)
    r = _sigmoid(_dot(ub, wa) + ba)
    gate_i = _sigmoid(_dot(ub, wi) + bi)
    log_a = (-LRU_C * r) * sp_neg_lam
    a = jnp.exp(log_a)
    m2 = -jnp.tanh(log_a) * (a * a + 1.0)
    mult = jnp.where(m2 > 0.0, m2 * lax.rsqrt(m2), 0.0)
    return a, mult * gate_i * u


def _lru_kernel(*refs, tm, tiles_per_seq, tile_offset, sample):
    if sample:
        (x_ref, g_ref, wg_ref, wx_ref, cw_ref, cb_ref, wa_ref, ba_ref, wi_ref, bi_ref, lam_ref, wo_ref,
         cin_ref, hin_ref, xs_ref, s0_ref, s1_ref, s2_ref, h0_ref,
         o_ref, rgc_ref, hl_ref, os_ref, xxs_ref, hs_ref,
         wg16_ref, wx16_ref, wa16_ref, wi16_ref, wo16_ref,
         hn_ref, xbuf_ref, hbuf_ref, pbuf_ref, carry_ref, hcarry_ref) = refs
        _start_mixer(xs_ref, g_ref, os_ref, hn_ref.at[tm:hn_ref.shape[0]])
    else:
        (x_ref, g_ref, wg_ref, wx_ref, cw_ref, cb_ref, wa_ref, ba_ref, wi_ref, bi_ref, lam_ref, wo_ref,
         cin_ref, hin_ref, o_ref, rgc_ref, hl_ref,
         hn_ref, xbuf_ref, hbuf_ref, pbuf_ref, carry_ref, hcarry_ref) = refs
    i, j = pl.program_id(0), pl.program_id(1)
    _start_mixer(x_ref, g_ref, o_ref, hn_ref.at[0:tm])
    wg, wx, wo, wa, wi = _mxu_weights((wg_ref, wx_ref, wo_ref, wa_ref, wi_ref),
                                      (wg16_ref, wx16_ref, wo16_ref, wa16_ref, wi16_ref) if sample else None)
    cw, cb = cw_ref[...], cb_ref[...]
    gate_params = (wa, ba_ref[...], wi, bi_ref[...], _softplus(-lam_ref[...]))
    first = ((i + tile_offset) % tiles_per_seq) == 0
    tail = _entering(first, cin_ref, carry_ref, j)
    h_state = _entering(first, hin_ref, hcarry_ref, j)
    rs = _row_subblock(tm)
    bounds = _row_blocks_of(tm, hn_ref.shape[0])
    ups = []
    for r0, r1 in bounds:
        hn = hn_ref[r0:r1, :]
        ups.append((_dot(hn, wg), _dot(hn, wx)))
    for k, ((r0, r1), (gate_pre, xx_all)) in enumerate(zip(bounds, ups)):
        with_sample = r1 - r0 > rs
        gate = _gelu_tanh(gate_pre)
        xx = xx_all[0:rs]
        (x3, x2, x1), tail = _time_shifts(xbuf_ref.at[k], xx, tail, RG_CONV_WIDTH)
        u = (x3 * cw[0:1, :] + x2 * cw[1:2, :] + x1 * cw[2:3, :] + xx * cw[3:4, :]) + cb
        if with_sample:
            xxs = xx_all[rs:]
            us = (s0_ref[...] * cw[0:1, :] + s1_ref[...] * cw[1:2, :] + s2_ref[...] * cw[2:3, :]
                  + xxs * cw[3:4, :]) + cb
            xxs_ref[...] = xxs
            u = jnp.concatenate([u, us], axis=0)
        a, b = _lru_gates(u, *gate_params)
        hs, h_state = _interleaved_scan(a[0:rs], b[0:rs], h_state, hbuf_ref.at[k], pbuf_ref.at[k])
        if with_sample:
            h_new = b[rs:] + a[rs:] * h0_ref[...]
            hs_ref[...] = h_new
            hs = jnp.concatenate([hs, h_new], axis=0)
        out = _dot((gate * hs).astype(_BF16), wo)
        o_ref[r0:r0 + rs, :] += out[0:rs]
        if with_sample:
            os_ref[...] += out[rs:]
    carry_ref[j] = tail
    rgc_ref[...] = tail
    hcarry_ref[j] = h_state
    hl_ref[...] = h_state


def _lru_mixer(x, layer, g, conv_w, conv_b, b_a, b_i, lam, w_gate, w_x, w_a, w_i, w_out, cin, hin,
               *, tm, seq_len, tile_offset, xs=None, state=None, h0=None):
    m = x.shape[0]
    tn = MIX_CHUNK
    nj = D_MODEL // tn
    j = layer // 2
    head = (RG_CONV_WIDTH - 1) * SUBLANES
    rs = _row_subblock(tm)
    nt = 1 if xs is not None else m // tm - tile_offset
    sample = xs is not None
    ms = xs.shape[0] if sample else 0

    def gate_blocks(lead):
        return pl.BlockSpec((None, None, LRU_BW, LRU_BW), lambda i, c: (lead, c, 0, 0))

    def vec_blocks():
        return pl.BlockSpec((None, 1, tn), lambda i, c: (j, 0, c))

    x_spec = _row_tile_spec(tm, nt, tile_offset)
    in_specs = [
        x_spec,
        pl.BlockSpec((None, 1, D_MODEL), lambda i, c: (layer, 0, 0)),
        _col_blocks(w_gate[1], 0, (D_MODEL, tn)),
        _col_blocks(w_x[1], 0, (D_MODEL, tn)),
        pl.BlockSpec((None, RG_CONV_WIDTH, tn), lambda i, c: (j, 0, c)),
        vec_blocks(),
        gate_blocks(w_a[1]),
        vec_blocks(),
        gate_blocks(w_i[1]),
        vec_blocks(),
        vec_blocks(),
        _row_blocks(w_out[1], (tn, D_MODEL)),
        _col_blocks(0, 0, (head, tn)),
        _col_blocks(0, 0, (1, tn)),
    ]
    args = [x, g, w_gate[0], w_x[0], conv_w, conv_b, w_a[0], b_a, w_i[0], b_i, lam, w_out[0], cin, hin]
    out_shape = [jax.ShapeDtypeStruct((m, D_MODEL), _F32), jax.ShapeDtypeStruct((nt, head, D_MODEL), _F32),
                 jax.ShapeDtypeStruct((nt, 1, D_MODEL), _F32)]
    out_specs = [x_spec, pl.BlockSpec((None, head, tn), lambda i, c: (i, 0, c)),
                 pl.BlockSpec((None, 1, tn), lambda i, c: (i, 0, c))]
    scratch = [pltpu.VMEM((tm + ms, D_MODEL), _BF16), pltpu.VMEM((tm // rs, head + rs, tn), _F32),
               pltpu.VMEM((tm // rs, rs, tn), _F32), pltpu.VMEM((tm // rs, rs, tn), _F32),
               pltpu.VMEM((nj, head, tn), _F32), pltpu.VMEM((nj, 1, tn), _F32)]
    if sample:
        xs_spec = pl.BlockSpec((ms, D_MODEL), lambda i, c: (0, 0))
        col_spec = pl.BlockSpec((ms, tn), lambda i, c: (0, c))
        in_specs += [xs_spec, col_spec, pl.BlockSpec((ms, tn), lambda i, c: (0, nj + c)),
                     pl.BlockSpec((ms, tn), lambda i, c: (0, 2 * nj + c)), col_spec]
        args += [xs, state, state, state, h0]
        w16 = jax.ShapeDtypeStruct((1, D_MODEL, D_MODEL), _BF16)
        g16 = jax.ShapeDtypeStruct((1, nj, LRU_BW, LRU_BW), _BF16)
        row = jax.ShapeDtypeStruct((ms, D_MODEL), _F32)
        out_shape += [row, row, row, w16, w16, g16, g16, w16]
        out_specs += [xs_spec, col_spec, col_spec,
                      _col_blocks(0, 0, (D_MODEL, tn)), _col_blocks(0, 0, (D_MODEL, tn)),
                      gate_blocks(0), gate_blocks(0), _row_blocks(0, (tn, D_MODEL))]
    kern = functools.partial(_lru_kernel, tm=tm, tiles_per_seq=seq_len // tm, tile_offset=tile_offset,
                             sample=sample)
    return pl.pallas_call(
        kern, grid=(nt, nj), in_specs=in_specs, out_specs=out_specs, out_shape=out_shape,
        scratch_shapes=scratch, compiler_params=_compiler_params(), input_output_aliases=_ALIAS_ROWS,
        name=f"lru_mixer_l{layer}_m{nt * tm}",
    )(*args)


def _ffn_ple_kernel(*refs, n_ffn, tp, final, sample):
    if sample:
        (x_ref, gf_ref, wg_ref, wu_ref, wd_ref, gp_ref, pwg_ref, p_ref, pwp_ref, gl_ref, xs_ref, ps_ref,
         o_ref, os_ref, wg16_ref, wu16_ref, wd16_ref, pwg16_ref, pwp16_ref, hn_ref) = refs
    else:
        (x_ref, gf_ref, wg_ref, wu_ref, wd_ref, gp_ref, pwg_ref, p_ref, pwp_ref, gl_ref,
         o_ref, hn_ref) = refs
    c = pl.program_id(1)
    tm = x_ref.shape[0]
    rows = hn_ref.shape[0]
    bounds = _row_blocks_of(tm, rows)

    def segments(r0, r1):
        top = min(r1, tm)
        segs = [(x_ref, o_ref, slice(r0, top), slice(r0, top))]
        if r1 > tm:
            segs.append((xs_ref, os_ref, slice(0, r1 - tm), slice(tm, r1)))
        return segs

    def normalise(r0, r1, g_ref, from_output):
        for in_ref, out_ref, there, here in segments(r0, r1):
            src = out_ref if from_output else in_ref
            hn_ref[here, :] = _rmsnorm(src[there, :], g_ref[...]).astype(_BF16)

    def add_rows(r0, r1, cols, upd, onto_input):
        off = 0
        for in_ref, out_ref, there, _ in segments(r0, r1):
            n = there.stop - there.start
            if onto_input:
                out_ref[there, cols] = in_ref[there, cols] + upd[off:off + n]
            else:
                out_ref[there, cols] += upd[off:off + n]
            off += n

    def halves(r0, r1):
        mid = (r0 + min(r1, tm)) // 2
        return (r0, mid), (mid, r1)

    def run_blocks(first, g_ref, from_output, block_fn):
        if first:
            normalise(*bounds[0], g_ref, from_output)
        for k, (r0, r1) in enumerate(bounds):
            todo = list(halves(*bounds[k + 1])) if first and k + 1 < len(bounds) else []

            def between():
                if todo:
                    normalise(*todo.pop(0), g_ref, from_output)

            block_fn(r0, r1, between)
            while todo:
                between()

    def ffn_step(first):
        wg, wu, wd = _mxu_weights((wg_ref, wu_ref, wd_ref),
                                  (wg16_ref, wu16_ref, wd16_ref) if sample else None)

        def block(r0, r1, between):
            hn = hn_ref[r0:r1, :]
            gt = _dot(hn, wg)
            between()
            up = _dot(hn, wu)
            between()
            h = (gt * _sigmoid(gt)) * up
            add_rows(r0, r1, slice(None), _dot(h.astype(_BF16), wd), onto_input=first)

        run_blocks(first, gf_ref, False, block)

    def ple_step(first):
        cols = pl.ds(pl.multiple_of((c - n_ffn) * tp, tp), tp)
        wg, wp = _mxu_weights((pwg_ref, pwp_ref), (pwg16_ref, pwp16_ref) if sample else None)

        def block(r0, r1, between):
            gate = _sigmoid(_dot(hn_ref[r0:r1, :], wg))
            between()
            top = min(r1, tm)
            p_rows = p_ref[r0:top, :].astype(_BF16)
            if r1 > tm:
                p_rows = jnp.concatenate([p_rows, ps_ref[...].astype(_BF16)], axis=0)
            add_rows(r0, r1, cols, gate * _dot(p_rows, wp), onto_input=False)

        run_blocks(first, gp_ref, True, block)

    pl.when(c == 0)(functools.partial(ffn_step, True))
    pl.when(jnp.logical_and(c > 0, c < n_ffn))(functools.partial(ffn_step, False))
    pl.when(c == n_ffn)(functools.partial(ple_step, True))
    pl.when(c > n_ffn)(functools.partial(ple_step, False))

    if final:
        @pl.when(c == pl.num_programs(1) - 1)
        def _():
            for _, out_ref, there, _ in segments(0, rows):
                out_ref[there, :] = _rmsnorm(out_ref[there, :], gl_ref[...])


def _ffn_ple(x, p, layer, g_ffn, g_ple, g_final, w_gate, w_up, w_down, pw_gate, pw_proj,
             *, tm, tile_offset, xs=None, ps=None):
    m = x.shape[0]
    nt = 1 if xs is not None else m // tm - tile_offset
    sample = xs is not None
    ms = xs.shape[0] if sample else 0
    tf, tp =(FFN_CHUNK_F32, PLE_CHUNK_F32) if sample else (FFN_CHUNK, PLE_CHUNK)
    n_ffn, n_ple = D_FF // tf, D_MODEL // tp

    def ffn_c(c):
        return jnp.minimum(c, n_ffn - 1)

    def ple_c(c):
        return jnp.maximum(c - n_ffn, 0)

    def ffn_cols(lead):
        return pl.BlockSpec((None, D_MODEL, tf), lambda i, c: (lead, 0, ffn_c(c)))

    def ffn_rows(lead):
        return pl.BlockSpec((None, tf, D_MODEL), lambda i, c: (lead, ffn_c(c), 0))

    def ple_cols(lead, rows):
        return pl.BlockSpec((None, rows, tp), lambda i, c: (lead, 0, ple_c(c)))

    x_spec = _row_tile_spec(tm, nt, tile_offset)
    in_specs = [
        x_spec,
        pl.BlockSpec((None, 1, D_MODEL), lambda i, c: (layer, 0, 0)),
        ffn_cols(w_gate[1]), ffn_cols(w_up[1]), ffn_rows(w_down[1]),
        pl.BlockSpec((None, 1, D_MODEL), lambda i, c: (layer, 0, 0)),
        ple_cols(pw_gate[1], D_MODEL),
        pl.BlockSpec((None, tm, PLE_DIM), lambda i, c: (layer, i + tile_offset, 0)),
        ple_cols(pw_proj[1], PLE_DIM),
        pl.BlockSpec((1, D_MODEL), lambda i, c: (0, 0)),
    ]
    args = [x, g_ffn, w_gate[0], w_up[0], w_down[0], g_ple, pw_gate[0], p, pw_proj[0], g_final]
    out_specs = [x_spec]
    out_shape = [jax.ShapeDtypeStruct((m, D_MODEL), _F32)]
    scratch = [pltpu.VMEM((tm + ms, D_MODEL), _BF16)]
    if sample:
        xs_spec = pl.BlockSpec((ms, D_MODEL), lambda i, c: (0, 0))
        in_specs += [xs_spec, pl.BlockSpec((None, ms, PLE_DIM), lambda i, c: (layer, 0, 0))]
        args += [xs, ps]
        out_specs += [xs_spec, ffn_cols(0), ffn_cols(0), ffn_rows(0), ple_cols(0, D_MODEL), ple_cols(0, PLE_DIM)]
        out_shape += [jax.ShapeDtypeStruct((ms, D_MODEL), _F32),
                      jax.ShapeDtypeStruct((1, D_MODEL, D_FF), _BF16), jax.ShapeDtypeStruct((1, D_MODEL, D_FF), _BF16),
                      jax.ShapeDtypeStruct((1, D_FF, D_MODEL), _BF16),
                      jax.ShapeDtypeStruct((1, D_MODEL, D_MODEL), _BF16),
                      jax.ShapeDtypeStruct((1, PLE_DIM, D_MODEL), _BF16)]
    kern = functools.partial(_ffn_ple_kernel, n_ffn=n_ffn, tp=tp, final=(layer == DEPTH - 1), sample=sample)
    return pl.pallas_call(
        kern, grid=(nt, n_ffn + n_ple), in_specs=in_specs, out_specs=out_specs, out_shape=out_shape,
        scratch_shapes=scratch, compiler_params=_compiler_params(), input_output_aliases=_ALIAS_ROWS,
        name=f"ffn_ple_l{layer}_m{nt * tm}",
    )(*args)


def _trunk(x, p, xs, ps, conv_state, rgc_state, rgh_state, weights, *, tm, seq_len):
    (mix_norm, ffn_norm, ple_norm, final_norm, sc_w_in, sc_w_conv, sc_w_out,
     rg_w_x, rg_w_gate, rg_conv_w, rg_conv_b, rg_w_a, rg_b_a, rg_w_i, rg_b_i, rg_lambda, rg_w_out,
     ffn_w_gate, ffn_w_up, ffn_w_down, ple_w_gate, ple_w_proj) = weights
    ms = xs.shape[0]
    tiles_per_seq = seq_len // tm
    assert tiles_per_seq >= 2, "the head call's row tile must not end a sequence"
    last_tile = slice(tiles_per_seq - 2, None, tiles_per_seq)
    last_time = slice(SUBLANES - 1, None, SUBLANES)
    nj = D_MODEL // MIX_CHUNK
    zeros = functools.partial(jnp.zeros, dtype=_F32)
    geom = dict(tm=tm, seq_len=seq_len)
    conv_p, conv_s, rgc_p, rgc_s, rgh_p, rgh_s = [], [], [], [], [], []
    for layer in range(DEPTH):
        j = layer // 2
        if layer % 2 == 0:
            st = conv_state[j]
            x, st0, xs, u, *w16 = _conv_mixer(
                x, layer, mix_norm, sc_w_conv, (sc_w_in, j, 0), (sc_w_in, j, nj), (sc_w_in, j, 2 * nj),
                (sc_w_out, j), zeros((1, (SC_WIDTH - 1) * SUBLANES, D_MODEL)), tile_offset=0, xs=xs,
                state=st.reshape(ms, (SC_WIDTH - 1) * D_MODEL), **geom)
            wb, wc, wx, wo = w16
            x, st_r = _conv_mixer(x, layer, mix_norm, sc_w_conv, (wb, 0, 0), (wc, 0, 0), (wx, 0, 0), (wo, 0),
                                  st0, tile_offset=1, **geom)
            conv_p.append(st_r[last_tile, last_time])
            conv_s.append(jnp.concatenate([st[:, 1:], u[:, None, :]], axis=1))
        else:
            lru_v = (rg_conv_w, rg_conv_b, rg_b_a, rg_b_i, rg_lambda)
            st = rgc_state[j]
            x, st0, h0, xs, xx, h, *w16 = _lru_mixer(
                x, layer, mix_norm, *lru_v, (rg_w_gate, j), (rg_w_x, j), (rg_w_a, j), (rg_w_i, j), (rg_w_out, j),
                zeros((1, (RG_CONV_WIDTH - 1) * SUBLANES, D_MODEL)), zeros((1, 1, D_MODEL)), tile_offset=0,
                xs=xs, state=st.reshape(ms, (RG_CONV_WIDTH - 1) * D_MODEL), h0=rgh_state[j], **geom)
            x, st_r, h_r = _lru_mixer(x, layer, mix_norm, *lru_v, *[(w, 0) for w in w16], st0, h0,
                                      tile_offset=1, **geom)
            rgc_p.append(st_r[last_tile, last_time])
            rgh_p.append(h_r[last_tile, 0, :])
            rgc_s.append(jnp.concatenate([st[:, 1:], xx[:, None, :]], axis=1))
            rgh_s.append(h)
        norms = (ffn_norm, ple_norm, final_norm)
        x, xs, *w16 = _ffn_ple(x, p, layer, *norms, (ffn_w_gate, layer), (ffn_w_up, layer), (ffn_w_down, layer),
                               (ple_w_gate, layer), (ple_w_proj, layer), tm=tm, tile_offset=0, xs=xs, ps=ps)
        (x,) = _ffn_ple(x, p, layer, *norms, *[(w, 0) for w in w16], tm=tm, tile_offset=1)
    return (x, xs, jnp.stack(conv_p), jnp.stack(conv_s), jnp.stack(rgc_p), jnp.stack(rgc_s),
            jnp.stack(rgh_p), jnp.stack(rgh_s))


def kernel(x_prompt, x_sample, p_prompt, p_sample, state_conv, state_rg_conv, state_rg_h, mix_norm, ffn_norm, ple_norm, final_norm, sc_w_in, sc_w_conv, sc_w_out, rg_w_x, rg_w_gate, rg_conv_w, rg_conv_b, rg_w_a, rg_b_a, rg_w_i, rg_b_i, rg_lambda, rg_w_out, ffn_w_gate, ffn_w_up, ffn_w_down, ple_w_gate, ple_w_proj):
    bsz, seq, _ = x_prompt.shape
    dec = x_sample.shape[0]

    def rows(v):
        return v.reshape(v.shape[0], 1, v.shape[1])

    weights = (rows(mix_norm), rows(ffn_norm), rows(ple_norm), final_norm.reshape(1, D_MODEL),
               sc_w_in, sc_w_conv, sc_w_out, rg_w_x, rg_w_gate, rg_conv_w, rows(rg_conv_b),
               rg_w_a, rows(rg_b_a), rg_w_i, rows(rg_b_i), rows(rg_lambda),
               rg_w_out, ffn_w_gate, ffn_w_up, ffn_w_down, ple_w_gate, ple_w_proj)
    rs = _row_subblock(ROW_TILE)
    y_p, y_s, conv_p, conv_s, rgc_p, rgc_s, rgh_p, rgh_s = _trunk(
        _interleave_time(x_prompt, rs).reshape(bsz * seq, D_MODEL),
        _interleave_time(p_prompt, rs).reshape(DEPTH, bsz * seq, PLE_DIM),
        x_sample.reshape(dec, D_MODEL), p_sample.reshape(DEPTH, dec, PLE_DIM),
        state_conv, state_rg_conv, state_rg_h, weights, tm=ROW_TILE, seq_len=seq)
    y_p = _deinterleave_time(y_p.reshape(bsz, seq, D_MODEL), rs)
    return (y_p, y_s.reshape(dec, 1, D_MODEL), conv_p, conv_s, rgc_p, rgc_s, rgh_p, rgh_s)
```

```python
import functools

import jax
import jax.numpy as jnp
from jax import lax
from jax.experimental import pallas as pl
from jax.experimental.pallas import tpu as pltpu

D_MODEL = 2048
DEPTH = 4
PLE_DIM = 256
SC_WIDTH = 3
RG_CONV_WIDTH = 4
LRU_BW = 256
LRU_C = 8.0
D_FF = 5632
EPS = 1e-6

SUBLANES = 8
MXU_DIM = 256
ROW_TILE = 1024
ROW_SUBBLOCKS = 2
MIX_CHUNK = LRU_BW
FFN_CHUNK = 512
PLE_CHUNK = 512
FFN_CHUNK_F32 = 256
PLE_CHUNK_F32 = 256
VMEM_LIMIT = 60 * 1024 * 1024

_F32 = jnp.float32
_BF16 = jnp.bfloat16


def _dot(a, b):
    return jnp.dot(a, b, preferred_element_type=_F32)


def _w(ref):
    w = ref[...]
    return w if w.dtype == _BF16 else w.astype(_BF16)


def _mxu_weights(w_refs, w16_refs=None):
    if w16_refs is None:
        return [_w(r) for r in w_refs]
    for src, dst in zip(w_refs, w16_refs):
        dst[...] = _w(src)
    return [dst[...] for dst in w16_refs]


def _rmsnorm(x, g):
    y = x * lax.rsqrt(jnp.mean(x * x, axis=-1, keepdims=True) + EPS)
    return y * g


def _sigmoid(x):
    return jax.nn.sigmoid(x)


def _gelu_tanh(x):
    c2 = 2.0 * 0.7978845608028654
    return x * _sigmoid(x * (c2 + (c2 * 0.044715) * (x * x)))


def _softplus(x):
    return jnp.maximum(x, 0.0) + jnp.log1p(jnp.exp(-jnp.abs(x)))


def _interleave_time(x, rs):
    *lead, t, c = x.shape
    y = x.reshape(*lead, t // rs, SUBLANES, rs // SUBLANES, c)
    return jnp.swapaxes(y, -2, -3).reshape(x.shape)


def _deinterleave_time(x, rs):
    *lead, t, c = x.shape
    y = x.reshape(*lead, t // rs, rs // SUBLANES, SUBLANES, c)
    return jnp.swapaxes(y, -2, -3).reshape(x.shape)


def _time_shifts(buf_ref, vals, prev_tail, width):
    rs = vals.shape[0]
    head = (width - 1) * SUBLANES
    tail = vals[rs - head:rs, :]
    sub = lax.broadcasted_iota(jnp.int32, (SUBLANES, vals.shape[1]), 0)
    for v in range(width - 1):
        grp = slice(v * SUBLANES, (v + 1) * SUBLANES)
        mixed = jnp.where(sub == SUBLANES - 1, prev_tail[grp, :], tail[grp, :])
        buf_ref[grp, :] = pltpu.roll(mixed, 1, 0)
    buf_ref[head:head + rs, :] = vals
    shifted = [buf_ref[head - k * SUBLANES:head - k * SUBLANES + rs, :] for k in range(width - 1, 0, -1)]
    return shifted, tail


def _interleaved_scan(a, b, h_in, hbuf_ref, pbuf_ref):
    rs, tn = a.shape
    groups = rs // SUBLANES
    h = b[0:SUBLANES, :]
    p = a[0:SUBLANES, :]
    hbuf_ref[0:SUBLANES, :] = h
    pbuf_ref[0:SUBLANES, :] = p
    for q in range(1, groups):
        grp = slice(q * SUBLANES, (q + 1) * SUBLANES)
        h = a[grp, :] * h + b[grp, :]
        p = a[grp, :] * p
        hbuf_ref[grp, :] = h
        pbuf_ref[grp, :] = p
    sub = lax.broadcasted_iota(jnp.int32, (SUBLANES, tn), 0)
    carry_s = h_in
    carry = jnp.broadcast_to(h_in, (SUBLANES, tn))
    for s in range(1, SUBLANES):
        carry_s = p[s - 1:s, :] * carry_s + h[s - 1:s, :]
        carry = jnp.where(sub == s, carry_s, carry)
    h_out = p[SUBLANES - 1:SUBLANES, :] * carry_s + h[SUBLANES - 1:SUBLANES, :]
    hs = hbuf_ref[...] + pbuf_ref[...] * jnp.concatenate([carry] * groups, axis=0)
    return hs, h_out


def _row_subblock(tm):
    return tm // ROW_SUBBLOCKS if tm >= ROW_SUBBLOCKS * MXU_DIM else tm


def _row_blocks_of(tm, rows):
    rs = _row_subblock(tm)
    bounds = [(r0, r0 + rs) for r0 in range(0, tm, rs)]
    bounds[-1] = (bounds[-1][0], rows)
    return bounds


class _Rows:
    def __init__(self, x_ref, o_ref, hn_ref, xs_ref=None, os_ref=None):
        self.x_ref, self.o_ref, self.hn_ref, self.xs_ref, self.os_ref = x_ref, o_ref, hn_ref, xs_ref, os_ref
        self.tm, self.rows = x_ref.shape[0], hn_ref.shape[0]
        self.bounds = _row_blocks_of(self.tm, self.rows)

    def segments(self, r0, r1):
        top = min(r1, self.tm)
        segs = [(self.x_ref, self.o_ref, slice(r0, top), slice(r0, top))]
        if r1 > self.tm:
            segs.append((self.xs_ref, self.os_ref, slice(0, r1 - self.tm), slice(self.tm, r1)))
        return segs

    def normalise(self, r0, r1, g_ref, from_output=False):
        for in_ref, out_ref, there, here in self.segments(r0, r1):
            src = out_ref if from_output else in_ref
            self.hn_ref[here, :] = _rmsnorm(src[there, :], g_ref[...]).astype(_BF16)

    def add(self, r0, r1, cols, upd, onto_input):
        off = 0
        for in_ref, out_ref, there, _ in self.segments(r0, r1):
            n = there.stop - there.start
            if onto_input:
                out_ref[there, cols] = in_ref[there, cols] + upd[off:off + n]
            else:
                out_ref[there, cols] += upd[off:off + n]
            off += n

    def blocks(self, first, g_ref, from_output=False):
        if first:
            self.normalise(*self.bounds[0], g_ref, from_output)
        for k, (r0, r1) in enumerate(self.bounds):
            todo = []
            if first and k + 1 < len(self.bounds):
                n0, n1 = self.bounds[k + 1]
                mid = (n0 + min(n1, self.tm)) // 2
                todo = [(n0, mid), (mid, n1)]

            def between():
                if todo:
                    self.normalise(*todo.pop(0), g_ref, from_output)

            yield r0, r1, between
            while todo:
                between()


def _first_and_later_steps(step):
    j = pl.program_id(1)
    pl.when(j == 0)(functools.partial(step, True))
    pl.when(j > 0)(functools.partial(step, False))


def _entering(first, cin_ref, carry_ref, j):
    held = jnp.where(pl.program_id(0) == 0, cin_ref[...], carry_ref[j])
    return jnp.where(first, 0.0, held)


def _col_blocks(lead, offset, shape):
    return pl.BlockSpec((None,) + shape, lambda i, c: (lead, 0, offset + c))


def _row_blocks(lead, shape):
    return pl.BlockSpec((None,) + shape, lambda i, c: (lead, c, 0))


def _row_tile_spec(tm, nt, tile_offset):
    mode = dict(pipeline_mode=pl.Buffered(1)) if nt == 1 else {}
    return pl.BlockSpec((tm, D_MODEL), lambda i, c: (i + tile_offset, 0), **mode)


_ALIAS_ROWS = {0: 0}


def _compiler_params():
    return pltpu.CompilerParams(dimension_semantics=("arbitrary", "arbitrary"), vmem_limit_bytes=VMEM_LIMIT)


def _conv_kernel(*refs, tm, tiles_per_seq, tile_offset, sample):
    if sample:
        (x_ref, g_ref, wb_ref, wc_ref, wx_ref, cw_ref, wo_ref, cin_ref, xs_ref, s0_ref, s1_ref,
         o_ref, st_ref, os_ref, us_ref, wb16_ref, wc16_ref, wx16_ref, wo16_ref,
         hn_ref, ubuf_ref, carry_ref) = refs
        rows = _Rows(x_ref, o_ref, hn_ref, xs_ref, os_ref)
    else:
        (x_ref, g_ref, wb_ref, wc_ref, wx_ref, cw_ref, wo_ref, cin_ref,
         o_ref, st_ref, hn_ref, ubuf_ref, carry_ref) = refs
        rows = _Rows(x_ref, o_ref, hn_ref)
    i, j = pl.program_id(0), pl.program_id(1)
    rs = _row_subblock(tm)

    def step(first_step):
        wb, wc, wx, wo = _mxu_weights((wb_ref, wc_ref, wx_ref, wo_ref),
                                      (wb16_ref, wc16_ref, wx16_ref, wo16_ref) if sample else None)
        cw = cw_ref[...]
        first = ((i + tile_offset) % tiles_per_seq) == 0
        tail = _entering(first, cin_ref, carry_ref, j)
        ups = []
        for r0, r1, between in rows.blocks(first_step, g_ref):
            hn = hn_ref[r0:r1, :]
            b_gate = _dot(hn, wb)
            between()
            c_gate = _dot(hn, wc)
            between()
            ups.append((b_gate, c_gate, _dot(hn, wx)))
        for k, ((r0, r1), (b_gate, c_gate, xin)) in enumerate(zip(rows.bounds, ups)):
            u = c_gate[0:rs] * xin[0:rs]
            (u2, u1), tail = _time_shifts(ubuf_ref.at[k], u, tail, SC_WIDTH)
            conv = u2 * cw[0:1, :] + u1 * cw[1:2, :] + u * cw[2:3, :]
            y = (b_gate[0:rs] * conv).astype(_BF16)
            if r1 - r0 > rs:
                u = c_gate[rs:] * xin[rs:]
                conv = s0_ref[...] * cw[0:1, :] + s1_ref[...] * cw[1:2, :] + u * cw[2:3, :]
                us_ref[...] = u
                y = jnp.concatenate([y, (b_gate[rs:] * conv).astype(_BF16)], axis=0)
            rows.add(r0, r1, slice(None), _dot(y, wo), onto_input=first_step)
        carry_ref[j] = tail
        st_ref[...] = tail

    _first_and_later_steps(step)


def _conv_mixer(x, layer, g, w_conv, w_b, w_c, w_x, w_out, cin, *, tm, seq_len, tile_offset,
                xs=None, state=None):
    m = x.shape[0]
    tn = MIX_CHUNK
    nj = D_MODEL // tn
    j = layer // 2
    head = (SC_WIDTH - 1) * SUBLANES
    rs = _row_subblock(tm)
    nt = 1 if xs is not None else m // tm - tile_offset
    sample = xs is not None
    ms = xs.shape[0] if sample else 0
    x_spec = _row_tile_spec(tm, nt, tile_offset)
    in_specs = [
        x_spec,
        pl.BlockSpec((None, 1, D_MODEL), lambda i, c: (layer, 0, 0)),
        _col_blocks(w_b[1], w_b[2], (D_MODEL, tn)),
        _col_blocks(w_c[1], w_c[2], (D_MODEL, tn)),
        _col_blocks(w_x[1], w_x[2], (D_MODEL, tn)),
        pl.BlockSpec((None, SC_WIDTH, tn), lambda i, c: (j, 0, c)),
        _row_blocks(w_out[1], (tn, D_MODEL)),
        _col_blocks(0, 0, (head, tn)),
    ]
    args = [x, g, w_b[0], w_c[0], w_x[0], w_conv, w_out[0], cin]
    out_shape = [jax.ShapeDtypeStruct((m, D_MODEL), _F32), jax.ShapeDtypeStruct((nt, head, D_MODEL), _F32)]
    out_specs = [x_spec, pl.BlockSpec((None, head, tn), lambda i, c: (i, 0, c))]
    scratch = [pltpu.VMEM((tm + ms, D_MODEL), _BF16), pltpu.VMEM((tm // rs, head + rs, tn), _F32),
               pltpu.VMEM((nj, head, tn), _F32)]
    if sample:
        xs_spec = pl.BlockSpec((ms, D_MODEL), lambda i, c: (0, 0))
        in_specs += [xs_spec, pl.BlockSpec((ms, tn), lambda i, c: (0, c)),
                     pl.BlockSpec((ms, tn), lambda i, c: (0, nj + c))]
        args += [xs, state, state]
        w16 = jax.ShapeDtypeStruct((1, D_MODEL, D_MODEL), _BF16)
        out_shape += [jax.ShapeDtypeStruct((ms, D_MODEL), _F32), jax.ShapeDtypeStruct((ms, D_MODEL), _F32),
                      w16, w16, w16, w16]
        out_specs += [xs_spec, pl.BlockSpec((ms, tn), lambda i, c: (0, c)),
                      _col_blocks(0, 0, (D_MODEL, tn)), _col_blocks(0, 0, (D_MODEL, tn)),
                      _col_blocks(0, 0, (D_MODEL, tn)), _row_blocks(0, (tn, D_MODEL))]
    kern = functools.partial(_conv_kernel, tm=tm, tiles_per_seq=seq_len // tm, tile_offset=tile_offset,
                             sample=sample)
    return pl.pallas_call(
        kern, grid=(nt, nj), in_specs=in_specs, out_specs=out_specs, out_shape=out_shape,
        scratch_shapes=scratch, compiler_params=_compiler_params(), input_output_aliases=_ALIAS_ROWS,
        name=f"conv_mixer_l{layer}_m{nt * tm}",
    )(*args)


def _lru_gates(u, wa, ba, wi, bi, sp_neg_lam):
    ub = u.astype(_BF16)
    r = _sigmoid(_dot(ub, wa) + ba)
    gate_i = _sigmoid(_dot(ub, wi) + bi)
    log_a = (-LRU_C * r) * sp_neg_lam
    a = jnp.exp(log_a)
    m2 = -jnp.tanh(log_a) * (a * a + 1.0)
    mult = jnp.where(m2 > 0.0, m2 * lax.rsqrt(m2), 0.0)
    return a, mult * gate_i * u


def _lru_kernel(*refs, tm, tiles_per_seq, tile_offset, sample):
    if sample:
        (x_ref, g_ref, wg_ref, wx_ref, cw_ref, cb_ref, wa_ref, ba_ref, wi_ref, bi_ref, lam_ref, wo_ref,
         cin_ref, hin_ref, xs_ref, s0_ref, s1_ref, s2_ref, h0_ref,
         o_ref, rgc_ref, hl_ref, os_ref, xxs_ref, hs_ref,
         wg16_ref, wx16_ref, wa16_ref, wi16_ref, wo16_ref,
         hn_ref, xbuf_ref, hbuf_ref, pbuf_ref, carry_ref, hcarry_ref) = refs
        rows = _Rows(x_ref, o_ref, hn_ref, xs_ref, os_ref)
    else:
        (x_ref, g_ref, wg_ref, wx_ref, cw_ref, cb_ref, wa_ref, ba_ref, wi_ref, bi_ref, lam_ref, wo_ref,
         cin_ref, hin_ref, o_ref, rgc_ref, hl_ref,
         hn_ref, xbuf_ref, hbuf_ref, pbuf_ref, carry_ref, hcarry_ref) = refs
        rows = _Rows(x_ref, o_ref, hn_ref)
    i, j = pl.program_id(0), pl.program_id(1)
    rs = _row_subblock(tm)

    def step(first_step):
        wg, wx, wo, wa, wi = _mxu_weights((wg_ref, wx_ref, wo_ref, wa_ref, wi_ref),
                                          (wg16_ref, wx16_ref, wo16_ref, wa16_ref, wi16_ref) if sample else None)
        cw, cb = cw_ref[...], cb_ref[...]
        gate_params = (wa, ba_ref[...], wi, bi_ref[...], _softplus(-lam_ref[...]))
        first = ((i + tile_offset) % tiles_per_seq) == 0
        tail = _entering(first, cin_ref, carry_ref, j)
        h_state = _entering(first, hin_ref, hcarry_ref, j)
        ups = []
        for r0, r1, between in rows.blocks(first_step, g_ref):
            hn = hn_ref[r0:r1, :]
            gate_pre = _dot(hn, wg)
            between()
            ups.append((gate_pre, _dot(hn, wx)))
        for k, ((r0, r1), (gate_pre, xx_all)) in enumerate(zip(rows.bounds, ups)):
            with_sample = r1 - r0 > rs
            gate = _gelu_tanh(gate_pre)
            xx = xx_all[0:rs]
            (x3, x2, x1), tail = _time_shifts(xbuf_ref.at[k], xx, tail, RG_CONV_WIDTH)
            u = (x3 * cw[0:1, :] + x2 * cw[1:2, :] + x1 * cw[2:3, :] + xx * cw[3:4, :]) + cb
            if with_sample:
                xxs = xx_all[rs:]
                us = (s0_ref[...] * cw[0:1, :] + s1_ref[...] * cw[1:2, :] + s2_ref[...] * cw[2:3, :]
                      + xxs * cw[3:4, :]) + cb
                xxs_ref[...] = xxs
                u = jnp.concatenate([u, us], axis=0)
            a, b = _lru_gates(u, *gate_params)
            hs, h_state = _interleaved_scan(a[0:rs], b[0:rs], h_state, hbuf_ref.at[k], pbuf_ref.at[k])
            if with_sample:
                h_new = b[rs:] + a[rs:] * h0_ref[...]
                hs_ref[...] = h_new
                hs = jnp.concatenate([hs, h_new], axis=0)
            rows.add(r0, r1, slice(None), _dot((gate * hs).astype(_BF16), wo), onto_input=first_step)
        carry_ref[j] = tail
        rgc_ref[...] = tail
        hcarry_ref[j] = h_state
        hl_ref[...] = h_state

    _first_and_later_steps(step)


def _lru_mixer(x, layer, g, conv_w, conv_b, b_a, b_i, lam, w_gate, w_x, w_a, w_i, w_out, cin, hin,
               *, tm, seq_len, tile_offset, xs=None, state=None, h0=None):
    m = x.shape[0]
    tn = MIX_CHUNK
    nj = D_MODEL // tn
    j = layer // 2
    head = (RG_CONV_WIDTH - 1) * SUBLANES
    rs = _row_subblock(tm)
    nt = 1 if xs is not None else m // tm - tile_offset
    sample = xs is not None
    ms = xs.shape[0] if sample else 0

    def gate_blocks(lead):
        return pl.BlockSpec((None, None, LRU_BW, LRU_BW), lambda i, c: (lead, c, 0, 0))

    def vec_blocks():
        return pl.BlockSpec((None, 1, tn), lambda i, c: (j, 0, c))

    x_spec = _row_tile_spec(tm, nt, tile_offset)
    in_specs = [
        x_spec,
        pl.BlockSpec((None, 1, D_MODEL), lambda i, c: (layer, 0, 0)),
        _col_blocks(w_gate[1], 0, (D_MODEL, tn)),
        _col_blocks(w_x[1], 0, (D_MODEL, tn)),
        pl.BlockSpec((None, RG_CONV_WIDTH, tn), lambda i, c: (j, 0, c)),
        vec_blocks(),
        gate_blocks(w_a[1]),
        vec_blocks(),
        gate_blocks(w_i[1]),
        vec_blocks(),
        vec_blocks(),
        _row_blocks(w_out[1], (tn, D_MODEL)),
        _col_blocks(0, 0, (head, tn)),
        _col_blocks(0, 0, (1, tn)),
    ]
    args = [x, g, w_gate[0], w_x[0], conv_w, conv_b, w_a[0], b_a, w_i[0], b_i, lam, w_out[0], cin, hin]
    out_shape = [jax.ShapeDtypeStruct((m, D_MODEL), _F32), jax.ShapeDtypeStruct((nt, head, D_MODEL), _F32),
                 jax.ShapeDtypeStruct((nt, 1, D_MODEL), _F32)]
    out_specs = [x_spec, pl.BlockSpec((None, head, tn), lambda i, c: (i, 0, c)),
                 pl.BlockSpec((None, 1, tn), lambda i, c: (i, 0, c))]
    scratch = [pltpu.VMEM((tm + ms, D_MODEL), _BF16), pltpu.VMEM((tm // rs, head + rs, tn), _F32),
               pltpu.VMEM((tm // rs, rs, tn), _F32), pltpu.VMEM((tm // rs, rs, tn), _F32),
               pltpu.VMEM((nj, head, tn), _F32), pltpu.VMEM((nj, 1, tn), _F32)]
    if sample:
        xs_spec = pl.BlockSpec((ms, D_MODEL), lambda i, c: (0, 0))
        col_spec = pl.BlockSpec((ms, tn), lambda i, c: (0, c))
        in_specs += [xs_spec, col_spec, pl.BlockSpec((ms, tn), lambda i, c: (0, nj + c)),
                     pl.BlockSpec((ms, tn), lambda i, c: (0, 2 * nj + c)), col_spec]
        args += [xs, state, state, state, h0]
        w16 = jax.ShapeDtypeStruct((1, D_MODEL, D_MODEL), _BF16)
        g16 = jax.ShapeDtypeStruct((1, nj, LRU_BW, LRU_BW), _BF16)
        row = jax.ShapeDtypeStruct((ms, D_MODEL), _F32)
        out_shape += [row, row, row, w16, w16, g16, g16, w16]
        out_specs += [xs_spec, col_spec, col_spec,
                      _col_blocks(0, 0, (D_MODEL, tn)), _col_blocks(0, 0, (D_MODEL, tn)),
                      gate_blocks(0), gate_blocks(0), _row_blocks(0, (tn, D_MODEL))]
    kern = functools.partial(_lru_kernel, tm=tm, tiles_per_seq=seq_len // tm, tile_offset=tile_offset,
                             sample=sample)
    return pl.pallas_call(
        kern, grid=(nt, nj), in_specs=in_specs, out_specs=out_specs, out_shape=out_shape,
        scratch_shapes=scratch, compiler_params=_compiler_params(), input_output_aliases=_ALIAS_ROWS,
        name=f"lru_mixer_l{layer}_m{nt * tm}",
    )(*args)


def _ffn_ple_kernel(*refs, n_ffn, tp, final, sample):
    if sample:
        (x_ref, gf_ref, wg_ref, wu_ref, wd_ref, gp_ref, pwg_ref, p_ref, pwp_ref, gl_ref, xs_ref, ps_ref,
         o_ref, os_ref, wg16_ref, wu16_ref, wd16_ref, pwg16_ref, pwp16_ref, hn_ref) = refs
    else:
        (x_ref, gf_ref, wg_ref, wu_ref, wd_ref, gp_ref, pwg_ref, p_ref, pwp_ref, gl_ref,
         o_ref, hn_ref) = refs
        xs_ref = os_ref = None
    c = pl.program_id(1)
    rows = _Rows(x_ref, o_ref, hn_ref, xs_ref, os_ref)

    def ffn_step(first):
        wg, wu, wd = _mxu_weights((wg_ref, wu_ref, wd_ref),
                                  (wg16_ref, wu16_ref, wd16_ref) if sample else None)
        for r0, r1, between in rows.blocks(first, gf_ref):
            hn = hn_ref[r0:r1, :]
            gt = _dot(hn, wg)
            between()
            up = _dot(hn, wu)
            between()
            h = (gt * _sigmoid(gt)) * up
            rows.add(r0, r1, slice(None), _dot(h.astype(_BF16), wd), onto_input=first)

    def ple_step(first):
        cols = pl.ds(pl.multiple_of((c - n_ffn) * tp, tp), tp)
        wg, wp = _mxu_weights((pwg_ref, pwp_ref), (pwg16_ref, pwp16_ref) if sample else None)
        for r0, r1, between in rows.blocks(first, gp_ref, from_output=True):
            gate = _sigmoid(_dot(hn_ref[r0:r1, :], wg))
            between()
            p_rows = p_ref[r0:min(r1, rows.tm), :].astype(_BF16)
            if r1 > rows.tm:
                p_rows = jnp.concatenate([p_rows, ps_ref[...].astype(_BF16)], axis=0)
            rows.add(r0, r1, cols, gate * _dot(p_rows, wp), onto_input=False)

    pl.when(c == 0)(functools.partial(ffn_step, True))
    pl.when(jnp.logical_and(c > 0, c < n_ffn))(functools.partial(ffn_step, False))
    pl.when(c == n_ffn)(functools.partial(ple_step, True))
    pl.when(c > n_ffn)(functools.partial(ple_step, False))

    if final:
        @pl.when(c == pl.num_programs(1) - 1)
        def _():
            for _, out_ref, there, _ in rows.segments(0, rows.rows):
                out_ref[there, :] = _rmsnorm(out_ref[there, :], gl_ref[...])


def _ffn_ple(x, p, layer, g_ffn, g_ple, g_final, w_gate, w_up, w_down, pw_gate, pw_proj,
             *, tm, tile_offset, xs=None, ps=None):
    m = x.shape[0]
    nt = 1 if xs is not None else m // tm - tile_offset
    sample = xs is not None
    ms = xs.shape[0] if sample else 0
    tf, tp =(FFN_CHUNK_F32, PLE_CHUNK_F32) if sample else (FFN_CHUNK, PLE_CHUNK)
    n_ffn, n_ple = D_FF // tf, D_MODEL // tp

    def ffn_c(c):
        return jnp.minimum(c, n_ffn - 1)

    def ple_c(c):
        return jnp.maximum(c - n_ffn, 0)

    def ffn_cols(lead):
        return pl.BlockSpec((None, D_MODEL, tf), lambda i, c: (lead, 0, ffn_c(c)))

    def ffn_rows(lead):
        return pl.BlockSpec((None, tf, D_MODEL), lambda i, c: (lead, ffn_c(c), 0))

    def ple_cols(lead, rows):
        return pl.BlockSpec((None, rows, tp), lambda i, c: (lead, 0, ple_c(c)))

    x_spec = _row_tile_spec(tm, nt, tile_offset)
    in_specs = [
        x_spec,
        pl.BlockSpec((None, 1, D_MODEL), lambda i, c: (layer, 0, 0)),
        ffn_cols(w_gate[1]), ffn_cols(w_up[1]), ffn_rows(w_down[1]),
        pl.BlockSpec((None, 1, D_MODEL), lambda i, c: (layer, 0, 0)),
        ple_cols(pw_gate[1], D_MODEL),
        pl.BlockSpec((None, tm, PLE_DIM), lambda i, c: (layer, i + tile_offset, 0)),
        ple_cols(pw_proj[1], PLE_DIM),
        pl.BlockSpec((1, D_MODEL), lambda i, c: (0, 0)),
    ]
    args = [x, g_ffn, w_gate[0], w_up[0], w_down[0], g_ple, pw_gate[0], p, pw_proj[0], g_final]
    out_specs = [x_spec]
    out_shape = [jax.ShapeDtypeStruct((m, D_MODEL), _F32)]
    scratch = [pltpu.VMEM((tm + ms, D_MODEL), _BF16)]
    if sample:
        xs_spec = pl.BlockSpec((ms, D_MODEL), lambda i, c: (0, 0))
        in_specs += [xs_spec, pl.BlockSpec((None, ms, PLE_DIM), lambda i, c: (layer, 0, 0))]
        args += [xs, ps]
        out_specs += [xs_spec, ffn_cols(0), ffn_cols(0), ffn_rows(0), ple_cols(0, D_MODEL), ple_cols(0, PLE_DIM)]
        out_shape += [jax.ShapeDtypeStruct((ms, D_MODEL), _F32),
                      jax.ShapeDtypeStruct((1, D_MODEL, D_FF), _BF16), jax.ShapeDtypeStruct((1, D_MODEL, D_FF), _BF16),
                      jax.ShapeDtypeStruct((1, D_FF, D_MODEL), _BF16),
                      jax.ShapeDtypeStruct((1, D_MODEL, D_MODEL), _BF16),
                      jax.ShapeDtypeStruct((1, PLE_DIM, D_MODEL), _BF16)]
    kern = functools.partial(_ffn_ple_kernel, n_ffn=n_ffn, tp=tp, final=(layer == DEPTH - 1), sample=sample)
    return pl.pallas_call(
        kern, grid=(nt, n_ffn + n_ple), in_specs=in_specs, out_specs=out_specs, out_shape=out_shape,
        scratch_shapes=scratch, compiler_params=_compiler_params(), input_output_aliases=_ALIAS_ROWS,
        name=f"ffn_ple_l{layer}_m{nt * tm}",
    )(*args)


def _trunk(x, p, xs, ps, conv_state, rgc_state, rgh_state, weights, *, tm, seq_len):
    (mix_norm, ffn_norm, ple_norm, final_norm, sc_w_in, sc_w_conv, sc_w_out,
     rg_w_x, rg_w_gate, rg_conv_w, rg_conv_b, rg_w_a, rg_b_a, rg_w_i, rg_b_i, rg_lambda, rg_w_out,
     ffn_w_gate, ffn_w_up, ffn_w_down, ple_w_gate, ple_w_proj) = weights
    ms = xs.shape[0]
    tiles_per_seq = seq_len // tm
    assert tiles_per_seq >= 2, "the head call's row tile must not end a sequence"
    last_tile = slice(tiles_per_seq - 2, None, tiles_per_seq)
    last_time = slice(SUBLANES - 1, None, SUBLANES)
    nj = D_MODEL // MIX_CHUNK
    zeros = functools.partial(jnp.zeros, dtype=_F32)
    geom = dict(tm=tm, seq_len=seq_len)
    conv_p, conv_s, rgc_p, rgc_s, rgh_p, rgh_s = [], [], [], [], [], []
    for layer in range(DEPTH):
        j = layer // 2
        if layer % 2 == 0:
            st = conv_state[j]
            x, st0, xs, u, *w16 = _conv_mixer(
                x, layer, mix_norm, sc_w_conv, (sc_w_in, j, 0), (sc_w_in, j, nj), (sc_w_in, j, 2 * nj),
                (sc_w_out, j), zeros((1, (SC_WIDTH - 1) * SUBLANES, D_MODEL)), tile_offset=0, xs=xs,
                state=st.reshape(ms, (SC_WIDTH - 1) * D_MODEL), **geom)
            wb, wc, wx, wo = w16
            x, st_r = _conv_mixer(x, layer, mix_norm, sc_w_conv, (wb, 0, 0), (wc, 0, 0), (wx, 0, 0), (wo, 0),
                                  st0, tile_offset=1, **geom)
            conv_p.append(st_r[last_tile, last_time])
            conv_s.append(jnp.concatenate([st[:, 1:], u[:, None, :]], axis=1))
        else:
            lru_v = (rg_conv_w, rg_conv_b, rg_b_a, rg_b_i, rg_lambda)
            st = rgc_state[j]
            x, st0, h0, xs, xx, h, *w16 = _lru_mixer(
                x, layer, mix_norm, *lru_v, (rg_w_gate, j), (rg_w_x, j), (rg_w_a, j), (rg_w_i, j), (rg_w_out, j),
                zeros((1, (RG_CONV_WIDTH - 1) * SUBLANES, D_MODEL)), zeros((1, 1, D_MODEL)), tile_offset=0,
                xs=xs, state=st.reshape(ms, (RG_CONV_WIDTH - 1) * D_MODEL), h0=rgh_state[j], **geom)
            x, st_r, h_r = _lru_mixer(x, layer, mix_norm, *lru_v, *[(w, 0) for w in w16], st0, h0,
                                      tile_offset=1, **geom)
            rgc_p.append(st_r[last_tile, last_time])
            rgh_p.append(h_r[last_tile, 0, :])
            rgc_s.append(jnp.concatenate([st[:, 1:], xx[:, None, :]], axis=1))
            rgh_s.append(h)
        norms = (ffn_norm, ple_norm, final_norm)
        x, xs, *w16 = _ffn_ple(x, p, layer, *norms, (ffn_w_gate, layer), (ffn_w_up, layer), (ffn_w_down, layer),
                               (ple_w_gate, layer), (ple_w_proj, layer), tm=tm, tile_offset=0, xs=xs, ps=ps)
        (x,) = _ffn_ple(x, p, layer, *norms, *[(w, 0) for w in w16], tm=tm, tile_offset=1)
    return (x, xs, jnp.stack(conv_p), jnp.stack(conv_s), jnp.stack(rgc_p), jnp.stack(rgc_s),
            jnp.stack(rgh_p), jnp.stack(rgh_s))


def kernel(x_prompt, x_sample, p_prompt, p_sample, state_conv, state_rg_conv, state_rg_h, mix_norm, ffn_norm, ple_norm, final_norm, sc_w_in, sc_w_conv, sc_w_out, rg_w_x, rg_w_gate, rg_conv_w, rg_conv_b, rg_w_a, rg_b_a, rg_w_i, rg_b_i, rg_lambda, rg_w_out, ffn_w_gate, ffn_w_up, ffn_w_down, ple_w_gate, ple_w_proj):
    bsz, seq, _ = x_prompt.shape
    dec = x_sample.shape[0]

    def rows(v):
        return v.reshape(v.shape[0], 1, v.shape[1])

    weights = (rows(mix_norm), rows(ffn_norm), rows(ple_norm), final_norm.reshape(1, D_MODEL),
               sc_w_in, sc_w_conv, sc_w_out, rg_w_x, rg_w_gate, rg_conv_w, rows(rg_conv_b),
               rg_w_a, rows(rg_b_a), rg_w_i, rows(rg_b_i), rows(rg_lambda),
               rg_w_out, ffn_w_gate, ffn_w_up, ffn_w_down, ple_w_gate, ple_w_proj)
    rs = _row_subblock(ROW_TILE)
    y_p, y_s, conv_p, conv_s, rgc_p, rgc_s, rgh_p, rgh_s = _trunk(
        _interleave_time(x_prompt, rs).reshape(bsz * seq, D_MODEL),
        _interleave_time(p_prompt, rs).reshape(DEPTH, bsz * seq, PLE_DIM),
        x_sample.reshape(dec, D_MODEL), p_sample.reshape(DEPTH, dec, PLE_DIM),
        state_conv, state_rg_conv, state_rg_h, weights, tm=ROW_TILE, seq_len=seq)
    y_p = _deinterleave_time(y_p.reshape(bsz, seq, D_MODEL), rs)
    return (y_p, y_s.reshape(dec, 1, D_MODEL), conv_p, conv_s, rgc_p, rgc_s, rgh_p, rgh_s)
```

```python
import functools

import jax
import jax.numpy as jnp
from jax import lax
from jax.experimental import pallas as pl
from jax.experimental.pallas import tpu as pltpu

D_MODEL = 2048
DEPTH = 4
PLE_DIM = 256
SC_WIDTH = 3
RG_CONV_WIDTH = 4
LRU_BW = 256
LRU_C = 8.0
D_FF = 5632
EPS = 1e-6

SUBLANES = 8
MXU_DIM = 256
ROW_TILE = 1024
ROW_SUBBLOCKS = 2
MIX_CHUNK = LRU_BW
FFN_CHUNK = 512
PLE_CHUNK = 512
FFN_CHUNK_F32 = 256
PLE_CHUNK_F32 = 256
VMEM_LIMIT = 60 * 1024 * 1024

_F32 = jnp.float32
_BF16 = jnp.bfloat16


def _dot(a, b):
    return jnp.dot(a, b, preferred_element_type=_F32)


def _w(ref):
    w = ref[...]
    return w if w.dtype == _BF16 else w.astype(_BF16)


def _mxu_weights(w_refs, w16_refs=None):
    if w16_refs is None:
        return [_w(r) for r in w_refs]
    for src, dst in zip(w_refs, w16_refs):
        dst[...] = _w(src)
    return [dst[...] for dst in w16_refs]


def _rmsnorm(x, g):
    y = x * lax.rsqrt(jnp.mean(x * x, axis=-1, keepdims=True) + EPS)
    return y * g


def _sigmoid(x):
    return jax.nn.sigmoid(x)


def _gelu_tanh(x):
    c2 = 2.0 * 0.7978845608028654
    return x * _sigmoid(x * (c2 + (c2 * 0.044715) * (x * x)))


def _softplus(x):
    return jnp.maximum(x, 0.0) + jnp.log1p(jnp.exp(-jnp.abs(x)))


def _interleave_time(x, rs):
    *lead, t, c = x.shape
    y = x.reshape(*lead, t // rs, SUBLANES, rs // SUBLANES, c)
    return jnp.swapaxes(y, -2, -3).reshape(x.shape)


def _deinterleave_time(x, rs):
    *lead, t, c = x.shape
    y = x.reshape(*lead, t // rs, rs // SUBLANES, SUBLANES, c)
    return jnp.swapaxes(y, -2, -3).reshape(x.shape)


def _time_shifts(buf_ref, vals, prev_tail, width):
    rs = vals.shape[0]
    head = (width - 1) * SUBLANES
    tail = vals[rs - head:rs, :]
    sub = lax.broadcasted_iota(jnp.int32, (SUBLANES, vals.shape[1]), 0)
    for v in range(width - 1):
        grp = slice(v * SUBLANES, (v + 1) * SUBLANES)
        mixed = jnp.where(sub == SUBLANES - 1, prev_tail[grp, :], tail[grp, :])
        buf_ref[grp, :] = pltpu.roll(mixed, 1, 0)
    buf_ref[head:head + rs, :] = vals
    shifted = [buf_ref[head - k * SUBLANES:head - k * SUBLANES + rs, :] for k in range(width - 1, 0, -1)]
    return shifted, tail


def _interleaved_scan(a, b, h_in, hbuf_ref, pbuf_ref):
    rs, tn = a.shape
    groups = rs // SUBLANES
    h = b[0:SUBLANES, :]
    p = a[0:SUBLANES, :]
    hbuf_ref[0:SUBLANES, :] = h
    pbuf_ref[0:SUBLANES, :] = p
    for q in range(1, groups):
        grp = slice(q * SUBLANES, (q + 1) * SUBLANES)
        h = a[grp, :] * h + b[grp, :]
        p = a[grp, :] * p
        hbuf_ref[grp, :] = h
        pbuf_ref[grp, :] = p
    sub = lax.broadcasted_iota(jnp.int32, (SUBLANES, tn), 0)
    carry_s = h_in
    carry = jnp.broadcast_to(h_in, (SUBLANES, tn))
    for s in range(1, SUBLANES):
        carry_s = p[s - 1:s, :] * carry_s + h[s - 1:s, :]
        carry = jnp.where(sub == s, carry_s, carry)
    h_out = p[SUBLANES - 1:SUBLANES, :] * carry_s + h[SUBLANES - 1:SUBLANES, :]
    hs = hbuf_ref[...] + pbuf_ref[...] * jnp.concatenate([carry] * groups, axis=0)
    return hs, h_out


def _row_subblock(tm):
    return tm // ROW_SUBBLOCKS if tm >= ROW_SUBBLOCKS * MXU_DIM else tm


def _row_blocks_of(tm, rows):
    rs = _row_subblock(tm)
    bounds = [(r0, r0 + rs) for r0 in range(0, tm, rs)]
    bounds[-1] = (bounds[-1][0], rows)
    return bounds


class _Rows:
    def __init__(self, x_ref, o_ref, hn_ref, xs_ref=None, os_ref=None):
        self.x_ref, self.o_ref, self.hn_ref, self.xs_ref, self.os_ref = x_ref, o_ref, hn_ref, xs_ref, os_ref
        self.tm, self.rows = x_ref.shape[0], hn_ref.shape[0]
        self.bounds = _row_blocks_of(self.tm, self.rows)

    def segments(self, r0, r1):
        top = min(r1, self.tm)
        segs = [(self.x_ref, self.o_ref, slice(r0, top), slice(r0, top))]
        if r1 > self.tm:
            segs.append((self.xs_ref, self.os_ref, slice(0, r1 - self.tm), slice(self.tm, r1)))
        return segs

    def normalise(self, r0, r1, g_ref, from_output=False):
        for in_ref, out_ref, there, here in self.segments(r0, r1):
            src = out_ref if from_output else in_ref
            self.hn_ref[here, :] = _rmsnorm(src[there, :], g_ref[...]).astype(_BF16)

    def add(self, r0, r1, cols, upd, onto_input):
        off = 0
        for in_ref, out_ref, there, _ in self.segments(r0, r1):
            n = there.stop - there.start
            if onto_input:
                out_ref[there, cols] = in_ref[there, cols] + upd[off:off + n]
            else:
                out_ref[there, cols] += upd[off:off + n]
            off += n

    def blocks(self, first, g_ref, from_output=False):
        if first:
            self.normalise(*self.bounds[0], g_ref, from_output)
        for k, (r0, r1) in enumerate(self.bounds):
            todo = []
            if first and k + 1 < len(self.bounds):
                n0, n1 = self.bounds[k + 1]
                mid = (n0 + min(n1, self.tm)) // 2
                todo = [(n0, mid), (mid, n1)]

            def between():
                if todo:
                    self.normalise(*todo.pop(0), g_ref, from_output)

            yield r0, r1, between
            while todo:
                between()


def _first_and_later_steps(step):
    j = pl.program_id(1)
    pl.when(j == 0)(functools.partial(step, True))
    pl.when(j > 0)(functools.partial(step, False))


def _entering(first, cin_ref, carry_ref, j):
    held = jnp.where(pl.program_id(0) == 0, cin_ref[...], carry_ref[j])
    return jnp.where(first, 0.0, held)


def _col_blocks(lead, offset, shape):
    return pl.BlockSpec((None,) + shape, lambda i, c: (lead, 0, offset + c))


def _row_blocks(lead, shape):
    return pl.BlockSpec((None,) + shape, lambda i, c: (lead, c, 0))


def _row_tile_spec(tm, nt, tile_offset):
    mode = dict(pipeline_mode=pl.Buffered(1)) if nt == 1 else {}
    return pl.BlockSpec((tm, D_MODEL), lambda i, c: (i + tile_offset, 0), **mode)


_ALIAS_ROWS = {0: 0}


def _compiler_params():
    return pltpu.CompilerParams(dimension_semantics=("arbitrary", "arbitrary"), vmem_limit_bytes=VMEM_LIMIT)


def _conv_kernel(*refs, tm, tiles_per_seq, tile_offset, sample):
    if sample:
        (x_ref, g_ref, wb_ref, wc_ref, wx_ref, cw_ref, wo_ref, cin_ref, xs_ref, s0_ref, s1_ref,
         o_ref, st_ref, os_ref, us_ref, wb16_ref, wc16_ref, wx16_ref, wo16_ref,
         hn_ref, ubuf_ref, carry_ref) = refs
        rows = _Rows(x_ref, o_ref, hn_ref, xs_ref, os_ref)
    else:
        (x_ref, g_ref, wb_ref, wc_ref, wx_ref, cw_ref, wo_ref, cin_ref,
         o_ref, st_ref, hn_ref, ubuf_ref, carry_ref) = refs
        rows = _Rows(x_ref, o_ref, hn_ref)
    i, j = pl.program_id(0), pl.program_id(1)
    rs = _row_subblock(tm)

    def step(first_step):
        wb, wc, wx, wo = _mxu_weights((wb_ref, wc_ref, wx_ref, wo_ref),
                                      (wb16_ref, wc16_ref, wx16_ref, wo16_ref) if sample else None)
        cw = cw_ref[...]
        first = ((i + tile_offset) % tiles_per_seq) == 0
        tail = _entering(first, cin_ref, carry_ref, j)
        ups = []
        for r0, r1, between in rows.blocks(first_step, g_ref):
            hn = hn_ref[r0:r1, :]
            b_gate = _dot(hn, wb)
            between()
            c_gate = _dot(hn, wc)
            between()
            ups.append((b_gate, c_gate, _dot(hn, wx)))
        for k, ((r0, r1), (b_gate, c_gate, xin)) in enumerate(zip(rows.bounds, ups)):
            u = c_gate[0:rs] * xin[0:rs]
            (u2, u1), tail = _time_shifts(ubuf_ref.at[k], u, tail, SC_WIDTH)
            conv = u2 * cw[0:1, :] + u1 * cw[1:2, :] + u * cw[2:3, :]
            y = (b_gate[0:rs] * conv).astype(_BF16)
            if r1 - r0 > rs:
                u = c_gate[rs:] * xin[rs:]
                conv = s0_ref[...] * cw[0:1, :] + s1_ref[...] * cw[1:2, :] + u * cw[2:3, :]
                us_ref[...] = u
                y = jnp.concatenate([y, (b_gate[rs:] * conv).astype(_BF16)], axis=0)
            rows.add(r0, r1, slice(None), _dot(y, wo), onto_input=first_step)
        carry_ref[j] = tail
        st_ref[...] = tail

    _first_and_later_steps(step)


def _conv_mixer(x, layer, g, w_conv, w_b, w_c, w_x, w_out, cin, *, tm, seq_len, tile_offset,
                xs=None, state=None):
    m = x.shape[0]
    tn = MIX_CHUNK
    nj = D_MODEL // tn
    j = layer // 2
    head = (SC_WIDTH - 1) * SUBLANES
    rs = _row_subblock(tm)
    nt = 1 if xs is not None else m // tm - tile_offset
    sample = xs is not None
    ms = xs.shape[0] if sample else 0
    x_spec = _row_tile_spec(tm, nt, tile_offset)
    in_specs = [
        x_spec,
        pl.BlockSpec((None, 1, D_MODEL), lambda i, c: (layer, 0, 0)),
        _col_blocks(w_b[1], w_b[2], (D_MODEL, tn)),
        _col_blocks(w_c[1], w_c[2], (D_MODEL, tn)),
        _col_blocks(w_x[1], w_x[2], (D_MODEL, tn)),
        pl.BlockSpec((None, SC_WIDTH, tn), lambda i, c: (j, 0, c)),
        _row_blocks(w_out[1], (tn, D_MODEL)),
        _col_blocks(0, 0, (head, tn)),
    ]
    args = [x, g, w_b[0], w_c[0], w_x[0], w_conv, w_out[0], cin]
    out_shape = [jax.ShapeDtypeStruct((m, D_MODEL), _F32), jax.ShapeDtypeStruct((nt, head, D_MODEL), _F32)]
    out_specs = [x_spec, pl.BlockSpec((None, head, tn), lambda i, c: (i, 0, c))]
    scratch = [pltpu.VMEM((tm + ms, D_MODEL), _BF16), pltpu.VMEM((tm // rs, head + rs, tn), _F32),
               pltpu.VMEM((nj, head, tn), _F32)]
    if sample:
        xs_spec = pl.BlockSpec((ms, D_MODEL), lambda i, c: (0, 0))
        in_specs += [xs_spec, pl.BlockSpec((ms, tn), lambda i, c: (0, c)),
                     pl.BlockSpec((ms, tn), lambda i, c: (0, nj + c))]
        args += [xs, state, state]
        w16 = jax.ShapeDtypeStruct((1, D_MODEL, D_MODEL), _BF16)
        out_shape += [jax.ShapeDtypeStruct((ms, D_MODEL), _F32), jax.ShapeDtypeStruct((ms, D_MODEL), _F32),
                      w16, w16, w16, w16]
        out_specs += [xs_spec, pl.BlockSpec((ms, tn), lambda i, c: (0, c)),
                      _col_blocks(0, 0, (D_MODEL, tn)), _col_blocks(0, 0, (D_MODEL, tn)),
                      _col_blocks(0, 0, (D_MODEL, tn)), _row_blocks(0, (tn, D_MODEL))]
    kern = functools.partial(_conv_kernel, tm=tm, tiles_per_seq=seq_len // tm, tile_offset=tile_offset,
                             sample=sample)
    return pl.pallas_call(
        kern, grid=(nt, nj), in_specs=in_specs, out_specs=out_specs, out_shape=out_shape,
        scratch_shapes=scratch, compiler_params=_compiler_params(), input_output_aliases=_ALIAS_ROWS,
        name=f"conv_mixer_l{layer}_m{nt * tm}",
    )(*args)


def _lru_gates(u, wa, ba, wi, bi, sp_neg_lam):
    ub = u.astype(_BF16)
    r = _sigmoid(_dot(ub, wa) + ba)
    gate_i = _sigmoid(_dot(ub, wi) + bi)
    log_a = (-LRU_C * r) * sp_neg_lam
    a = jnp.exp(log_a)
    m2 = -jnp.tanh(log_a) * (a * a + 1.0)
    mult = jnp.where(m2 > 0.0, m2 * lax.rsqrt(m2), 0.0)
    return a, mult * gate_i * u


def _lru_kernel(*refs, tm, tiles_per_seq, tile_offset, sample):
    if sample:
        (x_ref, g_ref, wg_ref, wx_ref, cw_ref, cb_ref, wa_ref, ba_ref, wi_ref, bi_ref, lam_ref, wo_ref,
         cin_ref, hin_ref, xs_ref, s0_ref, s1_ref, s2_ref, h0_ref,
         o_ref, rgc_ref, hl_ref, os_ref, xxs_ref, hs_ref,
         wg16_ref, wx16_ref, wa16_ref, wi16_ref, wo16_ref,
         hn_ref, xbuf_ref, hbuf_ref, pbuf_ref, carry_ref, hcarry_ref) = refs
        rows = _Rows(x_ref, o_ref, hn_ref, xs_ref, os_ref)
    else:
        (x_ref, g_ref, wg_ref, wx_ref, cw_ref, cb_ref, wa_ref, ba_ref, wi_ref, bi_ref, lam_ref, wo_ref,
         cin_ref, hin_ref, o_ref, rgc_ref, hl_ref,
         hn_ref, xbuf_ref, hbuf_ref, pbuf_ref, carry_ref, hcarry_ref) = refs
        rows = _Rows(x_ref, o_ref, hn_ref)
    i, j = pl.program_id(0), pl.program_id(1)
    rs = _row_subblock(tm)

    def step(first_step):
        wg, wx, wo, wa, wi = _mxu_weights((wg_ref, wx_ref, wo_ref, wa_ref, wi_ref),
                                          (wg16_ref, wx16_ref, wo16_ref, wa16_ref, wi16_ref) if sample else None)
        cw, cb = cw_ref[...], cb_ref[...]
        gate_params = (wa, ba_ref[...], wi, bi_ref[...], _softplus(-lam_ref[...]))
        first = ((i + tile_offset) % tiles_per_seq) == 0
        tail = _entering(first, cin_ref, carry_ref, j)
        h_state = _entering(first, hin_ref, hcarry_ref, j)
        carry = {"tail": tail, "h": h_state}
        blk = {}

        def valu_a(k):
            r0, r1 = rows.bounds[k]
            with_sample = r1 - r0 > rs
            gate_pre, xx_all = blk[k]["gate_pre"], blk[k]["xx_all"]
            xx = xx_all[0:rs]
            (x3, x2, x1), carry["tail"] = _time_shifts(xbuf_ref.at[k], xx, carry["tail"], RG_CONV_WIDTH)
            u = (x3 * cw[0:1, :] + x2 * cw[1:2, :] + x1 * cw[2:3, :] + xx * cw[3:4, :]) + cb
            if with_sample:
                xxs = xx_all[rs:]
                us = (s0_ref[...] * cw[0:1, :] + s1_ref[...] * cw[1:2, :] + s2_ref[...] * cw[2:3, :]
                      + xxs * cw[3:4, :]) + cb
                xxs_ref[...] = xxs
                u = jnp.concatenate([u, us], axis=0)
            blk[k]["gate"] = _gelu_tanh(gate_pre)
            blk[k]["ab"] = _lru_gates(u, *gate_params)

        def valu_b(k):
            r0, r1 = rows.bounds[k]
            a, b = blk[k]["ab"]
            hs, carry["h"] = _interleaved_scan(a[0:rs], b[0:rs], carry["h"], hbuf_ref.at[k], pbuf_ref.at[k])
            if r1 - r0 > rs:
                h_new = b[rs:] + a[rs:] * h0_ref[...]
                hs_ref[...] = h_new
                hs = jnp.concatenate([hs, h_new], axis=0)
            blk[k]["y"] = (blk[k]["gate"] * hs).astype(_BF16)

        def down(k):
            r0, r1 = rows.bounds[k]
            rows.add(r0, r1, slice(None), _dot(blk[k]["y"], wo), onto_input=first_step)

        last = len(rows.bounds) - 1
        for k, (r0, r1, between) in enumerate(rows.blocks(first_step, g_ref)):
            hn = hn_ref[r0:r1, :]
            blk[k] = {"gate_pre": _dot(hn, wg)}
            between()
            if k >= 2:
                down(k - 2)
            if k >= 1:
                valu_a(k - 1)
                valu_b(k - 1)
            blk[k]["xx_all"] = _dot(hn, wx)
        if last >= 1:
            down(last - 1)
        valu_a(last)
        valu_b(last)
        down(last)
        tail, h_state = carry["tail"], carry["h"]
        carry_ref[j] = tail
        rgc_ref[...] = tail
        hcarry_ref[j] = h_state
        hl_ref[...] = h_state

    _first_and_later_steps(step)


def _lru_mixer(x, layer, g, conv_w, conv_b, b_a, b_i, lam, w_gate, w_x, w_a, w_i, w_out, cin, hin,
               *, tm, seq_len, tile_offset, xs=None, state=None, h0=None):
    m = x.shape[0]
    tn = MIX_CHUNK
    nj = D_MODEL // tn
    j = layer // 2
    head = (RG_CONV_WIDTH - 1) * SUBLANES
    rs = _row_subblock(tm)
    nt = 1 if xs is not None else m // tm - tile_offset
    sample = xs is not None
    ms = xs.shape[0] if sample else 0

    def gate_blocks(lead):
        return pl.BlockSpec((None, None, LRU_BW, LRU_BW), lambda i, c: (lead, c, 0, 0))

    def vec_blocks():
        return pl.BlockSpec((None, 1, tn), lambda i, c: (j, 0, c))

    x_spec = _row_tile_spec(tm, nt, tile_offset)
    in_specs = [
        x_spec,
        pl.BlockSpec((None, 1, D_MODEL), lambda i, c: (layer, 0, 0)),
        _col_blocks(w_gate[1], 0, (D_MODEL, tn)),
        _col_blocks(w_x[1], 0, (D_MODEL, tn)),
        pl.BlockSpec((None, RG_CONV_WIDTH, tn), lambda i, c: (j, 0, c)),
        vec_blocks(),
        gate_blocks(w_a[1]),
        vec_blocks(),
        gate_blocks(w_i[1]),
        vec_blocks(),
        vec_blocks(),
        _row_blocks(w_out[1], (tn, D_MODEL)),
        _col_blocks(0, 0, (head, tn)),
        _col_blocks(0, 0, (1, tn)),
    ]
    args = [x, g, w_gate[0], w_x[0], conv_w, conv_b, w_a[0], b_a, w_i[0], b_i, lam, w_out[0], cin, hin]
    out_shape = [jax.ShapeDtypeStruct((m, D_MODEL), _F32), jax.ShapeDtypeStruct((nt, head, D_MODEL), _F32),
                 jax.ShapeDtypeStruct((nt, 1, D_MODEL), _F32)]
    out_specs = [x_spec, pl.BlockSpec((None, head, tn), lambda i, c: (i, 0, c)),
                 pl.BlockSpec((None, 1, tn), lambda i, c: (i, 0, c))]
    scratch = [pltpu.VMEM((tm + ms, D_MODEL), _BF16), pltpu.VMEM((tm // rs, head + rs, tn), _F32),
               pltpu.VMEM((tm // rs, rs, tn), _F32), pltpu.VMEM((tm // rs, rs, tn), _F32),
               pltpu.VMEM((nj, head, tn), _F32), pltpu.VMEM((nj, 1, tn), _F32)]
    if sample:
        xs_spec = pl.BlockSpec((ms, D_MODEL), lambda i, c: (0, 0))
        col_spec = pl.BlockSpec((ms, tn), lambda i, c: (0, c))
        in_specs += [xs_spec, col_spec, pl.BlockSpec((ms, tn), lambda i, c: (0, nj + c)),
                     pl.BlockSpec((ms, tn), lambda i, c: (0, 2 * nj + c)), col_spec]
        args += [xs, state, state, state, h0]
        w16 = jax.ShapeDtypeStruct((1, D_MODEL, D_MODEL), _BF16)
        g16 = jax.ShapeDtypeStruct((1, nj, LRU_BW, LRU_BW), _BF16)
        row = jax.ShapeDtypeStruct((ms, D_MODEL), _F32)
        out_shape += [row, row, row, w16, w16, g16, g16, w16]
        out_specs += [xs_spec, col_spec, col_spec,
                      _col_blocks(0, 0, (D_MODEL, tn)), _col_blocks(0, 0, (D_MODEL, tn)),
                      gate_blocks(0), gate_blocks(0), _row_blocks(0, (tn, D_MODEL))]
    kern = functools.partial(_lru_kernel, tm=tm, tiles_per_seq=seq_len // tm, tile_offset=tile_offset,
                             sample=sample)
    return pl.pallas_call(
        kern, grid=(nt, nj), in_specs=in_specs, out_specs=out_specs, out_shape=out_shape,
        scratch_shapes=scratch, compiler_params=_compiler_params(), input_output_aliases=_ALIAS_ROWS,
        name=f"lru_mixer_l{layer}_m{nt * tm}",
    )(*args)


def _ffn_ple_kernel(*refs, n_ffn, tp, final, sample):
    if sample:
        (x_ref, gf_ref, wg_ref, wu_ref, wd_ref, gp_ref, pwg_ref, p_ref, pwp_ref, gl_ref, xs_ref, ps_ref,
         o_ref, os_ref, wg16_ref, wu16_ref, wd16_ref, pwg16_ref, pwp16_ref, hn_ref) = refs
    else:
        (x_ref, gf_ref, wg_ref, wu_ref, wd_ref, gp_ref, pwg_ref, p_ref, pwp_ref, gl_ref,
         o_ref, hn_ref) = refs
        xs_ref = os_ref = None
    c = pl.program_id(1)
    rows = _Rows(x_ref, o_ref, hn_ref, xs_ref, os_ref)

    def ffn_step(first):
        wg, wu, wd = _mxu_weights((wg_ref, wu_ref, wd_ref),
                                  (wg16_ref, wu16_ref, wd16_ref) if sample else None)
        for r0, r1, between in rows.blocks(first, gf_ref):
            hn = hn_ref[r0:r1, :]
            gt = _dot(hn, wg)
            between()
            up = _dot(hn, wu)
            between()
            h = (gt * _sigmoid(gt)) * up
            rows.add(r0, r1, slice(None), _dot(h.astype(_BF16), wd), onto_input=first)

    def ple_step(first):
        cols = pl.ds(pl.multiple_of((c - n_ffn) * tp, tp), tp)
        wg, wp = _mxu_weights((pwg_ref, pwp_ref), (pwg16_ref, pwp16_ref) if sample else None)
        for r0, r1, between in rows.blocks(first, gp_ref, from_output=True):
            gate = _sigmoid(_dot(hn_ref[r0:r1, :], wg))
            between()
            p_rows = p_ref[r0:min(r1, rows.tm), :].astype(_BF16)
            if r1 > rows.tm:
                p_rows = jnp.concatenate([p_rows, ps_ref[...].astype(_BF16)], axis=0)
            rows.add(r0, r1, cols, gate * _dot(p_rows, wp), onto_input=False)

    pl.when(c == 0)(functools.partial(ffn_step, True))
    pl.when(jnp.logical_and(c > 0, c < n_ffn))(functools.partial(ffn_step, False))
    pl.when(c == n_ffn)(functools.partial(ple_step, True))
    pl.when(c > n_ffn)(functools.partial(ple_step, False))

    if final:
        @pl.when(c == pl.num_programs(1) - 1)
        def _():
            for _, out_ref, there, _ in rows.segments(0, rows.rows):
                out_ref[there, :] = _rmsnorm(out_ref[there, :], gl_ref[...])


def _ffn_ple(x, p, layer, g_ffn, g_ple, g_final, w_gate, w_up, w_down, pw_gate, pw_proj,
             *, tm, tile_offset, xs=None, ps=None):
    m = x.shape[0]
    nt = 1 if xs is not None else m // tm - tile_offset
    sample = xs is not None
    ms = xs.shape[0] if sample else 0
    tf, tp =(FFN_CHUNK_F32, PLE_CHUNK_F32) if sample else (FFN_CHUNK, PLE_CHUNK)
    n_ffn, n_ple = D_FF // tf, D_MODEL // tp

    def ffn_c(c):
        return jnp.minimum(c, n_ffn - 1)

    def ple_c(c):
        return jnp.maximum(c - n_ffn, 0)

    def ffn_cols(lead):
        return pl.BlockSpec((None, D_MODEL, tf), lambda i, c: (lead, 0, ffn_c(c)))

    def ffn_rows(lead):
        return pl.BlockSpec((None, tf, D_MODEL), lambda i, c: (lead, ffn_c(c), 0))

    def ple_cols(lead, rows):
        return pl.BlockSpec((None, rows, tp), lambda i, c: (lead, 0, ple_c(c)))

    x_spec = _row_tile_spec(tm, nt, tile_offset)
    in_specs = [
        x_spec,
        pl.BlockSpec((None, 1, D_MODEL), lambda i, c: (layer, 0, 0)),
        ffn_cols(w_gate[1]), ffn_cols(w_up[1]), ffn_rows(w_down[1]),
        pl.BlockSpec((None, 1, D_MODEL), lambda i, c: (layer, 0, 0)),
        ple_cols(pw_gate[1], D_MODEL),
        pl.BlockSpec((None, tm, PLE_DIM), lambda i, c: (layer, i + tile_offset, 0)),
        ple_cols(pw_proj[1], PLE_DIM),
        pl.BlockSpec((1, D_MODEL), lambda i, c: (0, 0)),
    ]
    args = [x, g_ffn, w_gate[0], w_up[0], w_down[0], g_ple, pw_gate[0], p, pw_proj[0], g_final]
    out_specs = [x_spec]
    out_shape = [jax.ShapeDtypeStruct((m, D_MODEL), _F32)]
    scratch = [pltpu.VMEM((tm + ms, D_MODEL), _BF16)]
    if sample:
        xs_spec = pl.BlockSpec((ms, D_MODEL), lambda i, c: (0, 0))
        in_specs += [xs_spec, pl.BlockSpec((None, ms, PLE_DIM), lambda i, c: (layer, 0, 0))]
        args += [xs, ps]
        out_specs += [xs_spec, ffn_cols(0), ffn_cols(0), ffn_rows(0), ple_cols(0, D_MODEL), ple_cols(0, PLE_DIM)]
        out_shape += [jax.ShapeDtypeStruct((ms, D_MODEL), _F32),
                      jax.ShapeDtypeStruct((1, D_MODEL, D_FF), _BF16), jax.ShapeDtypeStruct((1, D_MODEL, D_FF), _BF16),
                      jax.ShapeDtypeStruct((1, D_FF, D_MODEL), _BF16),
                      jax.ShapeDtypeStruct((1, D_MODEL, D_MODEL), _BF16),
                      jax.ShapeDtypeStruct((1, PLE_DIM, D_MODEL), _BF16)]
    kern = functools.partial(_ffn_ple_kernel, n_ffn=n_ffn, tp=tp, final=(layer == DEPTH - 1), sample=sample)
    return pl.pallas_call(
        kern, grid=(nt, n_ffn + n_ple), in_specs=in_specs, out_specs=out_specs, out_shape=out_shape,
        scratch_shapes=scratch, compiler_params=_compiler_params(), input_output_aliases=_ALIAS_ROWS,
        name=f"ffn_ple_l{layer}_m{nt * tm}",
    )(*args)


def _trunk(x, p, xs, ps, conv_state, rgc_state, rgh_state, weights, *, tm, seq_len):
    (mix_norm, ffn_norm, ple_norm, final_norm, sc_w_in, sc_w_conv, sc_w_out,
     rg_w_x, rg_w_gate, rg_conv_w, rg_conv_b, rg_w_a, rg_b_a, rg_w_i, rg_b_i, rg_lambda, rg_w_out,
     ffn_w_gate, ffn_w_up, ffn_w_down, ple_w_gate, ple_w_proj) = weights
    ms = xs.shape[0]
    tiles_per_seq = seq_len // tm
    assert tiles_per_seq >= 2, "the head call's row tile must not end a sequence"
    last_tile = slice(tiles_per_seq - 2, None, tiles_per_seq)
    last_time = slice(SUBLANES - 1, None, SUBLANES)
    nj = D_MODEL // MIX_CHUNK
    zeros = functools.partial(jnp.zeros, dtype=_F32)
    geom = dict(tm=tm, seq_len=seq_len)
    conv_p, conv_s, rgc_p, rgc_s, rgh_p, rgh_s = [], [], [], [], [], []
    for layer in range(DEPTH):
        j = layer // 2
        if layer % 2 == 0:
            st = conv_state[j]
            x, st0, xs, u, *w16 = _conv_mixer(
                x, layer, mix_norm, sc_w_conv, (sc_w_in, j, 0), (sc_w_in, j, nj), (sc_w_in, j, 2 * nj),
                (sc_w_out, j), zeros((1, (SC_WIDTH - 1) * SUBLANES, D_MODEL)), tile_offset=0, xs=xs,
                state=st.reshape(ms, (SC_WIDTH - 1) * D_MODEL), **geom)
            wb, wc, wx, wo = w16
            x, st_r = _conv_mixer(x, layer, mix_norm, sc_w_conv, (wb, 0, 0), (wc, 0, 0), (wx, 0, 0), (wo, 0),
                                  st0, tile_offset=1, **geom)
            conv_p.append(st_r[last_tile, last_time])
            conv_s.append(jnp.concatenate([st[:, 1:], u[:, None, :]], axis=1))
        else:
            lru_v = (rg_conv_w, rg_conv_b, rg_b_a, rg_b_i, rg_lambda)
            st = rgc_state[j]
            x, st0, h0, xs, xx, h, *w16 = _lru_mixer(
                x, layer, mix_norm, *lru_v, (rg_w_gate, j), (rg_w_x, j), (rg_w_a, j), (rg_w_i, j), (rg_w_out, j),
                zeros((1, (RG_CONV_WIDTH - 1) * SUBLANES, D_MODEL)), zeros((1, 1, D_MODEL)), tile_offset=0,
                xs=xs, state=st.reshape(ms, (RG_CONV_WIDTH - 1) * D_MODEL), h0=rgh_state[j], **geom)
            x, st_r, h_r = _lru_mixer(x, layer, mix_norm, *lru_v, *[(w, 0) for w in w16], st0, h0,
                                      tile_offset=1, **geom)
            rgc_p.append(st_r[last_tile, last_time])
            rgh_p.append(h_r[last_tile, 0, :])
            rgc_s.append(jnp.concatenate([st[:, 1:], xx[:, None, :]], axis=1))
            rgh_s.append(h)
        norms = (ffn_norm, ple_norm, final_norm)
        x, xs, *w16 = _ffn_ple(x, p, layer, *norms, (ffn_w_gate, layer), (ffn_w_up, layer), (ffn_w_down, layer),
                               (ple_w_gate, layer), (ple_w_proj, layer), tm=tm, tile_offset=0, xs=xs, ps=ps)
        (x,) = _ffn_ple(x, p, layer, *norms, *[(w, 0) for w in w16], tm=tm, tile_offset=1)
    return (x, xs, jnp.stack(conv_p), jnp.stack(conv_s), jnp.stack(rgc_p), jnp.stack(rgc_s),
            jnp.stack(rgh_p), jnp.stack(rgh_s))


def kernel(x_prompt, x_sample, p_prompt, p_sample, state_conv, state_rg_conv, state_rg_h, mix_norm, ffn_norm, ple_norm, final_norm, sc_w_in, sc_w_conv, sc_w_out, rg_w_x, rg_w_gate, rg_conv_w, rg_conv_b, rg_w_a, rg_b_a, rg_w_i, rg_b_i, rg_lambda, rg_w_out, ffn_w_gate, ffn_w_up, ffn_w_down, ple_w_gate, ple_w_proj):
    bsz, seq, _ = x_prompt.shape
    dec = x_sample.shape[0]

    def rows(v):
        return v.reshape(v.shape[0], 1, v.shape[1])

    weights = (rows(mix_norm), rows(ffn_norm), rows(ple_norm), final_norm.reshape(1, D_MODEL),
               sc_w_in, sc_w_conv, sc_w_out, rg_w_x, rg_w_gate, rg_conv_w, rows(rg_conv_b),
               rg_w_a, rows(rg_b_a), rg_w_i, rows(rg_b_i), rows(rg_lambda),
               rg_w_out, ffn_w_gate, ffn_w_up, ffn_w_down, ple_w_gate, ple_w_proj)
    rs = _row_subblock(ROW_TILE)
    y_p, y_s, conv_p, conv_s, rgc_p, rgc_s, rgh_p, rgh_s = _trunk(
        _interleave_time(x_prompt, rs).reshape(bsz * seq, D_MODEL),
        _interleave_time(p_prompt, rs).reshape(DEPTH, bsz * seq, PLE_DIM),
        x_sample.reshape(dec, D_MODEL), p_sample.reshape(DEPTH, dec, PLE_DIM),
        state_conv, state_rg_conv, state_rg_h, weights, tm=ROW_TILE, seq_len=seq)
    y_p = _deinterleave_time(y_p.reshape(bsz, seq, D_MODEL), rs)
    return (y_p, y_s.reshape(dec, 1, D_MODEL), conv_p, conv_s, rgc_p, rgc_s, rgh_p, rgh_s)
```

```python
import functools

import jax
import jax.numpy as jnp
from jax import lax
from jax.experimental import pallas as pl
from jax.experimental.pallas import tpu as pltpu

D_MODEL = 2048
DEPTH = 4
PLE_DIM = 256
SC_WIDTH = 3
RG_CONV_WIDTH = 4
LRU_BW = 256
LRU_C = 8.0
D_FF = 5632
EPS = 1e-6

SUBLANES = 8
MXU_DIM = 256
ROW_TILE = 1024
ROW_SUBBLOCKS = 2
MIX_CHUNK = LRU_BW
FFN_CHUNK = 512
PLE_CHUNK = 512
FFN_CHUNK_F32 = 256
PLE_CHUNK_F32 = 256
VMEM_LIMIT = 60 * 1024 * 1024

_F32 = jnp.float32
_BF16 = jnp.bfloat16


def _dot(a, b):
    return jnp.dot(a, b, preferred_element_type=_F32)


def _w(ref):
    w = ref[...]
    return w if w.dtype == _BF16 else w.astype(_BF16)


def _mxu_weights(w_refs, w16_refs=None):
    if w16_refs is None:
        return [_w(r) for r in w_refs]
    for src, dst in zip(w_refs, w16_refs):
        dst[...] = _w(src)
    return [dst[...] for dst in w16_refs]


def _rmsnorm(x, g):
    y = x * lax.rsqrt(jnp.mean(x * x, axis=-1, keepdims=True) + EPS)
    return y * g


def _sigmoid(x):
    return jax.nn.sigmoid(x)


def _gelu_tanh(x):
    c2 = 2.0 * 0.7978845608028654
    return x * _sigmoid(x * (c2 + (c2 * 0.044715) * (x * x)))


def _softplus(x):
    return jnp.maximum(x, 0.0) + jnp.log1p(jnp.exp(-jnp.abs(x)))


def _interleave_time(x, rs):
    *lead, t, c = x.shape
    y = x.reshape(*lead, t // rs, SUBLANES, rs // SUBLANES, c)
    return jnp.swapaxes(y, -2, -3).reshape(x.shape)


def _deinterleave_time(x, rs):
    *lead, t, c = x.shape
    y = x.reshape(*lead, t // rs, rs // SUBLANES, SUBLANES, c)
    return jnp.swapaxes(y, -2, -3).reshape(x.shape)


def _time_shifts(buf_ref, vals, prev_tail, width):
    rs = vals.shape[0]
    head = (width - 1) * SUBLANES
    tail = vals[rs - head:rs, :]
    sub = lax.broadcasted_iota(jnp.int32, (SUBLANES, vals.shape[1]), 0)
    for v in range(width - 1):
        grp = slice(v * SUBLANES, (v + 1) * SUBLANES)
        mixed = jnp.where(sub == SUBLANES - 1, prev_tail[grp, :], tail[grp, :])
        buf_ref[grp, :] = pltpu.roll(mixed, 1, 0)
    buf_ref[head:head + rs, :] = vals
    shifted = [buf_ref[head - k * SUBLANES:head - k * SUBLANES + rs, :] for k in range(width - 1, 0, -1)]
    return shifted, tail


def _interleaved_scan(a, b, h_in, hbuf_ref, pbuf_ref):
    rs, tn = a.shape
    groups = rs // SUBLANES
    h = b[0:SUBLANES, :]
    p = a[0:SUBLANES, :]
    hbuf_ref[0:SUBLANES, :] = h
    pbuf_ref[0:SUBLANES, :] = p
    for q in range(1, groups):
        grp = slice(q * SUBLANES, (q + 1) * SUBLANES)
        h = a[grp, :] * h + b[grp, :]
        p = a[grp, :] * p
        hbuf_ref[grp, :] = h
        pbuf_ref[grp, :] = p
    sub = lax.broadcasted_iota(jnp.int32, (SUBLANES, tn), 0)
    carry_s = h_in
    carry = jnp.broadcast_to(h_in, (SUBLANES, tn))
    for s in range(1, SUBLANES):
        carry_s = p[s - 1:s, :] * carry_s + h[s - 1:s, :]
        carry = jnp.where(sub == s, carry_s, carry)
    h_out = p[SUBLANES - 1:SUBLANES, :] * carry_s + h[SUBLANES - 1:SUBLANES, :]
    hs = hbuf_ref[...] + pbuf_ref[...] * jnp.concatenate([carry] * groups, axis=0)
    return hs, h_out


def _row_subblock(tm):
    return tm // ROW_SUBBLOCKS if tm >= ROW_SUBBLOCKS * MXU_DIM else tm


def _row_blocks_of(tm, rows):
    rs = _row_subblock(tm)
    bounds = [(r0, r0 + rs) for r0 in range(0, tm, rs)]
    bounds[-1] = (bounds[-1][0], rows)
    return bounds


class _Rows:
    def __init__(self, x_ref, o_ref, hn_ref, xs_ref=None, os_ref=None):
        self.x_ref, self.o_ref, self.hn_ref, self.xs_ref, self.os_ref = x_ref, o_ref, hn_ref, xs_ref, os_ref
        self.tm, self.rows = x_ref.shape[0], hn_ref.shape[0]
        self.bounds = _row_blocks_of(self.tm, self.rows)

    def segments(self, r0, r1):
        top = min(r1, self.tm)
        segs = [(self.x_ref, self.o_ref, slice(r0, top), slice(r0, top))]
        if r1 > self.tm:
            segs.append((self.xs_ref, self.os_ref, slice(0, r1 - self.tm), slice(self.tm, r1)))
        return segs

    def normalise(self, r0, r1, g_ref, from_output=False):
        for in_ref, out_ref, there, here in self.segments(r0, r1):
            src = out_ref if from_output else in_ref
            self.hn_ref[here, :] = _rmsnorm(src[there, :], g_ref[...]).astype(_BF16)

    def add(self, r0, r1, cols, upd, onto_input):
        off = 0
        for in_ref, out_ref, there, _ in self.segments(r0, r1):
            n = there.stop - there.start
            if onto_input:
                out_ref[there, cols] = in_ref[there, cols] + upd[off:off + n]
            else:
                out_ref[there, cols] += upd[off:off + n]
            off += n

    def blocks(self, first, g_ref, from_output=False):
        if first:
            self.normalise(*self.bounds[0], g_ref, from_output)
        for k, (r0, r1) in enumerate(self.bounds):
            todo = []
            if first and k + 1 < len(self.bounds):
                n0, n1 = self.bounds[k + 1]
                mid = (n0 + min(n1, self.tm)) // 2
                todo = [(n0, mid), (mid, n1)]

            def between():
                if todo:
                    self.normalise(*todo.pop(0), g_ref, from_output)

            yield r0, r1, between
            while todo:
                between()


def _first_and_later_steps(step):
    j = pl.program_id(1)
    pl.when(j == 0)(functools.partial(step, True))
    pl.when(j > 0)(functools.partial(step, False))


def _entering(first, cin_ref, carry_ref, j):
    held = jnp.where(pl.program_id(0) == 0, cin_ref[...], carry_ref[j])
    return jnp.where(first, 0.0, held)


def _col_blocks(lead, offset, shape):
    return pl.BlockSpec((None,) + shape, lambda i, c: (lead, 0, offset + c))


def _row_blocks(lead, shape):
    return pl.BlockSpec((None,) + shape, lambda i, c: (lead, c, 0))


def _row_tile_spec(tm, nt, tile_offset):
    mode = dict(pipeline_mode=pl.Buffered(1)) if nt == 1 else {}
    return pl.BlockSpec((tm, D_MODEL), lambda i, c: (i + tile_offset, 0), **mode)


_ALIAS_ROWS = {0: 0}


def _compiler_params():
    return pltpu.CompilerParams(dimension_semantics=("arbitrary", "arbitrary"), vmem_limit_bytes=VMEM_LIMIT)


def _conv_kernel(*refs, tm, tiles_per_seq, tile_offset, sample):
    if sample:
        (x_ref, g_ref, wb_ref, wc_ref, wx_ref, cw_ref, wo_ref, cin_ref, xs_ref, s0_ref, s1_ref,
         o_ref, st_ref, os_ref, us_ref, wb16_ref, wc16_ref, wx16_ref, wo16_ref,
         hn_ref, ubuf_ref, carry_ref) = refs
        rows = _Rows(x_ref, o_ref, hn_ref, xs_ref, os_ref)
    else:
        (x_ref, g_ref, wb_ref, wc_ref, wx_ref, cw_ref, wo_ref, cin_ref,
         o_ref, st_ref, hn_ref, ubuf_ref, carry_ref) = refs
        rows = _Rows(x_ref, o_ref, hn_ref)
    i, j = pl.program_id(0), pl.program_id(1)
    rs = _row_subblock(tm)

    def step(first_step):
        wb, wc, wx, wo = _mxu_weights((wb_ref, wc_ref, wx_ref, wo_ref),
                                      (wb16_ref, wc16_ref, wx16_ref, wo16_ref) if sample else None)
        cw = cw_ref[...]
        first = ((i + tile_offset) % tiles_per_seq) == 0
        tail = _entering(first, cin_ref, carry_ref, j)
        ups = []
        for r0, r1, between in rows.blocks(first_step, g_ref):
            hn = hn_ref[r0:r1, :]
            b_gate = _dot(hn, wb)
            between()
            c_gate = _dot(hn, wc)
            between()
            ups.append((b_gate, c_gate, _dot(hn, wx)))
        for k, ((r0, r1), (b_gate, c_gate, xin)) in enumerate(zip(rows.bounds, ups)):
            u = c_gate[0:rs] * xin[0:rs]
            (u2, u1), tail = _time_shifts(ubuf_ref.at[k], u, tail, SC_WIDTH)
            conv = u2 * cw[0:1, :] + u1 * cw[1:2, :] + u * cw[2:3, :]
            y = (b_gate[0:rs] * conv).astype(_BF16)
            if r1 - r0 > rs:
                u = c_gate[rs:] * xin[rs:]
                conv = s0_ref[...] * cw[0:1, :] + s1_ref[...] * cw[1:2, :] + u * cw[2:3, :]
                us_ref[...] = u
                y = jnp.concatenate([y, (b_gate[rs:] * conv).astype(_BF16)], axis=0)
            rows.add(r0, r1, slice(None), _dot(y, wo), onto_input=first_step)
        carry_ref[j] = tail
        st_ref[...] = tail

    _first_and_later_steps(step)


def _conv_mixer(x, layer, g, w_conv, w_b, w_c, w_x, w_out, cin, *, tm, seq_len, tile_offset,
                xs=None, state=None):
    m = x.shape[0]
    tn = MIX_CHUNK
    nj = D_MODEL // tn
    j = layer // 2
    head = (SC_WIDTH - 1) * SUBLANES
    rs = _row_subblock(tm)
    nt = 1 if xs is not None else m // tm - tile_offset
    sample = xs is not None
    ms = xs.shape[0] if sample else 0
    x_spec = _row_tile_spec(tm, nt, tile_offset)
    in_specs = [
        x_spec,
        pl.BlockSpec((None, 1, D_MODEL), lambda i, c: (layer, 0, 0)),
        _col_blocks(w_b[1], w_b[2], (D_MODEL, tn)),
        _col_blocks(w_c[1], w_c[2], (D_MODEL, tn)),
        _col_blocks(w_x[1], w_x[2], (D_MODEL, tn)),
        pl.BlockSpec((None, SC_WIDTH, tn), lambda i, c: (j, 0, c)),
        _row_blocks(w_out[1], (tn, D_MODEL)),
        _col_blocks(0, 0, (head, tn)),
    ]
    args = [x, g, w_b[0], w_c[0], w_x[0], w_conv, w_out[0], cin]
    out_shape = [jax.ShapeDtypeStruct((m, D_MODEL), _F32), jax.ShapeDtypeStruct((nt, head, D_MODEL), _F32)]
    out_specs = [x_spec, pl.BlockSpec((None, head, tn), lambda i, c: (i, 0, c))]
    scratch = [pltpu.VMEM((tm + ms, D_MODEL), _BF16), pltpu.VMEM((tm // rs, head + rs, tn), _F32),
               pltpu.VMEM((nj, head, tn), _F32)]
    if sample:
        xs_spec = pl.BlockSpec((ms, D_MODEL), lambda i, c: (0, 0))
        in_specs += [xs_spec, pl.BlockSpec((ms, tn), lambda i, c: (0, c)),
                     pl.BlockSpec((ms, tn), lambda i, c: (0, nj + c))]
        args += [xs, state, state]
        w16 = jax.ShapeDtypeStruct((1, D_MODEL, D_MODEL), _BF16)
        out_shape += [jax.ShapeDtypeStruct((ms, D_MODEL), _F32), jax.ShapeDtypeStruct((ms, D_MODEL), _F32),
                      w16, w16, w16, w16]
        out_specs += [xs_spec, pl.BlockSpec((ms, tn), lambda i, c: (0, c)),
                      _col_blocks(0, 0, (D_MODEL, tn)), _col_blocks(0, 0, (D_MODEL, tn)),
                      _col_blocks(0, 0, (D_MODEL, tn)), _row_blocks(0, (tn, D_MODEL))]
    kern = functools.partial(_conv_kernel, tm=tm, tiles_per_seq=seq_len // tm, tile_offset=tile_offset,
                             sample=sample)
    return pl.pallas_call(
        kern, grid=(nt, nj), in_specs=in_specs, out_specs=out_specs, out_shape=out_shape,
        scratch_shapes=scratch, compiler_params=_compiler_params(), input_output_aliases=_ALIAS_ROWS,
        name=f"conv_mixer_l{layer}_m{nt * tm}",
    )(*args)


def _lru_gates(u, wa, ba, wi, bi, sp_neg_lam):
    ub = u.astype(_BF16)
    r = _sigmoid(_dot(ub, wa) + ba)
    gate_i = _sigmoid(_dot(ub, wi) + bi)
    log_a = (-LRU_C * r) * sp_neg_lam
    a = jnp.exp(log_a)
    m2 = -jnp.tanh(log_a) * (a * a + 1.0)
    mult = jnp.where(m2 > 0.0, m2 * lax.rsqrt(m2), 0.0)
    return a, mult * gate_i * u


def _lru_kernel(*refs, tm, tiles_per_seq, tile_offset, sample):
    if sample:
        (x_ref, g_ref, wg_ref, wx_ref, cw_ref, cb_ref, wa_ref, ba_ref, wi_ref, bi_ref, lam_ref, wo_ref,
         cin_ref, hin_ref, xs_ref, s0_ref, s1_ref, s2_ref, h0_ref,
         o_ref, rgc_ref, hl_ref, os_ref, xxs_ref, hs_ref,
         wg16_ref, wx16_ref, wa16_ref, wi16_ref, wo16_ref,
         hn_ref, xbuf_ref, hbuf_ref, pbuf_ref, carry_ref, hcarry_ref) = refs
        rows = _Rows(x_ref, o_ref, hn_ref, xs_ref, os_ref)
    else:
        (x_ref, g_ref, wg_ref, wx_ref, cw_ref, cb_ref, wa_ref, ba_ref, wi_ref, bi_ref, lam_ref, wo_ref,
         cin_ref, hin_ref, o_ref, rgc_ref, hl_ref,
         hn_ref, xbuf_ref, hbuf_ref, pbuf_ref, carry_ref, hcarry_ref) = refs
        rows = _Rows(x_ref, o_ref, hn_ref)
    i, j = pl.program_id(0), pl.program_id(1)
    rs = _row_subblock(tm)

    def step(first_step):
        wg, wx, wo, wa, wi = _mxu_weights((wg_ref, wx_ref, wo_ref, wa_ref, wi_ref),
                                          (wg16_ref, wx16_ref, wo16_ref, wa16_ref, wi16_ref) if sample else None)
        cw, cb = cw_ref[...], cb_ref[...]
        gate_params = (wa, ba_ref[...], wi, bi_ref[...], _softplus(-lam_ref[...]))
        first = ((i + tile_offset) % tiles_per_seq) == 0
        tail = _entering(first, cin_ref, carry_ref, j)
        h_state = _entering(first, hin_ref, hcarry_ref, j)
        carry = {"tail": tail, "h": h_state}
        blk = {}

        def valu_a(k):
            r0, r1 = rows.bounds[k]
            with_sample = r1 - r0 > rs
            gate_pre, xx_all = blk[k]["gate_pre"], blk[k]["xx_all"]
            xx = xx_all[0:rs]
            (x3, x2, x1), carry["tail"] = _time_shifts(xbuf_ref.at[k], xx, carry["tail"], RG_CONV_WIDTH)
            u = (x3 * cw[0:1, :] + x2 * cw[1:2, :] + x1 * cw[2:3, :] + xx * cw[3:4, :]) + cb
            if with_sample:
                xxs = xx_all[rs:]
                us = (s0_ref[...] * cw[0:1, :] + s1_ref[...] * cw[1:2, :] + s2_ref[...] * cw[2:3, :]
                      + xxs * cw[3:4, :]) + cb
                xxs_ref[...] = xxs
                u = jnp.concatenate([u, us], axis=0)
            blk[k]["gate"] = _gelu_tanh(gate_pre)
            blk[k]["ab"] = _lru_gates(u, *gate_params)

        def valu_b(k):
            r0, r1 = rows.bounds[k]
            a, b = blk[k]["ab"]
            hs, carry["h"] = _interleaved_scan(a[0:rs], b[0:rs], carry["h"], hbuf_ref.at[k], pbuf_ref.at[k])
            if r1 - r0 > rs:
                h_new = b[rs:] + a[rs:] * h0_ref[...]
                hs_ref[...] = h_new
                hs = jnp.concatenate([hs, h_new], axis=0)
            blk[k]["y"] = (blk[k]["gate"] * hs).astype(_BF16)

        def down(k, cols=slice(None)):
            r0, r1 = rows.bounds[k]
            rows.add(r0, r1, cols, _dot(blk[k]["y"], wo[:, cols]), onto_input=first_step)

        last = len(rows.bounds) - 1
        for k, (r0, r1, between) in enumerate(rows.blocks(first_step, g_ref)):
            hn = hn_ref[r0:r1, :]
            blk[k] = {"gate_pre": _dot(hn, wg)}
            between()
            if k >= 2:
                down(k - 2)
            if k >= 1:
                valu_a(k - 1)
                valu_b(k - 1)
            blk[k]["xx_all"] = _dot(hn, wx)
        half = D_MODEL // 2
        if last >= 1:
            down(last - 1, slice(0, half))
        valu_a(last)
        if last >= 1:
            down(last - 1, slice(half, D_MODEL))
        valu_b(last)
        down(last)
        tail, h_state = carry["tail"], carry["h"]
        carry_ref[j] = tail
        rgc_ref[...] = tail
        hcarry_ref[j] = h_state
        hl_ref[...] = h_state

    _first_and_later_steps(step)


def _lru_mixer(x, layer, g, conv_w, conv_b, b_a, b_i, lam, w_gate, w_x, w_a, w_i, w_out, cin, hin,
               *, tm, seq_len, tile_offset, xs=None, state=None, h0=None):
    m = x.shape[0]
    tn = MIX_CHUNK
    nj = D_MODEL // tn
    j = layer // 2
    head = (RG_CONV_WIDTH - 1) * SUBLANES
    rs = _row_subblock(tm)
    nt = 1 if xs is not None else m // tm - tile_offset
    sample = xs is not None
    ms = xs.shape[0] if sample else 0

    def gate_blocks(lead):
        return pl.BlockSpec((None, None, LRU_BW, LRU_BW), lambda i, c: (lead, c, 0, 0))

    def vec_blocks():
        return pl.BlockSpec((None, 1, tn), lambda i, c: (j, 0, c))

    x_spec = _row_tile_spec(tm, nt, tile_offset)
    in_specs = [
        x_spec,
        pl.BlockSpec((None, 1, D_MODEL), lambda i, c: (layer, 0, 0)),
        _col_blocks(w_gate[1], 0, (D_MODEL, tn)),
        _col_blocks(w_x[1], 0, (D_MODEL, tn)),
        pl.BlockSpec((None, RG_CONV_WIDTH, tn), lambda i, c: (j, 0, c)),
        vec_blocks(),
        gate_blocks(w_a[1]),
        vec_blocks(),
        gate_blocks(w_i[1]),
        vec_blocks(),
        vec_blocks(),
        _row_blocks(w_out[1], (tn, D_MODEL)),
        _col_blocks(0, 0, (head, tn)),
        _col_blocks(0, 0, (1, tn)),
    ]
    args = [x, g, w_gate[0], w_x[0], conv_w, conv_b, w_a[0], b_a, w_i[0], b_i, lam, w_out[0], cin, hin]
    out_shape = [jax.ShapeDtypeStruct((m, D_MODEL), _F32), jax.ShapeDtypeStruct((nt, head, D_MODEL), _F32),
                 jax.ShapeDtypeStruct((nt, 1, D_MODEL), _F32)]
    out_specs = [x_spec, pl.BlockSpec((None, head, tn), lambda i, c: (i, 0, c)),
                 pl.BlockSpec((None, 1, tn), lambda i, c: (i, 0, c))]
    scratch = [pltpu.VMEM((tm + ms, D_MODEL), _BF16), pltpu.VMEM((tm // rs, head + rs, tn), _F32),
               pltpu.VMEM((tm // rs, rs, tn), _F32), pltpu.VMEM((tm // rs, rs, tn), _F32),
               pltpu.VMEM((nj, head, tn), _F32), pltpu.VMEM((nj, 1, tn), _F32)]
    if sample:
        xs_spec = pl.BlockSpec((ms, D_MODEL), lambda i, c: (0, 0))
        col_spec = pl.BlockSpec((ms, tn), lambda i, c: (0, c))
        in_specs += [xs_spec, col_spec, pl.BlockSpec((ms, tn), lambda i, c: (0, nj + c)),
                     pl.BlockSpec((ms, tn), lambda i, c: (0, 2 * nj + c)), col_spec]
        args += [xs, state, state, state, h0]
        w16 = jax.ShapeDtypeStruct((1, D_MODEL, D_MODEL), _BF16)
        g16 = jax.ShapeDtypeStruct((1, nj, LRU_BW, LRU_BW), _BF16)
        row = jax.ShapeDtypeStruct((ms, D_MODEL), _F32)
        out_shape += [row, row, row, w16, w16, g16, g16, w16]
        out_specs += [xs_spec, col_spec, col_spec,
                      _col_blocks(0, 0, (D_MODEL, tn)), _col_blocks(0, 0, (D_MODEL, tn)),
                      gate_blocks(0), gate_blocks(0), _row_blocks(0, (tn, D_MODEL))]
    kern = functools.partial(_lru_kernel, tm=tm, tiles_per_seq=seq_len // tm, tile_offset=tile_offset,
                             sample=sample)
    return pl.pallas_call(
        kern, grid=(nt, nj), in_specs=in_specs, out_specs=out_specs, out_shape=out_shape,
        scratch_shapes=scratch, compiler_params=_compiler_params(), input_output_aliases=_ALIAS_ROWS,
        name=f"lru_mixer_l{layer}_m{nt * tm}",
    )(*args)


def _ffn_ple_kernel(*refs, n_ffn, tp, final, sample):
    if sample:
        (x_ref, gf_ref, wg_ref, wu_ref, wd_ref, gp_ref, pwg_ref, p_ref, pwp_ref, gl_ref, xs_ref, ps_ref,
         o_ref, os_ref, wg16_ref, wu16_ref, wd16_ref, pwg16_ref, pwp16_ref, hn_ref) = refs
    else:
        (x_ref, gf_ref, wg_ref, wu_ref, wd_ref, gp_ref, pwg_ref, p_ref, pwp_ref, gl_ref,
         o_ref, hn_ref) = refs
        xs_ref = os_ref = None
    c = pl.program_id(1)
    rows = _Rows(x_ref, o_ref, hn_ref, xs_ref, os_ref)

    def ffn_step(first):
        wg, wu, wd = _mxu_weights((wg_ref, wu_ref, wd_ref),
                                  (wg16_ref, wu16_ref, wd16_ref) if sample else None)
        for r0, r1, between in rows.blocks(first, gf_ref):
            hn = hn_ref[r0:r1, :]
            gt = _dot(hn, wg)
            between()
            up = _dot(hn, wu)
            between()
            h = (gt * _sigmoid(gt)) * up
            rows.add(r0, r1, slice(None), _dot(h.astype(_BF16), wd), onto_input=first)

    def ple_step(first):
        cols = pl.ds(pl.multiple_of((c - n_ffn) * tp, tp), tp)
        wg, wp = _mxu_weights((pwg_ref, pwp_ref), (pwg16_ref, pwp16_ref) if sample else None)
        for r0, r1, between in rows.blocks(first, gp_ref, from_output=True):
            gate = _sigmoid(_dot(hn_ref[r0:r1, :], wg))
            between()
            p_rows = p_ref[r0:min(r1, rows.tm), :].astype(_BF16)
            if r1 > rows.tm:
                p_rows = jnp.concatenate([p_rows, ps_ref[...].astype(_BF16)], axis=0)
            rows.add(r0, r1, cols, gate * _dot(p_rows, wp), onto_input=False)

    pl.when(c == 0)(functools.partial(ffn_step, True))
    pl.when(jnp.logical_and(c > 0, c < n_ffn))(functools.partial(ffn_step, False))
    pl.when(c == n_ffn)(functools.partial(ple_step, True))
    pl.when(c > n_ffn)(functools.partial(ple_step, False))

    if final:
        @pl.when(c == pl.num_programs(1) - 1)
        def _():
            for _, out_ref, there, _ in rows.segments(0, rows.rows):
                out_ref[there, :] = _rmsnorm(out_ref[there, :], gl_ref[...])


def _ffn_ple(x, p, layer, g_ffn, g_ple, g_final, w_gate, w_up, w_down, pw_gate, pw_proj,
             *, tm, tile_offset, xs=None, ps=None):
    m = x.shape[0]
    nt = 1 if xs is not None else m // tm - tile_offset
    sample = xs is not None
    ms = xs.shape[0] if sample else 0
    tf, tp =(FFN_CHUNK_F32, PLE_CHUNK_F32) if sample else (FFN_CHUNK, PLE_CHUNK)
    n_ffn, n_ple = D_FF // tf, D_MODEL // tp

    def ffn_c(c):
        return jnp.minimum(c, n_ffn - 1)

    def ple_c(c):
        return jnp.maximum(c - n_ffn, 0)

    def ffn_cols(lead):
        return pl.BlockSpec((None, D_MODEL, tf), lambda i, c: (lead, 0, ffn_c(c)))

    def ffn_rows(lead):
        return pl.BlockSpec((None, tf, D_MODEL), lambda i, c: (lead, ffn_c(c), 0))

    def ple_cols(lead, rows):
        return pl.BlockSpec((None, rows, tp), lambda i, c: (lead, 0, ple_c(c)))

    x_spec = _row_tile_spec(tm, nt, tile_offset)
    in_specs = [
        x_spec,
        pl.BlockSpec((None, 1, D_MODEL), lambda i, c: (layer, 0, 0)),
        ffn_cols(w_gate[1]), ffn_cols(w_up[1]), ffn_rows(w_down[1]),
        pl.BlockSpec((None, 1, D_MODEL), lambda i, c: (layer, 0, 0)),
        ple_cols(pw_gate[1], D_MODEL),
        pl.BlockSpec((None, tm, PLE_DIM), lambda i, c: (layer, i + tile_offset, 0)),
        ple_cols(pw_proj[1], PLE_DIM),
        pl.BlockSpec((1, D_MODEL), lambda i, c: (0, 0)),
    ]
    args = [x, g_ffn, w_gate[0], w_up[0], w_down[0], g_ple, pw_gate[0], p, pw_proj[0], g_final]
    out_specs = [x_spec]
    out_shape = [jax.ShapeDtypeStruct((m, D_MODEL), _F32)]
    scratch = [pltpu.VMEM((tm + ms, D_MODEL), _BF16)]
    if sample:
        xs_spec = pl.BlockSpec((ms, D_MODEL), lambda i, c: (0, 0))
        in_specs += [xs_spec, pl.BlockSpec((None, ms, PLE_DIM), lambda i, c: (layer, 0, 0))]
        args += [xs, ps]
        out_specs += [xs_spec, ffn_cols(0), ffn_cols(0), ffn_rows(0), ple_cols(0, D_MODEL), ple_cols(0, PLE_DIM)]
        out_shape += [jax.ShapeDtypeStruct((ms, D_MODEL), _F32),
                      jax.ShapeDtypeStruct((1, D_MODEL, D_FF), _BF16), jax.ShapeDtypeStruct((1, D_MODEL, D_FF), _BF16),
                      jax.ShapeDtypeStruct((1, D_FF, D_MODEL), _BF16),
                      jax.ShapeDtypeStruct((1, D_MODEL, D_MODEL), _BF16),
                      jax.ShapeDtypeStruct((1, PLE_DIM, D_MODEL), _BF16)]
    kern = functools.partial(_ffn_ple_kernel, n_ffn=n_ffn, tp=tp, final=(layer == DEPTH - 1), sample=sample)
    return pl.pallas_call(
        kern, grid=(nt, n_ffn + n_ple), in_specs=in_specs, out_specs=out_specs, out_shape=out_shape,
        scratch_shapes=scratch, compiler_params=_compiler_params(), input_output_aliases=_ALIAS_ROWS,
        name=f"ffn_ple_l{layer}_m{nt * tm}",
    )(*args)


def _trunk(x, p, xs, ps, conv_state, rgc_state, rgh_state, weights, *, tm, seq_len):
    (mix_norm, ffn_norm, ple_norm, final_norm, sc_w_in, sc_w_conv, sc_w_out,
     rg_w_x, rg_w_gate, rg_conv_w, rg_conv_b, rg_w_a, rg_b_a, rg_w_i, rg_b_i, rg_lambda, rg_w_out,
     ffn_w_gate, ffn_w_up, ffn_w_down, ple_w_gate, ple_w_proj) = weights
    ms = xs.shape[0]
    tiles_per_seq = seq_len // tm
    assert tiles_per_seq >= 2, "the head call's row tile must not end a sequence"
    last_tile = slice(tiles_per_seq - 2, None, tiles_per_seq)
    last_time = slice(SUBLANES - 1, None, SUBLANES)
    nj = D_MODEL // MIX_CHUNK
    zeros = functools.partial(jnp.zeros, dtype=_F32)
    geom = dict(tm=tm, seq_len=seq_len)
    conv_p, conv_s, rgc_p, rgc_s, rgh_p, rgh_s = [], [], [], [], [], []
    for layer in range(DEPTH):
        j = layer // 2
        if layer % 2 == 0:
            st = conv_state[j]
            x, st0, xs, u, *w16 = _conv_mixer(
                x, layer, mix_norm, sc_w_conv, (sc_w_in, j, 0), (sc_w_in, j, nj), (sc_w_in, j, 2 * nj),
                (sc_w_out, j), zeros((1, (SC_WIDTH - 1) * SUBLANES, D_MODEL)), tile_offset=0, xs=xs,
                state=st.reshape(ms, (SC_WIDTH - 1) * D_MODEL), **geom)
            wb, wc, wx, wo = w16
            x, st_r = _conv_mixer(x, layer, mix_norm, sc_w_conv, (wb, 0, 0), (wc, 0, 0), (wx, 0, 0), (wo, 0),
                                  st0, tile_offset=1, **geom)
            conv_p.append(st_r[last_tile, last_time])
            conv_s.append(jnp.concatenate([st[:, 1:], u[:, None, :]], axis=1))
        else:
            lru_v = (rg_conv_w, rg_conv_b, rg_b_a, rg_b_i, rg_lambda)
            st = rgc_state[j]
            x, st0, h0, xs, xx, h, *w16 = _lru_mixer(
                x, layer, mix_norm, *lru_v, (rg_w_gate, j), (rg_w_x, j), (rg_w_a, j), (rg_w_i, j), (rg_w_out, j),
                zeros((1, (RG_CONV_WIDTH - 1) * SUBLANES, D_MODEL)), zeros((1, 1, D_MODEL)), tile_offset=0,
                xs=xs, state=st.reshape(ms, (RG_CONV_WIDTH - 1) * D_MODEL), h0=rgh_state[j], **geom)
            x, st_r, h_r = _lru_mixer(x, layer, mix_norm, *lru_v, *[(w, 0) for w in w16], st0, h0,
                                      tile_offset=1, **geom)
            rgc_p.append(st_r[last_tile, last_time])
            rgh_p.append(h_r[last_tile, 0, :])
            rgc_s.append(jnp.concatenate([st[:, 1:], xx[:, None, :]], axis=1))
            rgh_s.append(h)
        norms = (ffn_norm, ple_norm, final_norm)
        x, xs, *w16 = _ffn_ple(x, p, layer, *norms, (ffn_w_gate, layer), (ffn_w_up, layer), (ffn_w_down, layer),
                               (ple_w_gate, layer), (ple_w_proj, layer), tm=tm, tile_offset=0, xs=xs, ps=ps)
        (x,) = _ffn_ple(x, p, layer, *norms, *[(w, 0) for w in w16], tm=tm, tile_offset=1)
    return (x, xs, jnp.stack(conv_p), jnp.stack(conv_s), jnp.stack(rgc_p), jnp.stack(rgc_s),
            jnp.stack(rgh_p), jnp.stack(rgh_s))


def kernel(x_prompt, x_sample, p_prompt, p_sample, state_conv, state_rg_conv, state_rg_h, mix_norm, ffn_norm, ple_norm, final_norm, sc_w_in, sc_w_conv, sc_w_out, rg_w_x, rg_w_gate, rg_conv_w, rg_conv_b, rg_w_a, rg_b_a, rg_w_i, rg_b_i, rg_lambda, rg_w_out, ffn_w_gate, ffn_w_up, ffn_w_down, ple_w_gate, ple_w_proj):
    bsz, seq, _ = x_prompt.shape
    dec = x_sample.shape[0]

    def rows(v):
        return v.reshape(v.shape[0], 1, v.shape[1])

    weights = (rows(mix_norm), rows(ffn_norm), rows(ple_norm), final_norm.reshape(1, D_MODEL),
               sc_w_in, sc_w_conv, sc_w_out, rg_w_x, rg_w_gate, rg_conv_w, rows(rg_conv_b),
               rg_w_a, rows(rg_b_a), rg_w_i, rows(rg_b_i), rows(rg_lambda),
               rg_w_out, ffn_w_gate, ffn_w_up, ffn_w_down, ple_w_gate, ple_w_proj)
    rs = _row_subblock(ROW_TILE)
    y_p, y_s, conv_p, conv_s, rgc_p, rgc_s, rgh_p, rgh_s = _trunk(
        _interleave_time(x_prompt, rs).reshape(bsz * seq, D_MODEL),
        _interleave_time(p_prompt, rs).reshape(DEPTH, bsz * seq, PLE_DIM),
        x_sample.reshape(dec, D_MODEL), p_sample.reshape(DEPTH, dec, PLE_DIM),
        state_conv, state_rg_conv, state_rg_h, weights, tm=ROW_TILE, seq_len=seq)
    y_p = _deinterleave_time(y_p.reshape(bsz, seq, D_MODEL), rs)
    return (y_p, y_s.reshape(dec, 1, D_MODEL), conv_p, conv_s, rgc_p, rgc_s, rgh_p, rgh_s)
```
